```python
import math
import jax
import jax.numpy as jnp
from jax import lax
import numpy as np

D_MODEL = 2048
BATCH = 8
SEQ = 8192
DEPTH = 4

N_BRANCHES = 4
BRANCH_WIDTH = D_MODEL // 4
D_FF = 4 * D_MODEL
NORM_EPS = 1e-6

S5_GROUP = 16
S5_GROUPS = BRANCH_WIDTH // S5_GROUP
S5_STATE = 64
S5_DT_MIN = 1e-3
S5_DT_MAX = 1e-1

GDN_HEAD_DIM = 128
GDN_HEADS = BRANCH_WIDTH // GDN_HEAD_DIM
GDN_CONV = 4
GDN_CHUNK = 64

SWA_HEAD_DIM = 64
SWA_HEADS = BRANCH_WIDTH // SWA_HEAD_DIM
SWA_KV_HEADS = SWA_HEADS // 4
WINDOW = 128
SWA_BLOCK = 128
T5_BUCKETS = 32
T5_MAX_DISTANCE = 128

MLA_HEADS = BRANCH_WIDTH // 128
MLA_Q_RANK = 384
MLA_KV_RANK = 512
MLA_NOPE = 128
MLA_ROPE = 64
MLA_V = 128
ROPE_THETA = 10000.0
ATTN_BLOCK = 128

IN_SPLITS = (
    BRANCH_WIDTH,
    3 * BRANCH_WIDTH,
    BRANCH_WIDTH,
    2 * GDN_HEADS,
    2 * GDN_HEADS,
    SWA_HEADS * SWA_HEAD_DIM,
    2 * SWA_KV_HEADS * SWA_HEAD_DIM,
    MLA_Q_RANK,
    MLA_KV_RANK,
    MLA_ROPE,
    N_BRANCHES * D_MODEL,
)
D_IN = sum(IN_SPLITS)

kernel_name = 'hybrid_parallel_gated_encoder'


def rms_norm(x, gain):
    xf = x.astype(jnp.float32)
    y = xf * lax.rsqrt(jnp.mean(xf * xf, axis=-1, keepdims=True) + NORM_EPS)
    return (y * gain.astype(jnp.float32)).astype(x.dtype)


def _cmul(ar, ai, br, bi):
    return ar * br - ai * bi, ar * bi + ai * br


def _linear_recurrence_op(left, right):
    a1r, a1i, b1r, b1i = left
    a2r, a2i, b2r, b2i = right
    ar, ai = _cmul(a2r, a2i, a1r, a1i)
    br, bi = _cmul(a2r, a2i, b1r, b1i)
    return ar, ai, br + b2r, bi + b2i


def s5_direction(u, lam_re, lam_im, log_step, b_re, b_im, c_re, c_im, reverse):
    f = lambda t: t.astype(jnp.float32)
    lam_re = jnp.minimum(f(lam_re), -1e-4)
    lam_im = f(lam_im)
    dt = jnp.exp(f(log_step))[:, None]
    mag = jnp.exp(lam_re * dt)
    abar_r = mag * jnp.cos(lam_im * dt)
    abar_i = mag * jnp.sin(lam_im * dt)
    den = lam_re * lam_re + lam_im * lam_im
    xr = abar_r - 1.0
    xi = abar_i
    coef_r = (xr * lam_re + xi * lam_im) / den
    coef_i = (xi * lam_re - xr * lam_im) / den
    b_re, b_im = f(b_re), f(b_im)
    bbar_r = coef_r[..., None] * b_re - coef_i[..., None] * b_im
    bbar_i = coef_r[..., None] * b_im + coef_i[..., None] * b_re
    bu_r = jnp.einsum('blgh,gph->blgp', u, bbar_r)
    bu_i = jnp.einsum('blgh,gph->blgp', u, bbar_i)
    a_r = jnp.broadcast_to(abar_r, bu_r.shape)
    a_i = jnp.broadcast_to(abar_i, bu_i.shape)
    _, _, s_r, s_i = lax.associative_scan(
        _linear_recurrence_op, (a_r, a_i, bu_r, bu_i), reverse=reverse, axis=1)
    return (jnp.einsum('blgp,ghp->blgh', s_r, f(c_re))
            - jnp.einsum('blgp,ghp->blgh', s_i, f(c_im)))


def s5_mixer(u, lam_re, lam_im, log_step, b_re, b_im, c_re, c_im, d_skip, w_glu, b_glu):
    B_, L, W = u.shape
    uf = u.astype(jnp.float32)
    ug = uf.reshape(B_, L, S5_GROUPS, S5_GROUP)
    y = s5_direction(ug, lam_re[0], lam_im[0], log_step[0], b_re[0], b_im[0],
                     c_re[0], c_im[0], reverse=False)
    y = y + s5_direction(ug, lam_re[1], lam_im[1], log_step[1], b_re[1], b_im[1],
                         c_re[1], c_im[1], reverse=True)
    y = y.reshape(B_, L, W) + d_skip.astype(jnp.float32) * uf
    y = jax.nn.gelu(y)
    y = y * jax.nn.sigmoid(y @ w_glu.astype(jnp.float32) + b_glu.astype(jnp.float32))
    return y.astype(u.dtype)


def l2_normalize(x):
    xf = x.astype(jnp.float32)
    return xf * lax.rsqrt(jnp.sum(xf * xf, axis=-1, keepdims=True) + 1e-6)


def depthwise_conv_centred(x, w):
    K = w.shape[0]
    return lax.conv_general_dilated(
        x, w.astype(x.dtype)[:, None, :], window_strides=(1,),
        padding=[(K // 2, K - 1 - K // 2)],
        dimension_numbers=('NWC', 'WIO', 'NWC'),
        feature_group_count=x.shape[-1])


def gated_delta_rule_chunked(q, k, v, g, beta):
    B_, L, H, dk = q.shape
    dv = v.shape[-1]
    C = GDN_CHUNK
    N = L // C

    def chunks(t):
        return t.reshape(B_, N, C, H, -1).transpose(0, 3, 1, 2, 4)

    q, k, v = chunks(q), chunks(k), chunks(v)
    g = jnp.cumsum(g.reshape(B_, N, C, H).transpose(0, 3, 1, 2), axis=-1)
    beta = beta.reshape(B_, N, C, H).transpose(0, 3, 1, 2)[..., None]
    k_beta = k * beta
    lower = jnp.tril(jnp.ones((C, C), dtype=bool))
    strict = jnp.tril(jnp.ones((C, C), dtype=bool), -1)
    decay = jnp.exp(jnp.where(lower, g[..., :, None] - g[..., None, :], -jnp.inf))
    a_mat = jnp.where(strict, jnp.einsum('bhnik,bhnjk->bhnij', k_beta, k) * decay, 0.0)
    a_mat = a_mat + jnp.eye(C, dtype=q.dtype)
    rhs = jnp.concatenate([v * beta, k_beta * jnp.exp(g)[..., None]], axis=-1)
    sol = lax.linalg.triangular_solve(a_mat, rhs, left_side=True, lower=True,
                                      unit_diagonal=True)
    u, w = sol[..., :dv], sol[..., dv:]
    attn = jnp.einsum('bhnik,bhnjk->bhnij', q, k) * decay

    def step(S, inp):
        q_c, k_c, u_c, w_c, g_c, a_c = inp
        v_new = u_c - jnp.einsum('bhck,bhkv->bhcv', w_c, S)
        o = (jnp.einsum('bhck,bhkv->bhcv', q_c * jnp.exp(g_c)[..., None], S)
             + jnp.einsum('bhcj,bhjv->bhcv', a_c, v_new))
        g_last = g_c[..., -1]
        S = (S * jnp.exp(g_last)[..., None, None]
             + jnp.einsum('bhck,bhcv->bhkv', k_c * jnp.exp(g_last[..., None] - g_c)[..., None], v_new))
        return S, o

    s0 = jnp.zeros((B_, H, dk, dv), q.dtype)
    xs = tuple(jnp.moveaxis(t, 2, 0) for t in (q, k, u, w, g, attn))
    _, o = lax.scan(step, s0, xs)
    return jnp.moveaxis(o, 0, 2).transpose(0, 2, 3, 1, 4).reshape(B_, L, H, dv)


def gdn_mixer(qkv, z, beta_logits, decay_logits, conv_w, a_log, dt_bias, o_gain):
    B_, L, _ = qkv.shape
    H, Dh = GDN_HEADS, GDN_HEAD_DIM
    qkv = jax.nn.silu(depthwise_conv_centred(qkv, conv_w))
    q, k, v = jnp.split(qkv, 3, axis=-1)
    q = l2_normalize(q.reshape(B_, L, H, Dh)) * (Dh ** -0.5)
    k = l2_normalize(k.reshape(B_, L, H, Dh))
    v = v.reshape(B_, L, H, Dh).astype(jnp.float32)
    beta = jax.nn.sigmoid(beta_logits.astype(jnp.float32)).reshape(B_, L, 2, H)
    g = -jnp.exp(a_log.astype(jnp.float32)) * jax.nn.softplus(
        decay_logits.astype(jnp.float32).reshape(B_, L, 2, H) + dt_bias.astype(jnp.float32))
    o_fwd = gated_delta_rule_chunked(q, k, v, g[:, :, 0], beta[:, :, 0])
    flip = lambda t: jnp.flip(t, axis=1)
    o_bwd = flip(gated_delta_rule_chunked(flip(q), flip(k), flip(v),
                                          flip(g[:, :, 1]), flip(beta[:, :, 1])))
    o = rms_norm(o_fwd + o_bwd, o_gain)
    o = o * jax.nn.silu(z.astype(jnp.float32).reshape(B_, L, H, Dh))
    return o.reshape(B_, L, H * Dh).astype(z.dtype)


def t5_bucket(rel):
    nb = T5_BUCKETS // 2
    max_exact = nb // 2
    ret = jnp.where(rel > 0, nb, 0)
    n = jnp.abs(rel)
    nf = jnp.maximum(n, 1).astype(jnp.float32)
    large = max_exact + (jnp.log(nf / max_exact) / math.log(T5_MAX_DISTANCE / max_exact)
                         * (nb - max_exact)).astype(jnp.int32)
    large = jnp.minimum(large, nb - 1)
    return ret + jnp.where(n < max_exact, n, large)


def swa_mixer(q, kv, sink, t5_bias):
    B_, L, _ = q.shape
    NB = L // SWA_BLOCK
    G = SWA_HEADS // SWA_KV_HEADS
    q = q.reshape(B_, NB, SWA_BLOCK, SWA_KV_HEADS, G, SWA_HEAD_DIM)
    k, v = jnp.split(kv, 2, axis=-1)

    def band(t):
        t = jnp.pad(t, ((0, 0), (SWA_BLOCK, SWA_BLOCK), (0, 0)))
        t = t.reshape(B_, NB + 2, SWA_BLOCK, SWA_KV_HEADS, SWA_HEAD_DIM)
        return jnp.concatenate([t[:, :-2], t[:, 1:-1], t[:, 2:]], axis=2)

    kb, vb = band(k), band(v)
    qi = jnp.arange(SWA_BLOCK)[:, None]
    sj = jnp.arange(3 * SWA_BLOCK)[None, :]
    rel = sj - SWA_BLOCK - qi
    bias = t5_bias[t5_bucket(rel)].astype(jnp.float32)
    bias = bias.transpose(2, 0, 1).reshape(SWA_KV_HEADS, G, SWA_BLOCK, 3 * SWA_BLOCK)
    kpos = jnp.arange(NB)[:, None] * SWA_BLOCK + sj - SWA_BLOCK
    valid = (jnp.abs(rel) <= WINDOW)[None] & ((kpos >= 0) & (kpos < L))[:, None, :]
    logits = (jnp.einsum('bnqkgd,bnskd->bnkgqs', q, kb).astype(jnp.float32)
              * (SWA_HEAD_DIM ** -0.5) + bias)
    logits = jnp.where(valid[None, :, None, None], logits, -1e30)
    sink_col = jnp.broadcast_to(
        sink.astype(jnp.float32).reshape(SWA_KV_HEADS, G, 1, 1), logits.shape[:-1] + (1,))
    probs = jax.nn.softmax(jnp.concatenate([logits, sink_col], axis=-1), axis=-1)[..., :-1]
    out = jnp.einsum('bnkgqs,bnskd->bnqkgd', probs.astype(vb.dtype), vb)
    return out.reshape(B_, L, SWA_HEADS * SWA_HEAD_DIM)


def apply_rope(x, cos, sin):
    x1, x2 = jnp.split(x, 2, axis=-1)
    return jnp.concatenate([x1 * cos - x2 * sin, x2 * cos + x1 * sin], axis=-1)


def mla_mixer(c_q, c_kv, k_rope, q_gain, kv_gain, w_uq, w_ukv):
    B_, L, _ = c_q.shape
    H = MLA_HEADS
    q = (rms_norm(c_q, q_gain) @ w_uq).reshape(B_, L, H, MLA_NOPE + MLA_ROPE)
    kv = (rms_norm(c_kv, kv_gain) @ w_ukv).reshape(B_, L, H, MLA_NOPE + MLA_V)
    q_nope, q_pe = q[..., :MLA_NOPE], q[..., MLA_NOPE:]
    k_nope, v = kv[..., :MLA_NOPE], kv[..., MLA_NOPE:]
    pos = jnp.arange(L, dtype=jnp.float32)
    inv_freq = ROPE_THETA ** (-jnp.arange(0, MLA_ROPE, 2, dtype=jnp.float32) / MLA_ROPE)
    ang = pos[:, None] * inv_freq[None, :]
    cos, sin = jnp.cos(ang)[:, None, :], jnp.sin(ang)[:, None, :]
    q_pe = apply_rope(q_pe.astype(jnp.float32), cos, sin)
    k_pe = apply_rope(k_rope.astype(jnp.float32)[:, :, None, :], cos, sin)
    qf = jnp.concatenate([q_nope.astype(jnp.float32), q_pe], axis=-1).astype(c_q.dtype)
    kf = jnp.concatenate([k_nope.astype(jnp.float32),
                          jnp.broadcast_to(k_pe, (B_, L, H, MLA_ROPE))], axis=-1).astype(c_q.dtype)
    scale = (MLA_NOPE + MLA_ROPE) ** -0.5
    NB = L // ATTN_BLOCK
    qb = qf.reshape(B_, NB, ATTN_BLOCK, H, MLA_NOPE + MLA_ROPE).transpose(1, 0, 2, 3, 4)

    def attend(qblk):
        s = jnp.einsum('bqhd,bshd->bhqs', qblk, kf).astype(jnp.float32) * scale
        p = jax.nn.softmax(s, axis=-1)
        return jnp.einsum('bhqs,bshd->bqhd', p.astype(v.dtype), v)

    out = lax.map(attend, qb)
    return out.transpose(1, 0, 2, 3, 4).reshape(B_, L, H * MLA_V)


def _fwd_setup_inputs(seed: int = 0) -> dict:
    key = jax.random.key(seed)
    ks = jax.random.split(key, 28)
    f32 = jnp.float32

    def normal(k, shape, scale):
        return jax.random.normal(k, shape, f32) * scale

    G, P, Hg = S5_GROUPS, S5_STATE, S5_GROUP
    W = BRANCH_WIDTH
    gk = jax.random.split(ks[24], 4)
    gdn_dt = jnp.exp(jax.random.uniform(ks[14], (DEPTH, 2, GDN_HEADS), f32,
                                        math.log(1e-3), math.log(1e-1)))
    return {
        'x': jax.random.normal(ks[0], (BATCH, SEQ, D_MODEL), f32),
        'w_in': normal(ks[1], (DEPTH, D_MODEL, D_IN), D_MODEL ** -0.5),
        's5_lam_re': -0.5 + normal(ks[2], (DEPTH, 2, G, P), 0.01),
        's5_lam_im': math.pi * jnp.arange(P, dtype=f32) + normal(ks[3], (DEPTH, 2, G, P), 0.01),
        's5_log_step': jax.random.uniform(ks[4], (DEPTH, 2, G), f32,
                                          math.log(S5_DT_MIN), math.log(S5_DT_MAX)),
        's5_b_re': normal(ks[5], (DEPTH, 2, G, P, Hg), (2 * Hg) ** -0.5),
        's5_b_im': normal(ks[6], (DEPTH, 2, G, P, Hg), (2 * Hg) ** -0.5),
        's5_c_re': normal(ks[7], (DEPTH, 2, G, Hg, P), P ** -0.5),
        's5_c_im': normal(ks[8], (DEPTH, 2, G, Hg, P), P ** -0.5),
        's5_d': normal(ks[9], (DEPTH, W), 1.0),
        's5_w_glu': normal(ks[10], (DEPTH, W, W), W ** -0.5),
        's5_b_glu': normal(ks[11], (DEPTH, W), 0.01),
        'gdn_conv': normal(ks[12], (DEPTH, GDN_CONV, 3 * W), GDN_CONV ** -0.5),
        'gdn_a_log': jnp.log(jax.random.uniform(ks[13], (DEPTH, 2, GDN_HEADS), f32, 1.0, 16.0)),
        'gdn_dt_bias': gdn_dt + jnp.log(-jnp.expm1(-gdn_dt)),
        'gdn_o_gain': 1.0 + normal(ks[15], (DEPTH, GDN_HEAD_DIM), 0.02),
        'swa_sink': normal(ks[16], (DEPTH, SWA_HEADS), 0.5),
        't5_bias': normal(ks[17], (T5_BUCKETS, SWA_HEADS), 0.5),
        'mla_q_gain': 1.0 + normal(ks[18], (DEPTH, MLA_Q_RANK), 0.02),
        'mla_kv_gain': 1.0 + normal(ks[19], (DEPTH, MLA_KV_RANK), 0.02),
        'mla_w_uq': normal(ks[20], (DEPTH, MLA_Q_RANK, MLA_HEADS * (MLA_NOPE + MLA_ROPE)),
                           MLA_Q_RANK ** -0.5),
        'mla_w_ukv': normal(ks[21], (DEPTH, MLA_KV_RANK, MLA_HEADS * (MLA_NOPE + MLA_V)),
                            MLA_KV_RANK ** -0.5),
        'w_branch': normal(ks[22], (DEPTH, N_BRANCHES, W, D_MODEL), W ** -0.5),
        'w_out': normal(ks[23], (DEPTH, D_MODEL, D_MODEL), D_MODEL ** -0.5),
        'mix_pre_gain': 1.0 + normal(gk[0], (DEPTH, D_MODEL), 0.02),
        'mix_post_gain': 1.0 + normal(gk[1], (DEPTH, D_MODEL), 0.02),
        'mlp_pre_gain': 1.0 + normal(gk[2], (DEPTH, D_MODEL), 0.02),
        'mlp_post_gain': 1.0 + normal(gk[3], (DEPTH, D_MODEL), 0.02),
        'w_mlp_in': normal(ks[25], (DEPTH, D_MODEL, D_FF), D_MODEL ** -0.5),
        'w_mlp_out': normal(ks[26], (DEPTH, D_FF, D_MODEL), D_FF ** -0.5),
    }


def _fwd_reference(x, w_in, s5_lam_re, s5_lam_im, s5_log_step, s5_b_re, s5_b_im, s5_c_re,
              s5_c_im, s5_d, s5_w_glu, s5_b_glu, gdn_conv, gdn_a_log, gdn_dt_bias,
              gdn_o_gain, swa_sink, t5_bias, mla_q_gain, mla_kv_gain, mla_w_uq, mla_w_ukv,
              w_branch, w_out, mix_pre_gain, mix_post_gain, mlp_pre_gain, mlp_post_gain,
              w_mlp_in, w_mlp_out):
    B_, L, _ = x.shape
    split_points = np.cumsum(IN_SPLITS)[:-1].tolist()
    for l in range(DEPTH):
        h = rms_norm(x, mix_pre_gain[l])
        (s5_u, gdn_qkv, gdn_z, gdn_beta, gdn_decay, swa_q, swa_kv,
         mla_cq, mla_ckv, mla_kr, gate_logits) = jnp.split(h @ w_in[l], split_points, axis=-1)
        y_a = s5_mixer(s5_u, s5_lam_re[l], s5_lam_im[l], s5_log_step[l], s5_b_re[l],
                       s5_b_im[l], s5_c_re[l], s5_c_im[l], s5_d[l], s5_w_glu[l], s5_b_glu[l])
        y_b = gdn_mixer(gdn_qkv, gdn_z, gdn_beta, gdn_decay, gdn_conv[l], gdn_a_log[l],
                        gdn_dt_bias[l], gdn_o_gain[l])
        y_c = swa_mixer(swa_q, swa_kv, swa_sink[l], t5_bias)
        y_d = mla_mixer(mla_cq, mla_ckv, mla_kr, mla_q_gain[l], mla_kv_gain[l],
                        mla_w_uq[l], mla_w_ukv[l])
        ys = jnp.stack([y_a, y_b, y_c, y_d], axis=2)
        branch = jnp.einsum('blnw,nwd->blnd', ys, w_branch[l])
        gates = jax.nn.sigmoid(gate_logits.reshape(B_, L, N_BRANCHES, D_MODEL))
        merged = jnp.sum(gates * branch, axis=2)
        x = x + rms_norm(merged @ w_out[l], mix_post_gain[l])
        h = rms_norm(x, mlp_pre_gain[l])
        f = jnp.square(jax.nn.relu(h @ w_mlp_in[l])) @ w_mlp_out[l]
        x = x + rms_norm(f, mlp_post_gain[l])
    return x


import jax as _jax
import jax.numpy as _jnp

TWIN_FORMAT = 'train_step'
FWD_PARAMS = ['x', 'w_in', 's5_lam_re', 's5_lam_im', 's5_log_step', 's5_b_re', 's5_b_im', 's5_c_re', 's5_c_im', 's5_d', 's5_w_glu', 's5_b_glu', 'gdn_conv', 'gdn_a_log', 'gdn_dt_bias', 'gdn_o_gain', 'swa_sink', 't5_bias', 'mla_q_gain', 'mla_kv_gain', 'mla_w_uq', 'mla_w_ukv', 'w_branch', 'w_out', 'mix_pre_gain', 'mix_post_gain', 'mlp_pre_gain', 'mlp_post_gain', 'w_mlp_in', 'w_mlp_out']
TWIN_WEIGHTS = ['w_in', 's5_lam_re', 's5_lam_im', 's5_log_step', 's5_b_re', 's5_b_im', 's5_c_re', 's5_c_im', 's5_d', 's5_w_glu', 's5_b_glu', 'gdn_conv', 'gdn_a_log', 'gdn_dt_bias', 'gdn_o_gain', 'swa_sink', 't5_bias', 'mla_q_gain', 'mla_kv_gain', 'mla_w_uq', 'mla_w_ukv', 'w_branch', 'w_out', 'mix_pre_gain', 'mix_post_gain', 'mlp_pre_gain', 'mlp_post_gain', 'w_mlp_in', 'w_mlp_out']
TWIN_DIFF_INPUT = 'x'
TWIN_INPUTS = ['x', 'w_in', 's5_lam_re', 's5_lam_im', 's5_log_step', 's5_b_re', 's5_b_im', 's5_c_re', 's5_c_im', 's5_d', 's5_w_glu', 's5_b_glu', 'gdn_conv', 'gdn_a_log', 'gdn_dt_bias', 'gdn_o_gain', 'swa_sink', 't5_bias', 'mla_q_gain', 'mla_kv_gain', 'mla_w_uq', 'mla_w_ukv', 'w_branch', 'w_out', 'mix_pre_gain', 'mix_post_gain', 'mlp_pre_gain', 'mlp_post_gain', 'w_mlp_in', 'w_mlp_out', 'loss_target', 'm_w_in', 'm_s5_lam_re', 'm_s5_lam_im', 'm_s5_log_step', 'm_s5_b_re', 'm_s5_b_im', 'm_s5_c_re', 'm_s5_c_im', 'm_s5_d', 'm_s5_w_glu', 'm_s5_b_glu', 'm_gdn_conv', 'm_gdn_a_log', 'm_gdn_dt_bias', 'm_gdn_o_gain', 'm_swa_sink', 'm_t5_bias', 'm_mla_q_gain', 'm_mla_kv_gain', 'm_mla_w_uq', 'm_mla_w_ukv', 'm_w_branch', 'm_w_out', 'm_mix_pre_gain', 'm_mix_post_gain', 'm_mlp_pre_gain', 'm_mlp_post_gain', 'm_w_mlp_in', 'm_w_mlp_out', 'v_w_in', 'v_s5_lam_re', 'v_s5_lam_im', 'v_s5_log_step', 'v_s5_b_re', 'v_s5_b_im', 'v_s5_c_re', 'v_s5_c_im', 'v_s5_d', 'v_s5_w_glu', 'v_s5_b_glu', 'v_gdn_conv', 'v_gdn_a_log', 'v_gdn_dt_bias', 'v_gdn_o_gain', 'v_swa_sink', 'v_t5_bias', 'v_mla_q_gain', 'v_mla_kv_gain', 'v_mla_w_uq', 'v_mla_w_ukv', 'v_w_branch', 'v_w_out', 'v_mix_pre_gain', 'v_mix_post_gain', 'v_mlp_pre_gain', 'v_mlp_post_gain', 'v_w_mlp_in', 'v_w_mlp_out']
TWIN_OUTPUTS = ['loss', 'grad_x', 'grad_w_in', 'grad_s5_lam_re', 'grad_s5_lam_im', 'grad_s5_log_step', 'grad_s5_b_re', 'grad_s5_b_im', 'grad_s5_c_re', 'grad_s5_c_im', 'grad_s5_d', 'grad_s5_w_glu', 'grad_s5_b_glu', 'grad_gdn_conv', 'grad_gdn_a_log', 'grad_gdn_dt_bias', 'grad_gdn_o_gain', 'grad_swa_sink', 'grad_t5_bias', 'grad_mla_q_gain', 'grad_mla_kv_gain', 'grad_mla_w_uq', 'grad_mla_w_ukv', 'grad_w_branch', 'grad_w_out', 'grad_mix_pre_gain', 'grad_mix_post_gain', 'grad_mlp_pre_gain', 'grad_mlp_post_gain', 'grad_w_mlp_in', 'grad_w_mlp_out', 'delta_w_in', 'delta_s5_lam_re', 'delta_s5_lam_im', 'delta_s5_log_step', 'delta_s5_b_re', 'delta_s5_b_im', 'delta_s5_c_re', 'delta_s5_c_im', 'delta_s5_d', 'delta_s5_w_glu', 'delta_s5_b_glu', 'delta_gdn_conv', 'delta_gdn_a_log', 'delta_gdn_dt_bias', 'delta_gdn_o_gain', 'delta_swa_sink', 'delta_t5_bias', 'delta_mla_q_gain', 'delta_mla_kv_gain', 'delta_mla_w_uq', 'delta_mla_w_ukv', 'delta_w_branch', 'delta_w_out', 'delta_mix_pre_gain', 'delta_mix_post_gain', 'delta_mlp_pre_gain', 'delta_mlp_post_gain', 'delta_w_mlp_in', 'delta_w_mlp_out', 'new_m_w_in', 'new_m_s5_lam_re', 'new_m_s5_lam_im', 'new_m_s5_log_step', 'new_m_s5_b_re', 'new_m_s5_b_im', 'new_m_s5_c_re', 'new_m_s5_c_im', 'new_m_s5_d', 'new_m_s5_w_glu', 'new_m_s5_b_glu', 'new_m_gdn_conv', 'new_m_gdn_a_log', 'new_m_gdn_dt_bias', 'new_m_gdn_o_gain', 'new_m_swa_sink', 'new_m_t5_bias', 'new_m_mla_q_gain', 'new_m_mla_kv_gain', 'new_m_mla_w_uq', 'new_m_mla_w_ukv', 'new_m_w_branch', 'new_m_w_out', 'new_m_mix_pre_gain', 'new_m_mix_post_gain', 'new_m_mlp_pre_gain', 'new_m_mlp_post_gain', 'new_m_w_mlp_in', 'new_m_w_mlp_out', 'new_v_w_in', 'new_v_s5_lam_re', 'new_v_s5_lam_im', 'new_v_s5_log_step', 'new_v_s5_b_re', 'new_v_s5_b_im', 'new_v_s5_c_re', 'new_v_s5_c_im', 'new_v_s5_d', 'new_v_s5_w_glu', 'new_v_s5_b_glu', 'new_v_gdn_conv', 'new_v_gdn_a_log', 'new_v_gdn_dt_bias', 'new_v_gdn_o_gain', 'new_v_swa_sink', 'new_v_t5_bias', 'new_v_mla_q_gain', 'new_v_mla_kv_gain', 'new_v_mla_w_uq', 'new_v_mla_w_ukv', 'new_v_w_branch', 'new_v_w_out', 'new_v_mix_pre_gain', 'new_v_mix_post_gain', 'new_v_mlp_pre_gain', 'new_v_mlp_post_gain', 'new_v_w_mlp_in', 'new_v_w_mlp_out']
TWIN_LEAF_KINDS = {'loss': 'loss', 'grad_x': 'grad_x', 'grad_w_in': 'grad_w', 'grad_s5_lam_re': 'grad_w', 'grad_s5_lam_im': 'grad_w', 'grad_s5_log_step': 'grad_w', 'grad_s5_b_re': 'grad_w', 'grad_s5_b_im': 'grad_w', 'grad_s5_c_re': 'grad_w', 'grad_s5_c_im': 'grad_w', 'grad_s5_d': 'grad_w', 'grad_s5_w_glu': 'grad_w', 'grad_s5_b_glu': 'grad_w', 'grad_gdn_conv': 'grad_w', 'grad_gdn_a_log': 'grad_w', 'grad_gdn_dt_bias': 'grad_w', 'grad_gdn_o_gain': 'grad_w', 'grad_swa_sink': 'grad_w', 'grad_t5_bias': 'grad_w', 'grad_mla_q_gain': 'grad_w', 'grad_mla_kv_gain': 'grad_w', 'grad_mla_w_uq': 'grad_w', 'grad_mla_w_ukv': 'grad_w', 'grad_w_branch': 'grad_w', 'grad_w_out': 'grad_w', 'grad_mix_pre_gain': 'grad_w', 'grad_mix_post_gain': 'grad_w', 'grad_mlp_pre_gain': 'grad_w', 'grad_mlp_post_gain': 'grad_w', 'grad_w_mlp_in': 'grad_w', 'grad_w_mlp_out': 'grad_w', 'delta_w_in': 'delta_w', 'delta_s5_lam_re': 'delta_w', 'delta_s5_lam_im': 'delta_w', 'delta_s5_log_step': 'delta_w', 'delta_s5_b_re': 'delta_w', 'delta_s5_b_im': 'delta_w', 'delta_s5_c_re': 'delta_w', 'delta_s5_c_im': 'delta_w', 'delta_s5_d': 'delta_w', 'delta_s5_w_glu': 'delta_w', 'delta_s5_b_glu': 'delta_w', 'delta_gdn_conv': 'delta_w', 'delta_gdn_a_log': 'delta_w', 'delta_gdn_dt_bias': 'delta_w', 'delta_gdn_o_gain': 'delta_w', 'delta_swa_sink': 'delta_w', 'delta_t5_bias': 'delta_w', 'delta_mla_q_gain': 'delta_w', 'delta_mla_kv_gain': 'delta_w', 'delta_mla_w_uq': 'delta_w', 'delta_mla_w_ukv': 'delta_w', 'delta_w_branch': 'delta_w', 'delta_w_out': 'delta_w', 'delta_mix_pre_gain': 'delta_w', 'delta_mix_post_gain': 'delta_w', 'delta_mlp_pre_gain': 'delta_w', 'delta_mlp_post_gain': 'delta_w', 'delta_w_mlp_in': 'delta_w', 'delta_w_mlp_out': 'delta_w', 'new_m_w_in': 'new_m', 'new_m_s5_lam_re': 'new_m', 'new_m_s5_lam_im': 'new_m', 'new_m_s5_log_step': 'new_m', 'new_m_s5_b_re': 'new_m', 'new_m_s5_b_im': 'new_m', 'new_m_s5_c_re': 'new_m', 'new_m_s5_c_im': 'new_m', 'new_m_s5_d': 'new_m', 'new_m_s5_w_glu': 'new_m', 'new_m_s5_b_glu': 'new_m', 'new_m_gdn_conv': 'new_m', 'new_m_gdn_a_log': 'new_m', 'new_m_gdn_dt_bias': 'new_m', 'new_m_gdn_o_gain': 'new_m', 'new_m_swa_sink': 'new_m', 'new_m_t5_bias': 'new_m', 'new_m_mla_q_gain': 'new_m', 'new_m_mla_kv_gain': 'new_m', 'new_m_mla_w_uq': 'new_m', 'new_m_mla_w_ukv': 'new_m', 'new_m_w_branch': 'new_m', 'new_m_w_out': 'new_m', 'new_m_mix_pre_gain': 'new_m', 'new_m_mix_post_gain': 'new_m', 'new_m_mlp_pre_gain': 'new_m', 'new_m_mlp_post_gain': 'new_m', 'new_m_w_mlp_in': 'new_m', 'new_m_w_mlp_out': 'new_m', 'new_v_w_in': 'new_v', 'new_v_s5_lam_re': 'new_v', 'new_v_s5_lam_im': 'new_v', 'new_v_s5_log_step': 'new_v', 'new_v_s5_b_re': 'new_v', 'new_v_s5_b_im': 'new_v', 'new_v_s5_c_re': 'new_v', 'new_v_s5_c_im': 'new_v', 'new_v_s5_d': 'new_v', 'new_v_s5_w_glu': 'new_v', 'new_v_s5_b_glu': 'new_v', 'new_v_gdn_conv': 'new_v', 'new_v_gdn_a_log': 'new_v', 'new_v_gdn_dt_bias': 'new_v', 'new_v_gdn_o_gain': 'new_v', 'new_v_swa_sink': 'new_v', 'new_v_t5_bias': 'new_v', 'new_v_mla_q_gain': 'new_v', 'new_v_mla_kv_gain': 'new_v', 'new_v_mla_w_uq': 'new_v', 'new_v_mla_w_ukv': 'new_v', 'new_v_w_branch': 'new_v', 'new_v_w_out': 'new_v', 'new_v_mix_pre_gain': 'new_v', 'new_v_mix_post_gain': 'new_v', 'new_v_mlp_pre_gain': 'new_v', 'new_v_mlp_post_gain': 'new_v', 'new_v_w_mlp_in': 'new_v', 'new_v_w_mlp_out': 'new_v'}


def _forward(args):
    return _fwd_reference(*[args[k] for k in FWD_PARAMS])


def _output_shape():
    def fwd():
        inp = _fwd_setup_inputs(0)
        return _fwd_reference(*[inp[k] for k in FWD_PARAMS])
    out = _jax.eval_shape(fwd)
    return out.shape, out.dtype

N_MICROBATCH = 1
ADAM_LR = 0.001
ADAM_B1 = 0.9
ADAM_B2 = 0.999
ADAM_EPS = 1e-08
ADAM_WD = 0.01
ADAM_STEP = 10
PER_EXAMPLE_BATCH_AXIS = {'x': 0, 'loss_target': 0}
SHARED_INPUTS = []
_WEIGHT_DTYPES = {'w_in': _jnp.float32, 's5_lam_re': _jnp.float32, 's5_lam_im': _jnp.float32, 's5_log_step': _jnp.float32, 's5_b_re': _jnp.float32, 's5_b_im': _jnp.float32, 's5_c_re': _jnp.float32, 's5_c_im': _jnp.float32, 's5_d': _jnp.float32, 's5_w_glu': _jnp.float32, 's5_b_glu': _jnp.float32, 'gdn_conv': _jnp.float32, 'gdn_a_log': _jnp.float32, 'gdn_dt_bias': _jnp.float32, 'gdn_o_gain': _jnp.float32, 'swa_sink': _jnp.float32, 't5_bias': _jnp.float32, 'mla_q_gain': _jnp.float32, 'mla_kv_gain': _jnp.float32, 'mla_w_uq': _jnp.float32, 'mla_w_ukv': _jnp.float32, 'w_branch': _jnp.float32, 'w_out': _jnp.float32, 'mix_pre_gain': _jnp.float32, 'mix_post_gain': _jnp.float32, 'mlp_pre_gain': _jnp.float32, 'mlp_post_gain': _jnp.float32, 'w_mlp_in': _jnp.float32, 'w_mlp_out': _jnp.float32}
MOMENT_SCALE = {'w_in': 1.262378e+01, 's5_lam_re': 1.015139e+00, 's5_lam_im': 1.226481e+00, 's5_log_step': 4.110926e+01, 's5_b_re': 7.710626e-01, 's5_b_im': 7.794053e-01, 's5_c_re': 1.260737e+00, 's5_c_im': 1.075305e+00, 's5_d': 3.468461e+01, 's5_w_glu': 6.077223e+00, 's5_b_glu': 1.589800e+01, 'gdn_conv': 1.489704e+01, 'gdn_a_log': 1.194913e+01, 'gdn_dt_bias': 1.175958e+01, 'gdn_o_gain': 5.839262e+01, 'swa_sink': 8.762253e-01, 't5_bias': 9.258552e-01, 'mla_q_gain': 1.325535e+00, 'mla_kv_gain': 3.891648e+01, 'mla_w_uq': 9.413941e-01, 'mla_w_ukv': 2.809642e+01, 'w_branch': 1.834523e+01, 'w_out': 3.618143e+01, 'mix_pre_gain': 3.086895e+01, 'mix_post_gain': 5.167063e+01, 'mlp_pre_gain': 1.872784e+01, 'mlp_post_gain': 4.812156e+01, 'w_mlp_in': 9.225600e+00, 'w_mlp_out': 3.419514e+01}


def _to_microbatches(a, axis):
    t = _jnp.moveaxis(a, axis, 0)
    t = t.reshape((N_MICROBATCH, t.shape[0] // N_MICROBATCH) + t.shape[1:])
    return _jnp.moveaxis(t, 1, axis + 1)


def setup_inputs(seed: int = 0) -> dict:
    inp = _fwd_setup_inputs(seed)
    key = _jax.random.fold_in(_jax.random.key(seed), 7919)
    shape, _ = _output_shape()
    out = dict(inp)
    out["loss_target"] = _jax.random.normal(_jax.random.fold_in(key, 0), shape, _jnp.float32)
    for i, name in enumerate(TWIN_WEIGHTS):
        w = inp[name].astype(_jnp.float32)
        if MOMENT_SCALE is None:
            s = _jnp.sqrt(_jnp.mean(_jnp.square(w)) + 1e-30)
        else:
            s = MOMENT_SCALE[name]
        km, kv = _jax.random.split(_jax.random.fold_in(key, i + 1))
        out[name] = w
        out["m_" + name] = s * _jax.random.normal(km, w.shape, _jnp.float32)
        out["v_" + name] = (s * s) * _jax.random.uniform(kv, w.shape, _jnp.float32, 0.5, 1.5)
    if N_MICROBATCH > 1:
        for name, axis in PER_EXAMPLE_BATCH_AXIS.items():
            out[name] = _to_microbatches(out[name], axis)
    return {'x': out['x'], 'w_in': out['w_in'], 's5_lam_re': out['s5_lam_re'], 's5_lam_im': out['s5_lam_im'], 's5_log_step': out['s5_log_step'], 's5_b_re': out['s5_b_re'], 's5_b_im': out['s5_b_im'], 's5_c_re': out['s5_c_re'], 's5_c_im': out['s5_c_im'], 's5_d': out['s5_d'], 's5_w_glu': out['s5_w_glu'], 's5_b_glu': out['s5_b_glu'], 'gdn_conv': out['gdn_conv'], 'gdn_a_log': out['gdn_a_log'], 'gdn_dt_bias': out['gdn_dt_bias'], 'gdn_o_gain': out['gdn_o_gain'], 'swa_sink': out['swa_sink'], 't5_bias': out['t5_bias'], 'mla_q_gain': out['mla_q_gain'], 'mla_kv_gain': out['mla_kv_gain'], 'mla_w_uq': out['mla_w_uq'], 'mla_w_ukv': out['mla_w_ukv'], 'w_branch': out['w_branch'], 'w_out': out['w_out'], 'mix_pre_gain': out['mix_pre_gain'], 'mix_post_gain': out['mix_post_gain'], 'mlp_pre_gain': out['mlp_pre_gain'], 'mlp_post_gain': out['mlp_post_gain'], 'w_mlp_in': out['w_mlp_in'], 'w_mlp_out': out['w_mlp_out'], 'loss_target': out['loss_target'], 'm_w_in': out['m_w_in'], 'm_s5_lam_re': out['m_s5_lam_re'], 'm_s5_lam_im': out['m_s5_lam_im'], 'm_s5_log_step': out['m_s5_log_step'], 'm_s5_b_re': out['m_s5_b_re'], 'm_s5_b_im': out['m_s5_b_im'], 'm_s5_c_re': out['m_s5_c_re'], 'm_s5_c_im': out['m_s5_c_im'], 'm_s5_d': out['m_s5_d'], 'm_s5_w_glu': out['m_s5_w_glu'], 'm_s5_b_glu': out['m_s5_b_glu'], 'm_gdn_conv': out['m_gdn_conv'], 'm_gdn_a_log': out['m_gdn_a_log'], 'm_gdn_dt_bias': out['m_gdn_dt_bias'], 'm_gdn_o_gain': out['m_gdn_o_gain'], 'm_swa_sink': out['m_swa_sink'], 'm_t5_bias': out['m_t5_bias'], 'm_mla_q_gain': out['m_mla_q_gain'], 'm_mla_kv_gain': out['m_mla_kv_gain'], 'm_mla_w_uq': out['m_mla_w_uq'], 'm_mla_w_ukv': out['m_mla_w_ukv'], 'm_w_branch': out['m_w_branch'], 'm_w_out': out['m_w_out'], 'm_mix_pre_gain': out['m_mix_pre_gain'], 'm_mix_post_gain': out['m_mix_post_gain'], 'm_mlp_pre_gain': out['m_mlp_pre_gain'], 'm_mlp_post_gain': out['m_mlp_post_gain'], 'm_w_mlp_in': out['m_w_mlp_in'], 'm_w_mlp_out': out['m_w_mlp_out'], 'v_w_in': out['v_w_in'], 'v_s5_lam_re': out['v_s5_lam_re'], 'v_s5_lam_im': out['v_s5_lam_im'], 'v_s5_log_step': out['v_s5_log_step'], 'v_s5_b_re': out['v_s5_b_re'], 'v_s5_b_im': out['v_s5_b_im'], 'v_s5_c_re': out['v_s5_c_re'], 'v_s5_c_im': out['v_s5_c_im'], 'v_s5_d': out['v_s5_d'], 'v_s5_w_glu': out['v_s5_w_glu'], 'v_s5_b_glu': out['v_s5_b_glu'], 'v_gdn_conv': out['v_gdn_conv'], 'v_gdn_a_log': out['v_gdn_a_log'], 'v_gdn_dt_bias': out['v_gdn_dt_bias'], 'v_gdn_o_gain': out['v_gdn_o_gain'], 'v_swa_sink': out['v_swa_sink'], 'v_t5_bias': out['v_t5_bias'], 'v_mla_q_gain': out['v_mla_q_gain'], 'v_mla_kv_gain': out['v_mla_kv_gain'], 'v_mla_w_uq': out['v_mla_w_uq'], 'v_mla_w_ukv': out['v_mla_w_ukv'], 'v_w_branch': out['v_w_branch'], 'v_w_out': out['v_w_out'], 'v_mix_pre_gain': out['v_mix_pre_gain'], 'v_mix_post_gain': out['v_mix_post_gain'], 'v_mlp_pre_gain': out['v_mlp_pre_gain'], 'v_mlp_post_gain': out['v_mlp_post_gain'], 'v_w_mlp_in': out['v_w_mlp_in'], 'v_w_mlp_out': out['v_w_mlp_out']}


def _loss(weights, diff, rest, loss_target):
    with _jax.named_scope("forward"):
        args = {**rest, TWIN_DIFF_INPUT: diff, **{k: w.astype(_WEIGHT_DTYPES[k]) for k, w in weights.items()}}
        y = _forward(args)
    with _jax.named_scope("loss_head"):
        err = _jnp.square(y.astype(_jnp.float32) - loss_target)
        return 0.5 * _jnp.sum(_jnp.mean(err, axis=-1)) if err.ndim else 0.5 * err


def _adamw(w, g, m, v):
    m = ADAM_B1 * m + (1.0 - ADAM_B1) * g
    v = ADAM_B2 * v + (1.0 - ADAM_B2) * _jnp.square(g)
    m_hat = m / (1.0 - ADAM_B1 ** ADAM_STEP)
    v_hat = v / (1.0 - ADAM_B2 ** ADAM_STEP)
    delta = -ADAM_LR * (m_hat / (_jnp.sqrt(v_hat) + ADAM_EPS) + ADAM_WD * w)
    return delta, m, v


def reference(x, w_in, s5_lam_re, s5_lam_im, s5_log_step, s5_b_re, s5_b_im, s5_c_re, s5_c_im, s5_d, s5_w_glu, s5_b_glu, gdn_conv, gdn_a_log, gdn_dt_bias, gdn_o_gain, swa_sink, t5_bias, mla_q_gain, mla_kv_gain, mla_w_uq, mla_w_ukv, w_branch, w_out, mix_pre_gain, mix_post_gain, mlp_pre_gain, mlp_post_gain, w_mlp_in, w_mlp_out, loss_target, m_w_in, m_s5_lam_re, m_s5_lam_im, m_s5_log_step, m_s5_b_re, m_s5_b_im, m_s5_c_re, m_s5_c_im, m_s5_d, m_s5_w_glu, m_s5_b_glu, m_gdn_conv, m_gdn_a_log, m_gdn_dt_bias, m_gdn_o_gain, m_swa_sink, m_t5_bias, m_mla_q_gain, m_mla_kv_gain, m_mla_w_uq, m_mla_w_ukv, m_w_branch, m_w_out, m_mix_pre_gain, m_mix_post_gain, m_mlp_pre_gain, m_mlp_post_gain, m_w_mlp_in, m_w_mlp_out, v_w_in, v_s5_lam_re, v_s5_lam_im, v_s5_log_step, v_s5_b_re, v_s5_b_im, v_s5_c_re, v_s5_c_im, v_s5_d, v_s5_w_glu, v_s5_b_glu, v_gdn_conv, v_gdn_a_log, v_gdn_dt_bias, v_gdn_o_gain, v_swa_sink, v_t5_bias, v_mla_q_gain, v_mla_kv_gain, v_mla_w_uq, v_mla_w_ukv, v_w_branch, v_w_out, v_mix_pre_gain, v_mix_post_gain, v_mlp_pre_gain, v_mlp_post_gain, v_w_mlp_in, v_w_mlp_out):
    given = dict(x=x, w_in=w_in, s5_lam_re=s5_lam_re, s5_lam_im=s5_lam_im, s5_log_step=s5_log_step, s5_b_re=s5_b_re, s5_b_im=s5_b_im, s5_c_re=s5_c_re, s5_c_im=s5_c_im, s5_d=s5_d, s5_w_glu=s5_w_glu, s5_b_glu=s5_b_glu, gdn_conv=gdn_conv, gdn_a_log=gdn_a_log, gdn_dt_bias=gdn_dt_bias, gdn_o_gain=gdn_o_gain, swa_sink=swa_sink, t5_bias=t5_bias, mla_q_gain=mla_q_gain, mla_kv_gain=mla_kv_gain, mla_w_uq=mla_w_uq, mla_w_ukv=mla_w_ukv, w_branch=w_branch, w_out=w_out, mix_pre_gain=mix_pre_gain, mix_post_gain=mix_post_gain, mlp_pre_gain=mlp_pre_gain, mlp_post_gain=mlp_post_gain, w_mlp_in=w_mlp_in, w_mlp_out=w_mlp_out, loss_target=loss_target, m_w_in=m_w_in, m_s5_lam_re=m_s5_lam_re, m_s5_lam_im=m_s5_lam_im, m_s5_log_step=m_s5_log_step, m_s5_b_re=m_s5_b_re, m_s5_b_im=m_s5_b_im, m_s5_c_re=m_s5_c_re, m_s5_c_im=m_s5_c_im, m_s5_d=m_s5_d, m_s5_w_glu=m_s5_w_glu, m_s5_b_glu=m_s5_b_glu, m_gdn_conv=m_gdn_conv, m_gdn_a_log=m_gdn_a_log, m_gdn_dt_bias=m_gdn_dt_bias, m_gdn_o_gain=m_gdn_o_gain, m_swa_sink=m_swa_sink, m_t5_bias=m_t5_bias, m_mla_q_gain=m_mla_q_gain, m_mla_kv_gain=m_mla_kv_gain, m_mla_w_uq=m_mla_w_uq, m_mla_w_ukv=m_mla_w_ukv, m_w_branch=m_w_branch, m_w_out=m_w_out, m_mix_pre_gain=m_mix_pre_gain, m_mix_post_gain=m_mix_post_gain, m_mlp_pre_gain=m_mlp_pre_gain, m_mlp_post_gain=m_mlp_post_gain, m_w_mlp_in=m_w_mlp_in, m_w_mlp_out=m_w_mlp_out, v_w_in=v_w_in, v_s5_lam_re=v_s5_lam_re, v_s5_lam_im=v_s5_lam_im, v_s5_log_step=v_s5_log_step, v_s5_b_re=v_s5_b_re, v_s5_b_im=v_s5_b_im, v_s5_c_re=v_s5_c_re, v_s5_c_im=v_s5_c_im, v_s5_d=v_s5_d, v_s5_w_glu=v_s5_w_glu, v_s5_b_glu=v_s5_b_glu, v_gdn_conv=v_gdn_conv, v_gdn_a_log=v_gdn_a_log, v_gdn_dt_bias=v_gdn_dt_bias, v_gdn_o_gain=v_gdn_o_gain, v_swa_sink=v_swa_sink, v_t5_bias=v_t5_bias, v_mla_q_gain=v_mla_q_gain, v_mla_kv_gain=v_mla_kv_gain, v_mla_w_uq=v_mla_w_uq, v_mla_w_ukv=v_mla_w_ukv, v_w_branch=v_w_branch, v_w_out=v_w_out, v_mix_pre_gain=v_mix_pre_gain, v_mix_post_gain=v_mix_post_gain, v_mlp_pre_gain=v_mlp_pre_gain, v_mlp_post_gain=v_mlp_post_gain, v_w_mlp_in=v_w_mlp_in, v_w_mlp_out=v_w_mlp_out)
    weights = {n: given[n] for n in TWIN_WEIGHTS}
    shared = {n: given[n] for n in SHARED_INPUTS}
    per_example = {n: given[n] for n in ['x']}
    grad_fn = _jax.value_and_grad(_loss, argnums=(0, 1))

    def one_microbatch(ex, loss_target):
        ex = dict(ex)
        diff = ex.pop(TWIN_DIFF_INPUT)
        return grad_fn(weights, diff, {**shared, **ex}, loss_target)

    if N_MICROBATCH == 1:
        loss, (grad_w, grad_x) = one_microbatch(per_example, given["loss_target"])
    else:
        def body(carry, xs):
            loss_sum, grad_sum = carry
            l_k, (gw_k, gx_k) = one_microbatch(xs[0], xs[1])
            with _jax.named_scope("update"):
                return (loss_sum + l_k, _jax.tree.map(_jnp.add, grad_sum, gw_k)), gx_k

        init = (_jnp.zeros((), _jnp.float32), _jax.tree.map(_jnp.zeros_like, weights))
        (loss, grad_w), grad_x = _jax.lax.scan(body, init, (per_example, given["loss_target"]))
    with _jax.named_scope("update"):
        delta_w, new_m, new_v = {}, {}, {}
        for n in TWIN_WEIGHTS:
            delta_w[n], new_m[n], new_v[n] = _adamw(weights[n], grad_w[n], given["m_" + n], given["v_" + n])
    return (loss, grad_x, *[grad_w[n] for n in TWIN_WEIGHTS], *[delta_w[n] for n in TWIN_WEIGHTS],
            *[new_m[n] for n in TWIN_WEIGHTS], *[new_v[n] for n in TWIN_WEIGHTS])
```

```python
import functools
import math

import jax
import jax.numpy as jnp
import numpy as np
from jax import lax
from jax.experimental import pallas as pl
from jax.experimental.pallas import tpu as pltpu

F32 = jnp.float32
BF16 = jnp.bfloat16

VMEM_LIMIT_BYTES = 48 * 1024 * 1024
LANES = 128
SUBLANES = 8

N_DEV = 8
MESH_AXES = ("x", "y", "c")

DEPTH = 4
N_BRANCHES = 4
BRANCH_WIDTH = 512
NORM_EPS = 1e-6
S5_GROUP = 16
S5_GROUPS = 32
S5_STATE = 64
S5_WIDTH = S5_GROUPS * S5_STATE
GDN_HEAD_DIM = 128
GDN_HEADS = 4
GDN_CONV = 4
GDN_CHUNK = 64
SWA_HEAD_DIM = 64
SWA_HEADS = 8
SWA_KV_HEADS = 2
SWA_GROUP = SWA_HEADS // SWA_KV_HEADS
WINDOW = 128
SWA_BLOCK = 128
T5_BUCKETS = 32
T5_MAX_DISTANCE = 128
MLA_HEADS = 4
MLA_Q_RANK = 384
MLA_KV_RANK = 512
MLA_NOPE = 128
MLA_ROPE = 64
MLA_V = 128
ROPE_THETA = 10000.0

ADAM_LR = 0.001
ADAM_B1 = 0.9
ADAM_B2 = 0.999
ADAM_EPS = 1e-08
ADAM_WD = 0.01
ADAM_STEP = 10

NEG_BIG = -1e30


def _cparams(*sem):
    return pltpu.CompilerParams(dimension_semantics=sem if sem else None, vmem_limit_bytes=VMEM_LIMIT_BYTES)


def _pick(n, pref, unit):
    if n <= pref:
        return n
    t = (pref // unit) * unit
    while t >= unit:
        if n % t == 0:
            return t
        t -= unit
    return n


def _bdot(a, b, dims):
    return lax.dot_general(a.astype(BF16), b.astype(BF16), (dims, ((), ())), preferred_element_type=F32)


def _dot_nn(a, b):
    return _bdot(a, b, ((1,), (0,)))


def _dot_nt(a, b):
    return _bdot(a, b, ((1,), (1,)))


def _dot_tn(a, b):
    return _bdot(a, b, ((0,), (0,)))


def _mm_body(a_ref, b_ref, o_ref, acc_ref, *, ta, tb, nk):
    k = pl.program_id(2)

    @pl.when(k == 0)
    def _():
        acc_ref[...] = jnp.zeros_like(acc_ref)

    dims = ((0,) if ta else (1,), (1,) if tb else (0,))
    acc_ref[...] += _bdot(a_ref[...], b_ref[...], dims)

    @pl.when(k == nk - 1)
    def _():
        o_ref[...] = acc_ref[...].astype(o_ref.dtype)


def _mm_call(a, b, *, ta=False, tb=False, name):
    m, k = (a.shape[1], a.shape[0]) if ta else a.shape
    n = b.shape[0] if tb else b.shape[1]
    assert (b.shape[1] if tb else b.shape[0]) == k, (a.shape, b.shape, ta, tb)
    tm, tn, tk = _pick(m, 1024, LANES), _pick(n, 1024, LANES), _pick(k, 512, LANES)
    nk = k // tk
    a_spec = pl.BlockSpec((tk, tm), lambda i, j, kk: (kk, i)) if ta else pl.BlockSpec((tm, tk), lambda i, j, kk: (i, kk))
    b_spec = pl.BlockSpec((tn, tk), lambda i, j, kk: (j, kk)) if tb else pl.BlockSpec((tk, tn), lambda i, j, kk: (kk, j))
    return pl.pallas_call(
        functools.partial(_mm_body, ta=ta, tb=tb, nk=nk),
        name=name,
        grid=(m // tm, n // tn, nk),
        in_specs=[a_spec, b_spec],
        out_specs=pl.BlockSpec((tm, tn), lambda i, j, kk: (i, j)),
        out_shape=jax.ShapeDtypeStruct((m, n), F32),
        scratch_shapes=[pltpu.VMEM((tm, tn), F32)],
        compiler_params=_cparams("parallel", "parallel", "arbitrary"),
    )(a, b)


@jax.custom_vjp
def mm(a, b):
    return _mm_call(a, b, name="mm_fwd")


def _mm_fwd(a, b):
    return _mm_call(a, b, name="mm_fwd"), (a, b)


def _mm_bwd(res, g):
    a, b = res
    return _mm_call(g, b, tb=True, name="mm_da"), _mm_call(a, g, ta=True, name="mm_db")


mm.defvjp(_mm_fwd, _mm_bwd)


def _rms_fwd_body(x_ref, g_ref, y_ref):
    x = x_ref[...]
    r = lax.rsqrt(jnp.mean(x * x, axis=-1, keepdims=True) + NORM_EPS)
    y_ref[...] = x * r * g_ref[...]


def _rms_bwd_body(x_ref, g_ref, dy_ref, dx_ref, dg_ref, *, tr):
    i = pl.program_id(0)

    @pl.when(i == 0)
    def _():
        dg_ref[...] = jnp.zeros_like(dg_ref)

    x = x_ref[...]
    dy = dy_ref[...]
    r = lax.rsqrt(jnp.mean(x * x, axis=-1, keepdims=True) + NORM_EPS)
    xhat = x * r
    gy = dy * g_ref[...]
    dx_ref[...] = r * (gy - xhat * jnp.mean(gy * xhat, axis=-1, keepdims=True))
    dg_ref[...] += jnp.sum((dy * xhat).reshape(tr // SUBLANES, SUBLANES, x.shape[-1]), axis=0)


def _rms_rows(rows, cols):
    return _pick(rows, max(SUBLANES, (2 * 1024 * 1024) // (4 * cols)), SUBLANES)


def _rms_fwd_call(x, gain):
    rows, cols = x.shape
    tr = _rms_rows(rows, cols)
    return pl.pallas_call(
        functools.partial(_rms_fwd_body),
        name="rms_fwd",
        grid=(rows // tr,),
        in_specs=[pl.BlockSpec((tr, cols), lambda i: (i, 0)), pl.BlockSpec((1, cols), lambda i: (0, 0))],
        out_specs=pl.BlockSpec((tr, cols), lambda i: (i, 0)),
        out_shape=jax.ShapeDtypeStruct((rows, cols), F32),
        compiler_params=_cparams("parallel"),
    )(x, gain.reshape(1, cols))


def _rms_bwd_call(x, gain, dy):
    rows, cols = x.shape
    tr = _rms_rows(rows, cols)
    dx, dg = pl.pallas_call(
        functools.partial(_rms_bwd_body, tr=tr),
        name="rms_bwd",
        grid=(rows // tr,),
        in_specs=[
            pl.BlockSpec((tr, cols), lambda i: (i, 0)),
            pl.BlockSpec((1, cols), lambda i: (0, 0)),
            pl.BlockSpec((tr, cols), lambda i: (i, 0)),
        ],
        out_specs=[pl.BlockSpec((tr, cols), lambda i: (i, 0)), pl.BlockSpec((SUBLANES, cols), lambda i: (0, 0))],
        out_shape=[jax.ShapeDtypeStruct((rows, cols), F32), jax.ShapeDtypeStruct((SUBLANES, cols), F32)],
        compiler_params=_cparams("arbitrary"),
    )(x, gain.reshape(1, cols), dy)
    return dx, jnp.sum(dg, axis=0)


@jax.custom_vjp
def rmsnorm(x, gain):
    return _rms_fwd_call(x, gain)


def _rmsnorm_fwd(x, gain):
    return _rms_fwd_call(x, gain), (x, gain)


def _rmsnorm_bwd(res, dy):
    x, gain = res
    return _rms_bwd_call(x, gain, dy)


rmsnorm.defvjp(_rmsnorm_fwd, _rmsnorm_bwd)


def _loss_body(y_ref, t_ref, rows_ref, dy_ref):
    d = y_ref[...] - t_ref[...]
    rows_ref[...] = 0.5 * jnp.mean(d * d, axis=-1, keepdims=True)
    dy_ref[...] = d * (1.0 / d.shape[-1])


def loss_head(y, target):
    rows, cols = y.shape
    tr = _rms_rows(rows, cols)
    return pl.pallas_call(
        functools.partial(_loss_body),
        name="loss_head",
        grid=(rows // tr,),
        in_specs=[pl.BlockSpec((tr, cols), lambda i: (i, 0))] * 2,
        out_specs=[pl.BlockSpec((tr, 1), lambda i: (i, 0)), pl.BlockSpec((tr, cols), lambda i: (i, 0))],
        out_shape=[jax.ShapeDtypeStruct((rows, 1), F32), jax.ShapeDtypeStruct((rows, cols), F32)],
        compiler_params=_cparams("parallel"),
    )(y, target)


def _adamw_body(w_ref, g_ref, m_ref, v_ref, d_ref, nm_ref, nv_ref):
    g = g_ref[...]
    m = ADAM_B1 * m_ref[...] + (1.0 - ADAM_B1) * g
    v = ADAM_B2 * v_ref[...] + (1.0 - ADAM_B2) * (g * g)
    m_hat = m / (1.0 - ADAM_B1**ADAM_STEP)
    v_hat = v / (1.0 - ADAM_B2**ADAM_STEP)
    d_ref[...] = -ADAM_LR * (m_hat / (jnp.sqrt(v_hat) + ADAM_EPS) + ADAM_WD * w_ref[...])
    nm_ref[...] = m
    nv_ref[...] = v


def adamw_flat(w, g, m, v):
    rows, cols = w.shape
    tr = _pick(rows, 512, SUBLANES)
    spec = pl.BlockSpec((tr, cols), lambda i: (i, 0))
    return pl.pallas_call(
        functools.partial(_adamw_body),
        name="adamw",
        grid=(rows // tr,),
        in_specs=[spec] * 4,
        out_specs=[spec] * 3,
        out_shape=[jax.ShapeDtypeStruct((rows, cols), F32)] * 3,
        compiler_params=_cparams("parallel"),
    )(w, g, m, v)


def _sum_body(x_ref, o_ref, *, n):
    acc = x_ref[0].astype(F32)
    for k in range(1, n):
        acc = acc + x_ref[k].astype(F32)
    o_ref[...] = acc


def sum_leading(x):
    n, rows, cols = x.shape
    tr = _pick(rows, 256, 2 * SUBLANES)
    return pl.pallas_call(
        functools.partial(_sum_body, n=n),
        name="sum_leading",
        grid=(rows // tr,),
        in_specs=[pl.BlockSpec((n, tr, cols), lambda i: (0, i, 0))],
        out_specs=pl.BlockSpec((tr, cols), lambda i: (i, 0)),
        out_shape=jax.ShapeDtypeStruct((rows, cols), F32),
        compiler_params=_cparams("parallel"),
    )(x)


def _mla_fwd_body(q_ref, k_ref, v_ref, o_ref, lse_ref, m_sc, l_sc, acc_sc, *, scale, nk):
    ki = pl.program_id(2)

    @pl.when(ki == 0)
    def _():
        m_sc[...] = jnp.full_like(m_sc, NEG_BIG)
        l_sc[...] = jnp.zeros_like(l_sc)
        acc_sc[...] = jnp.zeros_like(acc_sc)

    s = _dot_nt(q_ref[...], k_ref[...]) * scale
    m_prev = m_sc[...]
    m_new = jnp.maximum(m_prev, jnp.max(s, axis=1, keepdims=True))
    alpha = jnp.exp(m_prev - m_new)
    p = jnp.exp(s - m_new)
    l_sc[...] = alpha * l_sc[...] + jnp.sum(p, axis=1, keepdims=True)
    acc_sc[...] = alpha * acc_sc[...] + _dot_nn(p, v_ref[...])
    m_sc[...] = m_new

    @pl.when(ki == nk - 1)
    def _():
        o_ref[...] = acc_sc[...] / l_sc[...]
        lse_ref[...] = m_sc[...] + jnp.log(l_sc[...])


def _mla_fwd_call(q, k, v):
    h, seq, dq = q.shape
    dv = v.shape[-1]
    tq = tk = _pick(seq, 512, LANES)
    nk = seq // tk
    scale = dq**-0.5
    return pl.pallas_call(
        functools.partial(_mla_fwd_body, scale=scale, nk=nk),
        name="mla_fwd",
        grid=(h, seq // tq, nk),
        in_specs=[
            pl.BlockSpec((None, tq, dq), lambda hh, i, j: (hh, i, 0)),
            pl.BlockSpec((None, tk, dq), lambda hh, i, j: (hh, j, 0)),
            pl.BlockSpec((None, tk, dv), lambda hh, i, j: (hh, j, 0)),
        ],
        out_specs=[
            pl.BlockSpec((None, tq, dv), lambda hh, i, j: (hh, i, 0)),
            pl.BlockSpec((None, tq, 1), lambda hh, i, j: (hh, i, 0)),
        ],
        out_shape=[jax.ShapeDtypeStruct((h, seq, dv), F32), jax.ShapeDtypeStruct((h, seq, 1), F32)],
        scratch_shapes=[pltpu.VMEM((tq, 1), F32), pltpu.VMEM((tq, 1), F32), pltpu.VMEM((tq, dv), F32)],
        compiler_params=_cparams("parallel", "parallel", "arbitrary"),
    )(q, k, v)


def _mla_bwd_body(q_ref, k_ref, v_ref, do_ref, lse_ref, dl_ref, dq_ref, dk_ref, dv_ref, dk_sc, dv_sc, *, scale, nq, tq):
    ki = pl.program_id(1)
    qi = pl.program_id(2)

    @pl.when(jnp.logical_and(ki == 0, qi == 0))
    def _():
        dq_ref[...] = jnp.zeros_like(dq_ref)

    @pl.when(qi == 0)
    def _():
        dk_sc[...] = jnp.zeros_like(dk_sc)
        dv_sc[...] = jnp.zeros_like(dv_sc)

    q = q_ref[...]
    k = k_ref[...]
    do = do_ref[...]
    p = jnp.exp(_dot_nt(q, k) * scale - lse_ref[...])
    dv_sc[...] += _dot_tn(p, do)
    ds = p * (_dot_nt(do, v_ref[...]) - dl_ref[...]) * scale
    dk_sc[...] += _dot_tn(ds, q)
    rows = pl.ds(pl.multiple_of(qi * tq, tq), tq)
    dq_ref[rows, :] += _dot_nn(ds, k)

    @pl.when(qi == nq - 1)
    def _():
        dk_ref[...] = dk_sc[...]
        dv_ref[...] = dv_sc[...]


def _mla_bwd_call(q, k, v, do, lse, delta):
    h, seq, dq = q.shape
    dv = v.shape[-1]
    tq = tk = _pick(seq, 512, LANES)
    nq = seq // tq
    scale = dq**-0.5
    return pl.pallas_call(
        functools.partial(_mla_bwd_body, scale=scale, nq=nq, tq=tq),
        name="mla_bwd",
        grid=(h, seq // tk, nq),
        in_specs=[
            pl.BlockSpec((None, tq, dq), lambda hh, j, i: (hh, i, 0)),
            pl.BlockSpec((None, tk, dq), lambda hh, j, i: (hh, j, 0)),
            pl.BlockSpec((None, tk, dv), lambda hh, j, i: (hh, j, 0)),
            pl.BlockSpec((None, tq, dv), lambda hh, j, i: (hh, i, 0)),
            pl.BlockSpec((None, tq, 1), lambda hh, j, i: (hh, i, 0)),
            pl.BlockSpec((None, tq, 1), lambda hh, j, i: (hh, i, 0)),
        ],
        out_specs=[
            pl.BlockSpec((None, seq, dq), lambda hh, j, i: (hh, 0, 0)),
            pl.BlockSpec((None, tk, dq), lambda hh, j, i: (hh, j, 0)),
            pl.BlockSpec((None, tk, dv), lambda hh, j, i: (hh, j, 0)),
        ],
        out_shape=[
            jax.ShapeDtypeStruct((h, seq, dq), F32),
            jax.ShapeDtypeStruct((h, seq, dq), F32),
            jax.ShapeDtypeStruct((h, seq, dv), F32),
        ],
        scratch_shapes=[pltpu.VMEM((tk, dq), F32), pltpu.VMEM((tk, dv), F32)],
        compiler_params=_cparams("parallel", "arbitrary", "arbitrary"),
    )(q, k, v, do, lse, delta)


@jax.custom_vjp
def mla_attention(q, k, v):
    return _mla_fwd_call(q, k, v)[0]


def _mla_attention_fwd(q, k, v):
    o, lse = _mla_fwd_call(q, k, v)
    return o, (q, k, v, o, lse)


def _mla_attention_bwd(res, do):
    q, k, v, o, lse = res
    delta = jnp.sum(do * o, axis=-1, keepdims=True)
    return tuple(_mla_bwd_call(q, k, v, do, lse, delta))


mla_attention.defvjp(_mla_attention_fwd, _mla_attention_bwd)


def _swa_block(q4, kp, kc, kn, vp, vc, vn, bias, sink, *, valid):
    kb = jnp.concatenate([kp, kc, kn], axis=0)
    vb = jnp.concatenate([vp, vc, vn], axis=0)
    s = _dot_nt(q4, kb) * (SWA_HEAD_DIM**-0.5) + bias
    s = jnp.where(valid, s, NEG_BIG)
    m = lax.stop_gradient(jnp.maximum(jnp.max(s, axis=1, keepdims=True), sink))
    p = jnp.exp(s - m)
    denom = jnp.sum(p, axis=1, keepdims=True) + jnp.exp(sink - m)
    return _dot_nn(p / denom, vb)


def _swa_valid(n, seq):
    rows = SWA_GROUP * SWA_BLOCK
    qi = lax.broadcasted_iota(jnp.int32, (rows, 3 * SWA_BLOCK), 0) % SWA_BLOCK
    sj = lax.broadcasted_iota(jnp.int32, (rows, 3 * SWA_BLOCK), 1)
    rel = sj - SWA_BLOCK - qi
    kpos = n * SWA_BLOCK + sj - SWA_BLOCK
    return (jnp.abs(rel) <= WINDOW) & (kpos >= 0) & (kpos < seq)


def _swa_operands(q_ref, kp_ref, kc_ref, kn_ref, vp_ref, vc_ref, vn_ref, b_ref, s_ref):
    q4 = q_ref[...].reshape(SWA_GROUP * SWA_BLOCK, SWA_HEAD_DIM)
    return (q4, kp_ref[...], kc_ref[...], kn_ref[...], vp_ref[...], vc_ref[...], vn_ref[...], b_ref[...], s_ref[...])


def _swa_fwd_body(q_ref, kp_ref, kc_ref, kn_ref, vp_ref, vc_ref, vn_ref, b_ref, s_ref, o_ref, *, seq):
    valid = _swa_valid(pl.program_id(1), seq)
    out = _swa_block(*_swa_operands(q_ref, kp_ref, kc_ref, kn_ref, vp_ref, vc_ref, vn_ref, b_ref, s_ref), valid=valid)
    o_ref[...] = out.reshape(SWA_GROUP, SWA_BLOCK, SWA_HEAD_DIM)


def _swa_bwd_body(q_ref, kp_ref, kc_ref, kn_ref, vp_ref, vc_ref, vn_ref, b_ref, s_ref, do_ref,
                  dq_ref, dk_ref, dv_ref, db_ref, ds_ref, *, seq):
    n = pl.program_id(1)

    @pl.when(n == 0)
    def _():
        db_ref[...] = jnp.zeros_like(db_ref)
        ds_ref[...] = jnp.zeros_like(ds_ref)

    valid = _swa_valid(n, seq)
    ops = _swa_operands(q_ref, kp_ref, kc_ref, kn_ref, vp_ref, vc_ref, vn_ref, b_ref, s_ref)
    _, vjp = jax.vjp(functools.partial(_swa_block, valid=valid), *ops)
    do = do_ref[...].reshape(SWA_GROUP * SWA_BLOCK, SWA_HEAD_DIM)
    dq4, dkp, dkc, dkn, dvp, dvc, dvn, dbias, dsink = vjp(do)
    dq_ref[...] = dq4.reshape(SWA_GROUP, SWA_BLOCK, SWA_HEAD_DIM)
    dk_ref[0] = dkp
    dk_ref[1] = dkc
    dk_ref[2] = dkn
    dv_ref[0] = dvp
    dv_ref[1] = dvc
    dv_ref[2] = dvn
    db_ref[...] += dbias
    ds_ref[...] += dsink


def _swa_in_specs(nb):
    blk = (None, SWA_BLOCK, SWA_HEAD_DIM)
    prev = lambda h, n: (h, jnp.maximum(n - 1, 0), 0)
    own = lambda h, n: (h, n, 0)
    nxt = lambda h, n: (h, jnp.minimum(n + 1, nb - 1), 0)
    rows = SWA_GROUP * SWA_BLOCK
    return [
        pl.BlockSpec((None, SWA_GROUP, SWA_BLOCK, SWA_HEAD_DIM), lambda h, n: (h, 0, n, 0)),
        pl.BlockSpec(blk, prev), pl.BlockSpec(blk, own), pl.BlockSpec(blk, nxt),
        pl.BlockSpec(blk, prev), pl.BlockSpec(blk, own), pl.BlockSpec(blk, nxt),
        pl.BlockSpec((None, rows, 3 * SWA_BLOCK), lambda h, n: (h, 0, 0)),
        pl.BlockSpec((None, rows, 1), lambda h, n: (h, 0, 0)),
    ]


def _swa_fwd_call(q, k, v, bias, sink):
    kv, g, seq, d = q.shape
    nb = seq // SWA_BLOCK
    return pl.pallas_call(
        functools.partial(_swa_fwd_body, seq=seq),
        name="swa_fwd",
        grid=(kv, nb),
        in_specs=_swa_in_specs(nb),
        out_specs=pl.BlockSpec((None, g, SWA_BLOCK, d), lambda h, n: (h, 0, n, 0)),
        out_shape=jax.ShapeDtypeStruct(q.shape, F32),
        compiler_params=_cparams("parallel", "parallel"),
    )(q, k, k, k, v, v, v, bias, sink)


def _swa_bwd_call(q, k, v, bias, sink, do):
    kv, g, seq, d = q.shape
    nb = seq // SWA_BLOCK
    rows = g * SWA_BLOCK
    part = jax.ShapeDtypeStruct((kv, nb, 3, SWA_BLOCK, d), F32)
    part_spec = pl.BlockSpec((None, None, 3, SWA_BLOCK, d), lambda h, n: (h, n, 0, 0, 0))
    dq, dkp, dvp, dbias, dsink = pl.pallas_call(
        functools.partial(_swa_bwd_body, seq=seq),
        name="swa_bwd",
        grid=(kv, nb),
        in_specs=_swa_in_specs(nb) + [pl.BlockSpec((None, g, SWA_BLOCK, d), lambda h, n: (h, 0, n, 0))],
        out_specs=[
            pl.BlockSpec((None, g, SWA_BLOCK, d), lambda h, n: (h, 0, n, 0)),
            part_spec, part_spec,
            pl.BlockSpec((None, rows, 3 * SWA_BLOCK), lambda h, n: (h, 0, 0)),
            pl.BlockSpec((None, rows, 1), lambda h, n: (h, 0, 0)),
        ],
        out_shape=[jax.ShapeDtypeStruct(q.shape, F32), part, part,
                   jax.ShapeDtypeStruct(bias.shape, F32), jax.ShapeDtypeStruct(sink.shape, F32)],
        compiler_params=_cparams("parallel", "arbitrary"),
    )(q, k, k, k, v, v, v, bias, sink, do)

    def fold(p):
        zero = jnp.zeros_like(p[:, :1, 0])
        total = p[:, :, 1] + jnp.concatenate([p[:, 1:, 0], zero], axis=1) + jnp.concatenate([zero, p[:, :-1, 2]], axis=1)
        return total.reshape(kv, seq, d)

    return dq, fold(dkp), fold(dvp), dbias, dsink


@jax.custom_vjp
def swa_attention(q, k, v, bias, sink):
    return _swa_fwd_call(q, k, v, bias, sink)


def _swa_attention_fwd(q, k, v, bias, sink):
    return _swa_fwd_call(q, k, v, bias, sink), (q, k, v, bias, sink)


def _swa_attention_bwd(res, do):
    return _swa_bwd_call(*res, do)


swa_attention.defvjp(_swa_attention_fwd, _swa_attention_bwd)


def _scan_tiles(n_tiles, reverse, tile_fn, init):
    def step(i, carry):
        ti = (n_tiles - 1 - i) if reverse else i
        return tile_fn(pl.multiple_of(ti * SUBLANES, SUBLANES), carry)

    return lax.fori_loop(0, n_tiles, step, init)


def _row_order(reverse):
    return tuple(reversed(range(SUBLANES))) if reverse else tuple(range(SUBLANES))


def _s5_fwd_dir(a_ref, bu_ref, s_ref, carry, *, reverse, tt, w):
    ar, ai = a_ref[0:1, :], a_ref[1:2, :]
    rowid = lax.broadcasted_iota(jnp.int32, (SUBLANES, w), 0)

    def tile(base, c):
        sr, si = c
        x = bu_ref[pl.ds(base, SUBLANES), :]
        xr, xi = x[:, :w], x[:, w:]
        out_r = jnp.zeros((SUBLANES, w), F32)
        out_i = jnp.zeros((SUBLANES, w), F32)
        for j in _row_order(reverse):
            nr = ar * sr - ai * si + xr[j:j + 1, :]
            ni = ar * si + ai * sr + xi[j:j + 1, :]
            out_r = jnp.where(rowid == j, nr, out_r)
            out_i = jnp.where(rowid == j, ni, out_i)
            sr, si = nr, ni
        s_ref[pl.ds(base, SUBLANES), :] = jnp.concatenate([out_r, out_i], axis=1)
        return sr, si

    sr, si = _scan_tiles(tt // SUBLANES, reverse, tile, (carry[0:1, :], carry[1:2, :]))
    carry[0:1, :] = sr
    carry[1:2, :] = si


def _s5_fwd_body(a_ref, bu_ref, s_ref, carry, *, tt, w):
    d = pl.program_id(0)

    @pl.when(pl.program_id(1) == 0)
    def _():
        carry[...] = jnp.zeros_like(carry)

    @pl.when(d == 0)
    def _():
        _s5_fwd_dir(a_ref, bu_ref, s_ref, carry, reverse=False, tt=tt, w=w)

    @pl.when(d == 1)
    def _():
        _s5_fwd_dir(a_ref, bu_ref, s_ref, carry, reverse=True, tt=tt, w=w)


def _s5_bwd_dir(a_ref, s_ref, ds_ref, dbu_ref, da_ref, carry, *, reverse, tt, w):
    ar, ai = a_ref[0:1, :], a_ref[1:2, :]
    rowid = lax.broadcasted_iota(jnp.int32, (SUBLANES, w), 0)

    def tile(base, c):
        lr, li, dar, dai = c
        s = s_ref[pl.ds(base, SUBLANES), :]
        g = ds_ref[pl.ds(base, SUBLANES), :]
        out_r = jnp.zeros((SUBLANES, w), F32)
        out_i = jnp.zeros((SUBLANES, w), F32)
        for j in _row_order(reverse):
            sr, si = s[j:j + 1, :w], s[j:j + 1, w:]
            dar = dar + sr * lr + si * li
            dai = dai + sr * li - si * lr
            nr = g[j:j + 1, :w] + ar * lr + ai * li
            ni = g[j:j + 1, w:] + ar * li - ai * lr
            out_r = jnp.where(rowid == j, nr, out_r)
            out_i = jnp.where(rowid == j, ni, out_i)
            lr, li = nr, ni
        dbu_ref[pl.ds(base, SUBLANES), :] = jnp.concatenate([out_r, out_i], axis=1)
        return lr, li, dar, dai

    zero = jnp.zeros((1, w), F32)
    lr, li, dar, dai = _scan_tiles(tt // SUBLANES, reverse, tile, (carry[0:1, :], carry[1:2, :], zero, zero))
    carry[0:1, :] = lr
    carry[1:2, :] = li
    da_ref[0:1, :] += dar
    da_ref[1:2, :] += dai


def _s5_bwd_body(a_ref, s_ref, ds_ref, dbu_ref, da_ref, carry, *, tt, w):
    d = pl.program_id(0)

    @pl.when(pl.program_id(1) == 0)
    def _():
        carry[...] = jnp.zeros_like(carry)
        da_ref[...] = jnp.zeros_like(da_ref)

    @pl.when(d == 0)
    def _():
        _s5_bwd_dir(a_ref, s_ref, ds_ref, dbu_ref, da_ref, carry, reverse=True, tt=tt, w=w)

    @pl.when(d == 1)
    def _():
        _s5_bwd_dir(a_ref, s_ref, ds_ref, dbu_ref, da_ref, carry, reverse=False, tt=tt, w=w)


def _s5_time_map(nt, flip_dir):
    def time_block(d, t):
        back = nt - 1 - t
        return jnp.where(d == flip_dir, back, t)

    return time_block


def _s5_fwd_call(a, bu):
    seq, w4 = bu.shape
    w = w4 // 4
    tt = _pick(seq, 256, SUBLANES)
    nt = seq // tt
    tb = _s5_time_map(nt, 1)
    return pl.pallas_call(
        functools.partial(_s5_fwd_body, tt=tt, w=w),
        name="s5_scan_fwd",
        grid=(2, nt),
        in_specs=[
            pl.BlockSpec((None, 2, w), lambda d, t: (d, 0, 0)),
            pl.BlockSpec((tt, 2 * w), lambda d, t: (tb(d, t), d)),
        ],
        out_specs=pl.BlockSpec((tt, 2 * w), lambda d, t: (tb(d, t), d)),
        out_shape=jax.ShapeDtypeStruct(bu.shape, F32),
        scratch_shapes=[pltpu.VMEM((2, w), F32)],
        compiler_params=_cparams("parallel", "arbitrary"),
    )(a, bu)


def _s5_bwd_call(a, s_prev, ds):
    seq, w4 = ds.shape
    w = w4 // 4
    tt = _pick(seq, 256, SUBLANES)
    nt = seq // tt
    tb = _s5_time_map(nt, 0)
    blk = pl.BlockSpec((tt, 2 * w), lambda d, t: (tb(d, t), d))
    return pl.pallas_call(
        functools.partial(_s5_bwd_body, tt=tt, w=w),
        name="s5_scan_bwd",
        grid=(2, nt),
        in_specs=[pl.BlockSpec((None, 2, w), lambda d, t: (d, 0, 0)), blk, blk],
        out_specs=[blk, pl.BlockSpec((None, 2, w), lambda d, t: (d, 0, 0))],
        out_shape=[jax.ShapeDtypeStruct(ds.shape, F32), jax.ShapeDtypeStruct(a.shape, F32)],
        scratch_shapes=[pltpu.VMEM((2, w), F32)],
        compiler_params=_cparams("parallel", "arbitrary"),
    )(a, s_prev, ds)


@jax.custom_vjp
def s5_scan(a, bu):
    return _s5_fwd_call(a, bu)


def _s5_scan_fwd(a, bu):
    s = _s5_fwd_call(a, bu)
    return s, (a, s)


def _s5_scan_bwd(res, ds):
    a, s = res
    dbu, da = _s5_bwd_call(a, s, ds)
    return da, dbu


s5_scan.defvjp(_s5_scan_fwd, _s5_scan_bwd)


def _dot3(a, b):
    ah = a.astype(BF16)
    bh = b.astype(BF16)
    al = a - ah.astype(F32)
    bl = b - bh.astype(F32)
    return _dot_nn(ah, bh) + _dot_nn(ah, bl) + _dot_nn(al, bh)


GDN_INV_BASE = 8


def _unit_lower_inverse(a, ri, ci):
    c = a.shape[0]

    def same_block(size):
        shift = int(math.log2(size))
        return lax.shift_right_logical(ri, shift) == lax.shift_right_logical(ci, shift)

    diag = jnp.where(same_block(GDN_INV_BASE), a, 0.0)
    inv = jnp.where(ri == ci, 1.0, 0.0) - diag
    power = diag
    for _ in range(int(math.log2(GDN_INV_BASE)) - 1):
        power = _dot3(power, power)
        inv = inv + _dot3(inv, power)
    size = GDN_INV_BASE
    while size < c:
        off = jnp.where(jnp.logical_and(same_block(2 * size), jnp.logical_not(same_block(size))), a, 0.0)
        inv = inv - _dot3(_dot3(inv, off), inv)
        size *= 2
    return inv


def _gdn_chunk(q, k, v, gc, gr, bc, state):
    c = q.shape[0]
    dv = v.shape[1]
    ri = lax.broadcasted_iota(jnp.int32, (c, c), 0)
    ci = lax.broadcasted_iota(jnp.int32, (c, c), 1)
    lower = ri >= ci
    strict = ri > ci
    kb = k * bc
    decay = jnp.where(lower, jnp.exp(jnp.where(lower, gc - gr, 0.0)), 0.0)
    a = jnp.where(strict, _dot_nt(kb, k) * decay, 0.0)
    inv = _unit_lower_inverse(a, ri, ci)
    eg = jnp.exp(gc)
    sol = _dot3(inv, jnp.concatenate([v * bc, kb * eg], axis=1))
    u, w = sol[:, :dv], sol[:, dv:]
    attn = _dot_nt(q, k) * decay
    v_new = u - _dot_nn(w, state)
    o = _dot_nn(q * eg, state) + _dot_nn(attn, v_new)
    last = lax.broadcasted_iota(jnp.int32, (c, 1), 0) == c - 1
    g_last = jnp.sum(jnp.where(last, gc, 0.0), axis=0, keepdims=True)
    new_state = state * jnp.exp(g_last) + _dot_tn(k * jnp.exp(g_last - gc), v_new)
    return o, new_state


def _gdn_fwd_body(q_ref, k_ref, v_ref, gc_ref, gr_ref, bc_ref, o_ref, s0_ref, state):
    @pl.when(pl.program_id(2) == 0)
    def _():
        state[...] = jnp.zeros_like(state)

    s0 = state[...]
    s0_ref[...] = s0
    o, new_state = _gdn_chunk(q_ref[...], k_ref[...], v_ref[...], gc_ref[...], gr_ref[...], bc_ref[...], s0)
    o_ref[...] = o
    state[...] = new_state


def _gdn_bwd_body(q_ref, k_ref, v_ref, gc_ref, gr_ref, bc_ref, s0_ref, do_ref,
                  dq_ref, dk_ref, dv_ref, dgc_ref, dgr_ref, dbc_ref, dstate):
    @pl.when(pl.program_id(2) == 0)
    def _():
        dstate[...] = jnp.zeros_like(dstate)

    ops = (q_ref[...], k_ref[...], v_ref[...], gc_ref[...], gr_ref[...], bc_ref[...], s0_ref[...])
    _, vjp = jax.vjp(_gdn_chunk, *ops)
    dq, dk, dv, dgc, dgr, dbc, ds0 = vjp((do_ref[...], dstate[...]))
    dq_ref[...] = dq
    dk_ref[...] = dk
    dv_ref[...] = dv
    dgc_ref[...] = dgc
    dgr_ref[...] = dgr
    dbc_ref[...] = dbc
    dstate[...] = ds0


def _gdn_specs(nc, dh, reverse):
    c = GDN_CHUNK
    chunk = (lambda n: nc - 1 - n) if reverse else (lambda n: n)
    seq_blk = pl.BlockSpec((None, c, dh), lambda d, h, n: (d, chunk(n), h))
    col_blk = pl.BlockSpec((None, None, c, 1), lambda d, h, n: (d, h, chunk(n), 0))
    row_blk = pl.BlockSpec((None, None, None, 1, c), lambda d, h, n: (d, h, chunk(n), 0, 0))
    st_blk = pl.BlockSpec((None, None, None, dh, dh), lambda d, h, n: (d, h, chunk(n), 0, 0))
    return seq_blk, col_blk, row_blk, st_blk


def _gdn_fwd_call(q, k, v, gc, gr, bc):
    _, seq, width = q.shape
    heads, dh = gc.shape[1], width // gc.shape[1]
    nc = seq // GDN_CHUNK
    seq_blk, col_blk, row_blk, st_blk = _gdn_specs(nc, dh, False)
    return pl.pallas_call(
        functools.partial(_gdn_fwd_body),
        name="gdn_fwd",
        grid=(2, heads, nc),
        in_specs=[seq_blk, seq_blk, seq_blk, col_blk, row_blk, col_blk],
        out_specs=[seq_blk, st_blk],
        out_shape=[jax.ShapeDtypeStruct(q.shape, F32), jax.ShapeDtypeStruct((2, heads, nc, dh, dh), F32)],
        scratch_shapes=[pltpu.VMEM((dh, dh), F32)],
        compiler_params=_cparams("parallel", "parallel", "arbitrary"),
    )(q, k, v, gc, gr, bc)


def _gdn_bwd_call(q, k, v, gc, gr, bc, s0, do):
    _, seq, width = q.shape
    heads, dh = gc.shape[1], width // gc.shape[1]
    nc = seq // GDN_CHUNK
    seq_blk, col_blk, row_blk, st_blk = _gdn_specs(nc, dh, True)
    return pl.pallas_call(
        functools.partial(_gdn_bwd_body),
        name="gdn_bwd",
        grid=(2, heads, nc),
        in_specs=[seq_blk, seq_blk, seq_blk, col_blk, row_blk, col_blk, st_blk, seq_blk],
        out_specs=[seq_blk, seq_blk, seq_blk, col_blk, row_blk, col_blk],
        out_shape=[jax.ShapeDtypeStruct(q.shape, F32)] * 3
        + [jax.ShapeDtypeStruct(gc.shape, F32), jax.ShapeDtypeStruct(gr.shape, F32), jax.ShapeDtypeStruct(bc.shape, F32)],
        scratch_shapes=[pltpu.VMEM((dh, dh), F32)],
        compiler_params=_cparams("parallel", "parallel", "arbitrary"),
    )(q, k, v, gc, gr, bc, s0, do)


@jax.custom_vjp
def gdn_delta_rule(q, k, v, gc, gr, bc):
    return _gdn_fwd_call(q, k, v, gc, gr, bc)[0]


def _gdn_delta_rule_fwd(q, k, v, gc, gr, bc):
    o, s0 = _gdn_fwd_call(q, k, v, gc, gr, bc)
    return o, (q, k, v, gc, gr, bc, s0)


def _gdn_delta_rule_bwd(res, do):
    return tuple(_gdn_bwd_call(*res, do))


gdn_delta_rule.defvjp(_gdn_delta_rule_fwd, _gdn_delta_rule_bwd)


def _mesh_position():
    return lax.axis_index("x"), lax.axis_index("y"), lax.axis_index("c")


def _all_gather_body(x_ref, out_ref, send_sems, recv_sems, local_sem):
    x, y, c = _mesh_position()
    me, sibling = (x, y, c), (x, y, 1 - c)
    chips = [(1 - x, y), (x, 1 - y), (1 - x, 1 - y)]

    def slot(px, py, pc):
        return out_ref.at[4 * px + 2 * py + pc]

    def copy(k, block, to, src=None):
        return pltpu.make_async_remote_copy(
            src_ref=slot(*block) if src is None else src, dst_ref=slot(*block),
            send_sem=send_sems.at[k], recv_sem=recv_sems.at[k], device_id=to, device_id_type=pl.DeviceIdType.MESH)

    mine = pltpu.make_async_copy(x_ref, slot(*me), local_sem)
    mine.start()
    first = [copy(0, me, sibling, src=x_ref)]
    first += [copy(1 + j, me, (*chip, c), src=x_ref) for j, chip in enumerate(chips)]
    for cp in first:
        cp.start()
    passed = [copy(4 + j, (*chip, c), sibling) for j, chip in enumerate(chips)]
    for j, chip in enumerate(chips):
        copy(1 + j, (*chip, c), me).wait_recv()
        passed[j].start()
    copy(0, sibling, me).wait_recv()
    for j, chip in enumerate(chips):
        copy(4 + j, (*chip, 1 - c), me).wait_recv()
    for cp in first + passed:
        cp.wait_send()
    mine.wait()


def all_gather(x, name):
    return pl.pallas_call(
        functools.partial(_all_gather_body),
        name=name,
        in_specs=[pl.BlockSpec(memory_space=pl.ANY)],
        out_specs=pl.BlockSpec(memory_space=pl.ANY),
        out_shape=jax.ShapeDtypeStruct((N_DEV,) + x.shape, x.dtype),
        scratch_shapes=[pltpu.SemaphoreType.DMA((N_DEV - 1,)), pltpu.SemaphoreType.DMA((N_DEV - 1,)), pltpu.SemaphoreType.DMA],
    )(x)


def _all_to_all_body(x_ref, out_ref, send_sems, recv_sems, local_sem):
    x, y, c = _mesh_position()
    my = 4 * x + 2 * y + c
    copies = []
    for m in range(1, N_DEV):
        px = 1 - x if m & 4 else x
        py = 1 - y if m & 2 else y
        pc = 1 - c if m & 1 else c
        copies.append(pltpu.make_async_remote_copy(
            src_ref=x_ref.at[4 * px + 2 * py + pc], dst_ref=out_ref.at[my],
            send_sem=send_sems.at[m - 1], recv_sem=recv_sems.at[m - 1],
            device_id=(px, py, pc), device_id_type=pl.DeviceIdType.MESH))
    for cp in copies:
        cp.start()
    mine = pltpu.make_async_copy(x_ref.at[my], out_ref.at[my], local_sem)
    mine.start()
    for cp in copies:
        cp.wait()
    mine.wait()


def all_to_all(x, name):
    return pl.pallas_call(
        functools.partial(_all_to_all_body),
        name=name,
        in_specs=[pl.BlockSpec(memory_space=pl.ANY)],
        out_specs=pl.BlockSpec(memory_space=pl.ANY),
        out_shape=jax.ShapeDtypeStruct(x.shape, x.dtype),
        scratch_shapes=[pltpu.SemaphoreType.DMA((N_DEV - 1,)), pltpu.SemaphoreType.DMA((N_DEV - 1,)), pltpu.SemaphoreType.DMA],
    )(x)


PACK_COLS = 1024
PACK_ROW_MULTIPLE = 512


def _pack(arrays, dtype):
    parts = []
    rows = 0
    for a in arrays:
        flat = a.reshape(-1).astype(dtype)
        n_rows = -(-flat.shape[0] // PACK_COLS)
        parts.append(jnp.pad(flat, (0, n_rows * PACK_COLS - flat.shape[0])).reshape(n_rows, PACK_COLS))
        rows += n_rows
    pad_rows = -rows % PACK_ROW_MULTIPLE
    if pad_rows:
        parts.append(jnp.zeros((pad_rows, PACK_COLS), dtype))
    return jnp.concatenate(parts, axis=0)


def _unpack(flat, shapes):
    lead = flat.shape[:-2]
    out = []
    row = 0
    for shape in shapes:
        size = int(np.prod(shape))
        n_rows = -(-size // PACK_COLS)
        seg = flat[..., row:row + n_rows, :].reshape(lead + (n_rows * PACK_COLS,))[..., :size]
        out.append(seg.reshape(lead + tuple(shape)))
        row += n_rows
    return out


W_IN_SMALL = 4608


def _split_w_in(w_in):
    d = w_in.shape[0]
    small = jnp.concatenate(
        [w_in[:, 0:2560], w_in[:, 2576:4304], w_in[:, 2560:2576], jnp.zeros((d, W_IN_SMALL - 4304), w_in.dtype)], axis=1)
    return small, w_in[:, 4304:]


def _s5_mixer(u, lam_re, lam_im, log_step, b_re, b_im, c_re, c_im, d_skip, w_glu, b_glu):
    g, p, hg = S5_GROUPS, S5_STATE, S5_GROUP
    lam_re = jnp.minimum(lam_re, -1e-4)
    dt = jnp.exp(log_step)[..., None]
    mag = jnp.exp(lam_re * dt)
    abar_r = mag * jnp.cos(lam_im * dt)
    abar_i = mag * jnp.sin(lam_im * dt)
    den = lam_re * lam_re + lam_im * lam_im
    xr = abar_r - 1.0
    xi = abar_i
    coef_r = (xr * lam_re + xi * lam_im) / den
    coef_i = (xi * lam_re - xr * lam_im) / den
    bbar_r = coef_r[..., None] * b_re - coef_i[..., None] * b_im
    bbar_i = coef_r[..., None] * b_im + coef_i[..., None] * b_re
    eye = jnp.eye(g, dtype=F32)

    def block_diag(t, rows, cols):
        return (eye[:, None, :, None] * t[:, :, None, :]).reshape(g * rows, g * cols)

    b_all = jnp.concatenate(
        [block_diag(t.transpose(0, 2, 1), hg, p) for t in (bbar_r[0], bbar_i[0], bbar_r[1], bbar_i[1])], axis=1)
    c_all = jnp.concatenate(
        [block_diag(t.transpose(0, 2, 1), p, hg) for t in (c_re[0], -c_im[0], c_re[1], -c_im[1])], axis=0)
    a_all = jnp.stack([abar_r.reshape(2, g * p), abar_i.reshape(2, g * p)], axis=1)
    s = s5_scan(a_all, mm(u, b_all))
    y = mm(s, c_all) + d_skip * u
    y = jax.nn.gelu(y)
    return y * jax.nn.sigmoid(mm(y, w_glu) + b_glu)


def _gdn_mixer(qkv, z, beta_logits, decay_logits, conv_w, a_log, dt_bias, o_gain):
    seq = qkv.shape[0]
    h, dh, c = GDN_HEADS, GDN_HEAD_DIM, GDN_CHUNK
    nc = seq // c
    padded = jnp.pad(qkv, ((GDN_CONV // 2, GDN_CONV - 1 - GDN_CONV // 2), (0, 0)))
    conv = sum(padded[j:j + seq] * conv_w[j] for j in range(GDN_CONV))
    q, k, v = jnp.split(jax.nn.silu(conv), 3, axis=-1)

    def l2(t):
        t = t.reshape(seq, h, dh)
        return (t * lax.rsqrt(jnp.sum(t * t, axis=-1, keepdims=True) + 1e-6)).reshape(seq, h * dh)

    q = l2(q) * (dh**-0.5)
    k = l2(k)
    beta = jax.nn.sigmoid(beta_logits).reshape(seq, 2, h)
    g = -jnp.exp(a_log) * jax.nn.softplus(decay_logits.reshape(seq, 2, h) + dt_bias)
    flip = lambda t: jnp.flip(t, axis=0)
    both = lambda t: jnp.stack([t, flip(t)])
    g2 = jnp.stack([g[:, 0], flip(g[:, 1])])
    beta2 = jnp.stack([beta[:, 0], flip(beta[:, 1])])
    gcs = jnp.cumsum(g2.reshape(2, nc, c, h), axis=2).transpose(0, 3, 1, 2)
    o2 = gdn_delta_rule(both(q), both(k), both(v), gcs.reshape(2, h, seq, 1), gcs.reshape(2, h, nc, 1, c),
                        beta2.transpose(0, 2, 1).reshape(2, h, seq, 1))
    o = o2[0] + flip(o2[1])
    o = rmsnorm(o.reshape(seq * h, dh), o_gain).reshape(seq, h * dh)
    return o * jax.nn.silu(z)


def _t5_bucket(rel):
    nb = T5_BUCKETS // 2
    max_exact = nb // 2
    ret = jnp.where(rel > 0, nb, 0)
    n = jnp.abs(rel)
    nf = jnp.maximum(n, 1).astype(F32)
    large = max_exact + (jnp.log(nf / max_exact) / math.log(T5_MAX_DISTANCE / max_exact) * (nb - max_exact)).astype(jnp.int32)
    large = jnp.minimum(large, nb - 1)
    return ret + jnp.where(n < max_exact, n, large)


def _swa_mixer(q, kv, sink, t5_bias):
    seq = q.shape[0]
    kvh, g, d, blk = SWA_KV_HEADS, SWA_GROUP, SWA_HEAD_DIM, SWA_BLOCK
    q4 = q.reshape(seq, kvh, g, d).transpose(1, 2, 0, 3)
    k, v = jnp.split(kv, 2, axis=-1)
    heads_first = lambda t: t.reshape(seq, kvh, d).transpose(1, 0, 2)
    rel = jnp.arange(3 * blk)[None, :] - blk - jnp.arange(blk)[:, None]
    onehot = (_t5_bucket(rel)[..., None] == jnp.arange(T5_BUCKETS)).astype(F32)
    bias = jnp.einsum("qsb,bh->hqs", onehot, t5_bias, precision=lax.Precision.HIGHEST).reshape(kvh, g * blk, 3 * blk)
    sink_col = jnp.broadcast_to(sink.reshape(kvh, g, 1, 1), (kvh, g, blk, 1)).reshape(kvh, g * blk, 1)
    o = swa_attention(q4, heads_first(k), heads_first(v), bias, sink_col)
    return o.transpose(2, 0, 1, 3).reshape(seq, kvh * g * d)


def _rope(t, cos, sin):
    t1, t2 = jnp.split(t, 2, axis=-1)
    return jnp.concatenate([t1 * cos - t2 * sin, t2 * cos + t1 * sin], axis=-1)


def _mla_mixer(c_q, c_kv, k_rope, q_gain, kv_gain, w_uq, w_ukv):
    seq = c_q.shape[0]
    h = MLA_HEADS
    q = mm(rmsnorm(c_q, q_gain), w_uq).reshape(seq, h, MLA_NOPE + MLA_ROPE)
    kv = mm(rmsnorm(c_kv, kv_gain), w_ukv).reshape(seq, h, MLA_NOPE + MLA_V)
    q_nope, q_pe = q[..., :MLA_NOPE], q[..., MLA_NOPE:]
    k_nope, v = kv[..., :MLA_NOPE], kv[..., MLA_NOPE:]
    pos = jnp.arange(seq, dtype=F32)
    inv_freq = ROPE_THETA ** (-jnp.arange(0, MLA_ROPE, 2, dtype=F32) / MLA_ROPE)
    ang = pos[:, None] * inv_freq[None, :]
    cos, sin = jnp.cos(ang)[:, None, :], jnp.sin(ang)[:, None, :]
    q_pe = _rope(q_pe, cos, sin)
    k_pe = _rope(k_rope[:, None, :], cos, sin)
    qf = jnp.concatenate([q_nope, q_pe], axis=-1)
    kf = jnp.concatenate([k_nope, jnp.broadcast_to(k_pe, (seq, h, MLA_ROPE))], axis=-1)
    o = mla_attention(qf.transpose(1, 0, 2), kf.transpose(1, 0, 2), v.transpose(1, 0, 2))
    return o.transpose(1, 0, 2).reshape(seq, h * MLA_V)


def _layer(x, p, t5_bias):
    d = x.shape[1]
    h = rmsnorm(x, p["mix_pre_gain"])
    w_small, w_gate = _split_w_in(p["w_in"])
    ps = mm(h, w_small)
    gate_logits = mm(h, w_gate)
    y_a = _s5_mixer(ps[:, 0:512], p["s5_lam_re"], p["s5_lam_im"], p["s5_log_step"], p["s5_b_re"], p["s5_b_im"],
                    p["s5_c_re"], p["s5_c_im"], p["s5_d"], p["s5_w_glu"], p["s5_b_glu"])
    y_b = _gdn_mixer(ps[:, 512:2048], ps[:, 2048:2560], ps[:, 4288:4296], ps[:, 4296:4304], p["gdn_conv"],
                     p["gdn_a_log"], p["gdn_dt_bias"], p["gdn_o_gain"])
    y_c = _swa_mixer(ps[:, 2560:3072], ps[:, 3072:3328], p["swa_sink"], t5_bias)
    y_d = _mla_mixer(ps[:, 3328:3712], ps[:, 3712:4224], ps[:, 4224:4288], p["mla_q_gain"], p["mla_kv_gain"],
                     p["mla_w_uq"], p["mla_w_ukv"])
    merged = sum(jax.nn.sigmoid(gate_logits[:, b * d:(b + 1) * d]) * mm(y, p["w_branch"][b])
                 for b, y in enumerate((y_a, y_b, y_c, y_d)))
    x = x + rmsnorm(mm(merged, p["w_out"]), p["mix_post_gain"])
    h = rmsnorm(x, p["mlp_pre_gain"])
    f = mm(jnp.square(jax.nn.relu(mm(h, p["w_mlp_in"]))), p["w_mlp_out"])
    return x + rmsnorm(f, p["mlp_post_gain"])


LAYER_WEIGHTS = ("w_in", "s5_lam_re", "s5_lam_im", "s5_log_step", "s5_b_re", "s5_b_im", "s5_c_re", "s5_c_im", "s5_d",
                 "s5_w_glu", "s5_b_glu", "gdn_conv", "gdn_a_log", "gdn_dt_bias", "gdn_o_gain", "swa_sink", "mla_q_gain",
                 "mla_kv_gain", "mla_w_uq", "mla_w_ukv", "w_branch", "w_out", "mix_pre_gain", "mix_post_gain",
                 "mlp_pre_gain", "mlp_post_gain", "w_mlp_in", "w_mlp_out")


def _forward(x, weights):
    for layer in range(DEPTH):
        x = _layer(x, {n: weights[n][layer] for n in LAYER_WEIGHTS}, weights["t5_bias"])
    return x


WEIGHT_NAMES = ("w_in", "s5_lam_re", "s5_lam_im", "s5_log_step", "s5_b_re", "s5_b_im", "s5_c_re", "s5_c_im", "s5_d",
                "s5_w_glu", "s5_b_glu", "gdn_conv", "gdn_a_log", "gdn_dt_bias", "gdn_o_gain", "swa_sink", "t5_bias",
                "mla_q_gain", "mla_kv_gain", "mla_w_uq", "mla_w_ukv", "w_branch", "w_out", "mix_pre_gain", "mix_post_gain",
                "mlp_pre_gain", "mlp_post_gain", "w_mlp_in", "w_mlp_out")
SHARD_AXIS = {"w_in": 2, "s5_w_glu": 1, "mla_w_uq": 2, "mla_w_ukv": 2, "w_branch": 3, "w_out": 1, "w_mlp_in": 2,
              "w_mlp_out": 1}
CONV = "gdn_conv"
CONV_AXIS = 2
REPLICATED = tuple(n for n in WEIGHT_NAMES if n not in SHARD_AXIS and n != CONV)


def _step(x, target, w, m, v):
    big = tuple(SHARD_AXIS)
    x_pos, y_pos, c_pos = _mesh_position()
    my = 4 * x_pos + 2 * y_pos + c_pos

    shard_shapes = [w[n].shape for n in big]
    gathered = _unpack(all_gather(_pack([w[n] for n in big], BF16), "gather_weights"), shard_shapes)
    full = {n: jnp.concatenate([g[k] for k in range(N_DEV)], axis=SHARD_AXIS[n]).astype(F32) for n, g in zip(big, gathered)}
    conv_all = _unpack(all_gather(_pack([w[CONV]], F32), "gather_conv"), [w[CONV].shape])[0]
    full[CONV] = jnp.concatenate([conv_all[k] for k in range(N_DEV)], axis=CONV_AXIS)
    for n in REPLICATED:
        full[n] = w[n]

    y, vjp = jax.vjp(_forward, x, full)
    loss_rows, dy = loss_head(y, target)
    grad_x, grad_full = vjp(dy)
    loss_part = jnp.sum(loss_rows)

    def pieces(n, k):
        size = w[n].shape[SHARD_AXIS[n]]
        return lax.slice_in_dim(grad_full[n], k * size, (k + 1) * size, axis=SHARD_AXIS[n])

    send = jnp.stack([_pack([pieces(n, k) for n in big], BF16) for k in range(N_DEV)])
    grad_big = sum_leading(all_to_all(send, "exchange_grads"))
    small_names = REPLICATED + (CONV,)
    small = [grad_full[n] for n in small_names] + [loss_part.reshape(1)]
    small_sum = sum_leading(all_gather(_pack(small, F32), "gather_small_grads"))
    small_grads = _unpack(small_sum, [a.shape for a in small])
    loss = small_grads[-1][0]
    grads = dict(zip(small_names, small_grads[:-1]))
    conv_size = w[CONV].shape[CONV_AXIS]
    grads[CONV] = lax.dynamic_slice_in_dim(grads[CONV], my * conv_size, conv_size, axis=CONV_AXIS)
    grads.update(zip(big, _unpack(grad_big, shard_shapes)))

    big_rows = grad_big.shape[0]
    small_shapes = [w[n].shape for n in small_names]
    flat = lambda t: jnp.concatenate([_pack([t[n] for n in big], F32), _pack([t[n] for n in small_names], F32)], axis=0)
    g_flat = jnp.concatenate([grad_big, _pack([grads[n] for n in small_names], F32)], axis=0)

    def named(t):
        return dict(zip(big + small_names, _unpack(t[:big_rows], shard_shapes) + _unpack(t[big_rows:], small_shapes)))

    delta, new_m, new_v = (named(t) for t in adamw_flat(flat(w), g_flat, flat(m), flat(v)))
    return loss, grad_x, grads, delta, new_m, new_v


def kernel(x, w_in, s5_lam_re, s5_lam_im, s5_log_step, s5_b_re, s5_b_im, s5_c_re, s5_c_im, s5_d, s5_w_glu, s5_b_glu, gdn_conv, gdn_a_log, gdn_dt_bias, gdn_o_gain, swa_sink, t5_bias, mla_q_gain, mla_kv_gain, mla_w_uq, mla_w_ukv, w_branch, w_out, mix_pre_gain, mix_post_gain, mlp_pre_gain, mlp_post_gain, w_mlp_in, w_mlp_out, loss_target, m_w_in, m_s5_lam_re, m_s5_lam_im, m_s5_log_step, m_s5_b_re, m_s5_b_im, m_s5_c_re, m_s5_c_im, m_s5_d, m_s5_w_glu, m_s5_b_glu, m_gdn_conv, m_gdn_a_log, m_gdn_dt_bias, m_gdn_o_gain, m_swa_sink, m_t5_bias, m_mla_q_gain, m_mla_kv_gain, m_mla_w_uq, m_mla_w_ukv, m_w_branch, m_w_out, m_mix_pre_gain, m_mix_post_gain, m_mlp_pre_gain, m_mlp_post_gain, m_w_mlp_in, m_w_mlp_out, v_w_in, v_s5_lam_re, v_s5_lam_im, v_s5_log_step, v_s5_b_re, v_s5_b_im, v_s5_c_re, v_s5_c_im, v_s5_d, v_s5_w_glu, v_s5_b_glu, v_gdn_conv, v_gdn_a_log, v_gdn_dt_bias, v_gdn_o_gain, v_swa_sink, v_t5_bias, v_mla_q_gain, v_mla_kv_gain, v_mla_w_uq, v_mla_w_ukv, v_w_branch, v_w_out, v_mix_pre_gain, v_mix_post_gain, v_mlp_pre_gain, v_mlp_post_gain, v_w_mlp_in, v_w_mlp_out):
    args = locals()
    w = {n: args[n] for n in WEIGHT_NAMES}
    m = {n: args["m_" + n] for n in WEIGHT_NAMES}
    v = {n: args["v_" + n] for n in WEIGHT_NAMES}
    loss, grad_x, grads, delta, new_m, new_v = _step(x[0], loss_target[0], w, m, v)
    return (loss, grad_x[None], *[grads[n] for n in WEIGHT_NAMES], *[delta[n] for n in WEIGHT_NAMES],
            *[new_m[n] for n in WEIGHT_NAMES], *[new_v[n] for n in WEIGHT_NAMES])
```

```python
import functools
import math

import jax
import jax.numpy as jnp
import numpy as np
from jax import lax
from jax.experimental import pallas as pl
from jax.experimental.pallas import tpu as pltpu

F32 = jnp.float32
BF16 = jnp.bfloat16

VMEM_LIMIT_BYTES = 56 * 1024 * 1024
LANES = 128
SUBLANES = 8

N_DEV = 8
MESH_AXES = ("x", "y", "c")

DEPTH = 4
N_BRANCHES = 4
BRANCH_WIDTH = 512
NORM_EPS = 1e-6
S5_GROUP = 16
S5_GROUPS = 32
S5_STATE = 64
S5_WIDTH = S5_GROUPS * S5_STATE
GDN_HEAD_DIM = 128
GDN_HEADS = 4
GDN_CONV = 4
GDN_CHUNK = 64
SWA_HEAD_DIM = 64
SWA_HEADS = 8
SWA_KV_HEADS = 2
SWA_GROUP = SWA_HEADS // SWA_KV_HEADS
WINDOW = 128
SWA_BLOCK = 128
T5_BUCKETS = 32
T5_MAX_DISTANCE = 128
MLA_HEADS = 4
MLA_Q_RANK = 384
MLA_KV_RANK = 512
MLA_NOPE = 128
MLA_ROPE = 64
MLA_V = 128
ROPE_THETA = 10000.0

ADAM_LR = 0.001
ADAM_B1 = 0.9
ADAM_B2 = 0.999
ADAM_EPS = 1e-08
ADAM_WD = 0.01
ADAM_STEP = 10

NEG_BIG = -1e30


def _cparams(*sem):
    return pltpu.CompilerParams(dimension_semantics=sem if sem else None, vmem_limit_bytes=VMEM_LIMIT_BYTES)


def _pick(n, pref, unit):
    if n <= pref:
        return n
    t = (pref // unit) * unit
    while t >= unit:
        if n % t == 0:
            return t
        t -= unit
    return n


def _bdot(a, b, dims):
    return lax.dot_general(a.astype(BF16), b.astype(BF16), (dims, ((), ())), preferred_element_type=F32)


def _dot_nn(a, b):
    return _bdot(a, b, ((1,), (0,)))


def _dot_nt(a, b):
    return _bdot(a, b, ((1,), (1,)))


def _dot_tn(a, b):
    return _bdot(a, b, ((0,), (0,)))


def _mm_body(a_ref, b_ref, o_ref, *, ta, tb, nk):
    dims = ((0,) if ta else (1,), (1,) if tb else (0,))
    part = _bdot(a_ref[...], b_ref[...], dims)
    if nk == 1:
        o_ref[...] = part
    else:
        k = pl.program_id(2)

        @pl.when(k == 0)
        def _():
            o_ref[...] = part

        @pl.when(k > 0)
        def _():
            o_ref[...] += part


MM_BLOCK_BYTES = 32 * 1024 * 1024


def _mm_tiles(m, n, k, a_bytes, b_bytes):
    tm, tn = _pick(m, 1024, LANES), _pick(n, 1024, LANES)
    tk = LANES if k % LANES == 0 else k
    for cand in range(k, 0, -LANES) if k % LANES == 0 else (k,):
        if k % cand == 0 and 2 * cand * (tm * a_bytes + tn * b_bytes) + 2 * tm * tn * 4 <= MM_BLOCK_BYTES:
            tk = cand
            break
    return tm, tn, tk


def _mm_call(a, b, *, ta=False, tb=False, name):
    m, k = (a.shape[1], a.shape[0]) if ta else a.shape
    n = b.shape[0] if tb else b.shape[1]
    assert (b.shape[1] if tb else b.shape[0]) == k, (a.shape, b.shape, ta, tb)
    tm, tn, tk = _mm_tiles(m, n, k, a.dtype.itemsize, b.dtype.itemsize)
    nk = k // tk
    a_spec = pl.BlockSpec((tk, tm), lambda i, j, kk: (kk, i)) if ta else pl.BlockSpec((tm, tk), lambda i, j, kk: (i, kk))
    b_spec = pl.BlockSpec((tn, tk), lambda i, j, kk: (j, kk)) if tb else pl.BlockSpec((tk, tn), lambda i, j, kk: (kk, j))
    return pl.pallas_call(
        functools.partial(_mm_body, ta=ta, tb=tb, nk=nk),
        name=name,
        grid=(m // tm, n // tn, nk),
        in_specs=[a_spec, b_spec],
        out_specs=pl.BlockSpec((tm, tn), lambda i, j, kk: (i, j)),
        out_shape=jax.ShapeDtypeStruct((m, n), F32),
        compiler_params=_cparams("parallel", "parallel", "arbitrary"),
    )(a, b)


@jax.custom_vjp
def mm(a, b):
    return _mm_call(a, b, name="mm_fwd")


def _mm_fwd(a, b):
    return _mm_call(a, b, name="mm_fwd"), (a, b)


def _mm_bwd(res, g):
    a, b = res
    return _mm_call(g, b, tb=True, name="mm_da"), _mm_call(a, g, ta=True, name="mm_db")


mm.defvjp(_mm_fwd, _mm_bwd)


@jax.custom_vjp
def wmm(a, w, w_bf16):
    return _mm_call(a, w_bf16, name="wmm_fwd")


def _wmm_fwd(a, w, w_bf16):
    return _mm_call(a, w_bf16, name="wmm_fwd"), (a, w_bf16)


def _wmm_bwd(res, g):
    a, w_bf16 = res
    return _mm_call(g, w_bf16, tb=True, name="wmm_da"), _mm_call(a, g, ta=True, name="wmm_db"), jnp.zeros_like(w_bf16)


wmm.defvjp(_wmm_fwd, _wmm_bwd)


def _rms_fwd_body(x_ref, g_ref, y_ref):
    x = x_ref[...]
    r = lax.rsqrt(jnp.mean(x * x, axis=-1, keepdims=True) + NORM_EPS)
    y_ref[...] = x * r * g_ref[...]


def _rms_bwd_body(x_ref, g_ref, dy_ref, dx_ref, dg_ref, *, tr):
    i = pl.program_id(0)

    @pl.when(i == 0)
    def _():
        dg_ref[...] = jnp.zeros_like(dg_ref)

    x = x_ref[...]
    dy = dy_ref[...]
    r = lax.rsqrt(jnp.mean(x * x, axis=-1, keepdims=True) + NORM_EPS)
    xhat = x * r
    gy = dy * g_ref[...]
    dx_ref[...] = r * (gy - xhat * jnp.mean(gy * xhat, axis=-1, keepdims=True))
    dg_ref[...] += jnp.sum((dy * xhat).reshape(tr // SUBLANES, SUBLANES, x.shape[-1]), axis=0)


def _rms_rows(rows, cols):
    return _pick(rows, max(SUBLANES, (2 * 1024 * 1024) // (4 * cols)), SUBLANES)


def _rms_fwd_call(x, gain):
    rows, cols = x.shape
    tr = _rms_rows(rows, cols)
    return pl.pallas_call(
        functools.partial(_rms_fwd_body),
        name="rms_fwd",
        grid=(rows // tr,),
        in_specs=[pl.BlockSpec((tr, cols), lambda i: (i, 0)), pl.BlockSpec((1, cols), lambda i: (0, 0))],
        out_specs=pl.BlockSpec((tr, cols), lambda i: (i, 0)),
        out_shape=jax.ShapeDtypeStruct((rows, cols), F32),
        compiler_params=_cparams("parallel"),
    )(x, gain.reshape(1, cols))


def _rms_bwd_call(x, gain, dy):
    rows, cols = x.shape
    tr = _rms_rows(rows, cols)
    dx, dg = pl.pallas_call(
        functools.partial(_rms_bwd_body, tr=tr),
        name="rms_bwd",
        grid=(rows // tr,),
        in_specs=[
            pl.BlockSpec((tr, cols), lambda i: (i, 0)),
            pl.BlockSpec((1, cols), lambda i: (0, 0)),
            pl.BlockSpec((tr, cols), lambda i: (i, 0)),
        ],
        out_specs=[pl.BlockSpec((tr, cols), lambda i: (i, 0)), pl.BlockSpec((SUBLANES, cols), lambda i: (0, 0))],
        out_shape=[jax.ShapeDtypeStruct((rows, cols), F32), jax.ShapeDtypeStruct((SUBLANES, cols), F32)],
        compiler_params=_cparams("arbitrary"),
    )(x, gain.reshape(1, cols), dy)
    return dx, jnp.sum(dg, axis=0)


@jax.custom_vjp
def rmsnorm(x, gain):
    return _rms_fwd_call(x, gain)


def _rmsnorm_fwd(x, gain):
    return _rms_fwd_call(x, gain), (x, gain)


def _rmsnorm_bwd(res, dy):
    x, gain = res
    return _rms_bwd_call(x, gain, dy)


rmsnorm.defvjp(_rmsnorm_fwd, _rmsnorm_bwd)


def _loss_body(y_ref, t_ref, rows_ref, dy_ref):
    d = y_ref[...] - t_ref[...]
    rows_ref[...] = 0.5 * jnp.mean(d * d, axis=-1, keepdims=True)
    dy_ref[...] = d * (1.0 / d.shape[-1])


def loss_head(y, target):
    rows, cols = y.shape
    tr = _rms_rows(rows, cols)
    return pl.pallas_call(
        functools.partial(_loss_body),
        name="loss_head",
        grid=(rows // tr,),
        in_specs=[pl.BlockSpec((tr, cols), lambda i: (i, 0))] * 2,
        out_specs=[pl.BlockSpec((tr, 1), lambda i: (i, 0)), pl.BlockSpec((tr, cols), lambda i: (i, 0))],
        out_shape=[jax.ShapeDtypeStruct((rows, 1), F32), jax.ShapeDtypeStruct((rows, cols), F32)],
        compiler_params=_cparams("parallel"),
    )(y, target)


def _adamw_body(w_ref, g_ref, m_ref, v_ref, d_ref, nm_ref, nv_ref):
    g = g_ref[...]
    m = ADAM_B1 * m_ref[...] + (1.0 - ADAM_B1) * g
    v = ADAM_B2 * v_ref[...] + (1.0 - ADAM_B2) * (g * g)
    m_hat = m / (1.0 - ADAM_B1**ADAM_STEP)
    v_hat = v / (1.0 - ADAM_B2**ADAM_STEP)
    d_ref[...] = -ADAM_LR * (m_hat / (jnp.sqrt(v_hat) + ADAM_EPS) + ADAM_WD * w_ref[...])
    nm_ref[...] = m
    nv_ref[...] = v


def adamw_flat(w, g, m, v):
    rows, cols = w.shape
    tr = _pick(rows, 512, SUBLANES)
    spec = pl.BlockSpec((tr, cols), lambda i: (i, 0))
    return pl.pallas_call(
        functools.partial(_adamw_body),
        name="adamw",
        grid=(rows // tr,),
        in_specs=[spec] * 4,
        out_specs=[spec] * 3,
        out_shape=[jax.ShapeDtypeStruct((rows, cols), F32)] * 3,
        compiler_params=_cparams("parallel"),
    )(w, g, m, v)


def _sum_body(x_ref, o_ref, *, n):
    acc = x_ref[0].astype(F32)
    for k in range(1, n):
        acc = acc + x_ref[k].astype(F32)
    o_ref[...] = acc


def sum_leading(x):
    n, rows, cols = x.shape
    tr = _pick(rows, 256, 2 * SUBLANES)
    return pl.pallas_call(
        functools.partial(_sum_body, n=n),
        name="sum_leading",
        grid=(rows // tr,),
        in_specs=[pl.BlockSpec((n, tr, cols), lambda i: (0, i, 0))],
        out_specs=pl.BlockSpec((tr, cols), lambda i: (i, 0)),
        out_shape=jax.ShapeDtypeStruct((rows, cols), F32),
        compiler_params=_cparams("parallel"),
    )(x)


def _mla_fwd_body(q_ref, k_ref, v_ref, o_ref, lse_ref, m_sc, l_sc, acc_sc, *, scale, nk):
    ki = pl.program_id(2)

    @pl.when(ki == 0)
    def _():
        m_sc[...] = jnp.full_like(m_sc, NEG_BIG)
        l_sc[...] = jnp.zeros_like(l_sc)
        acc_sc[...] = jnp.zeros_like(acc_sc)

    s = _dot_nt(q_ref[...], k_ref[...]) * scale
    m_prev = m_sc[...]
    m_new = jnp.maximum(m_prev, jnp.max(s, axis=1, keepdims=True))
    alpha = jnp.exp(m_prev - m_new)
    p = jnp.exp(s - m_new)
    l_sc[...] = alpha * l_sc[...] + jnp.sum(p, axis=1, keepdims=True)
    acc_sc[...] = alpha * acc_sc[...] + _dot_nn(p, v_ref[...])
    m_sc[...] = m_new

    @pl.when(ki == nk - 1)
    def _():
        o_ref[...] = acc_sc[...] / l_sc[...]
        lse_ref[...] = m_sc[...] + jnp.log(l_sc[...])


def _mla_fwd_call(q, k, v):
    h, seq, dq = q.shape
    dv = v.shape[-1]
    tq, tk = _pick(seq, 512, LANES), _pick(seq, 1024, LANES)
    nk = seq // tk
    scale = dq**-0.5
    return pl.pallas_call(
        functools.partial(_mla_fwd_body, scale=scale, nk=nk),
        name="mla_fwd",
        grid=(h, seq // tq, nk),
        in_specs=[
            pl.BlockSpec((None, tq, dq), lambda hh, i, j: (hh, i, 0)),
            pl.BlockSpec((None, tk, dq), lambda hh, i, j: (hh, j, 0)),
            pl.BlockSpec((None, tk, dv), lambda hh, i, j: (hh, j, 0)),
        ],
        out_specs=[
            pl.BlockSpec((None, tq, dv), lambda hh, i, j: (hh, i, 0)),
            pl.BlockSpec((None, tq, 1), lambda hh, i, j: (hh, i, 0)),
        ],
        out_shape=[jax.ShapeDtypeStruct((h, seq, dv), F32), jax.ShapeDtypeStruct((h, seq, 1), F32)],
        scratch_shapes=[pltpu.VMEM((tq, 1), F32), pltpu.VMEM((tq, 1), F32), pltpu.VMEM((tq, dv), F32)],
        compiler_params=_cparams("parallel", "parallel", "arbitrary"),
    )(q, k, v)


def _mla_bwd_body(q_ref, k_ref, v_ref, do_ref, lse_ref, dl_ref, dq_ref, dk_ref, dv_ref, dk_sc, dv_sc, *, scale, nq, tq):
    ki = pl.program_id(1)
    qi = pl.program_id(2)

    @pl.when(jnp.logical_and(ki == 0, qi == 0))
    def _():
        dq_ref[...] = jnp.zeros_like(dq_ref)

    @pl.when(qi == 0)
    def _():
        dk_sc[...] = jnp.zeros_like(dk_sc)
        dv_sc[...] = jnp.zeros_like(dv_sc)

    q = q_ref[...]
    k = k_ref[...]
    do = do_ref[...]
    p = jnp.exp(_dot_nt(q, k) * scale - lse_ref[...])
    dv_sc[...] += _dot_tn(p, do)
    ds = p * (_dot_nt(do, v_ref[...]) - dl_ref[...]) * scale
    dk_sc[...] += _dot_tn(ds, q)
    rows = pl.ds(pl.multiple_of(qi * tq, tq), tq)
    dq_ref[rows, :] += _dot_nn(ds, k)

    @pl.when(qi == nq - 1)
    def _():
        dk_ref[...] = dk_sc[...]
        dv_ref[...] = dv_sc[...]


def _mla_bwd_call(q, k, v, do, lse, delta):
    h, seq, dq = q.shape
    dv = v.shape[-1]
    tq = tk = _pick(seq, 512, LANES)
    nq = seq // tq
    scale = dq**-0.5
    return pl.pallas_call(
        functools.partial(_mla_bwd_body, scale=scale, nq=nq, tq=tq),
        name="mla_bwd",
        grid=(h, seq // tk, nq),
        in_specs=[
            pl.BlockSpec((None, tq, dq), lambda hh, j, i: (hh, i, 0)),
            pl.BlockSpec((None, tk, dq), lambda hh, j, i: (hh, j, 0)),
            pl.BlockSpec((None, tk, dv), lambda hh, j, i: (hh, j, 0)),
            pl.BlockSpec((None, tq, dv), lambda hh, j, i: (hh, i, 0)),
            pl.BlockSpec((None, tq, 1), lambda hh, j, i: (hh, i, 0)),
            pl.BlockSpec((None, tq, 1), lambda hh, j, i: (hh, i, 0)),
        ],
        out_specs=[
            pl.BlockSpec((None, seq, dq), lambda hh, j, i: (hh, 0, 0)),
            pl.BlockSpec((None, tk, dq), lambda hh, j, i: (hh, j, 0)),
            pl.BlockSpec((None, tk, dv), lambda hh, j, i: (hh, j, 0)),
        ],
        out_shape=[
            jax.ShapeDtypeStruct((h, seq, dq), F32),
            jax.ShapeDtypeStruct((h, seq, dq), F32),
            jax.ShapeDtypeStruct((h, seq, dv), F32),
        ],
        scratch_shapes=[pltpu.VMEM((tk, dq), F32), pltpu.VMEM((tk, dv), F32)],
        compiler_params=_cparams("parallel", "arbitrary", "arbitrary"),
    )(q, k, v, do, lse, delta)


@jax.custom_vjp
def mla_attention(q, k, v):
    return _mla_fwd_call(q, k, v)[0]


def _mla_attention_fwd(q, k, v):
    o, lse = _mla_fwd_call(q, k, v)
    return o, (q, k, v, o, lse)


def _mla_attention_bwd(res, do):
    q, k, v, o, lse = res
    delta = jnp.sum(do * o, axis=-1, keepdims=True)
    return tuple(_mla_bwd_call(q, k, v, do, lse, delta))


mla_attention.defvjp(_mla_attention_fwd, _mla_attention_bwd)


def _swa_block(q4, kp, kc, kn, vp, vc, vn, bias, sink, *, valid):
    kb = jnp.concatenate([kp, kc, kn], axis=0)
    vb = jnp.concatenate([vp, vc, vn], axis=0)
    s = _dot_nt(q4, kb) * (SWA_HEAD_DIM**-0.5) + bias
    s = jnp.where(valid, s, NEG_BIG)
    m = lax.stop_gradient(jnp.maximum(jnp.max(s, axis=1, keepdims=True), sink))
    p = jnp.exp(s - m)
    denom = jnp.sum(p, axis=1, keepdims=True) + jnp.exp(sink - m)
    return _dot_nn(p / denom, vb)


def _swa_valid(n, seq):
    rows = SWA_GROUP * SWA_BLOCK
    qi = lax.broadcasted_iota(jnp.int32, (rows, 3 * SWA_BLOCK), 0) % SWA_BLOCK
    sj = lax.broadcasted_iota(jnp.int32, (rows, 3 * SWA_BLOCK), 1)
    rel = sj - SWA_BLOCK - qi
    kpos = n * SWA_BLOCK + sj - SWA_BLOCK
    return (jnp.abs(rel) <= WINDOW) & (kpos >= 0) & (kpos < seq)


def _swa_operands(q_ref, kp_ref, kc_ref, kn_ref, vp_ref, vc_ref, vn_ref, b_ref, s_ref):
    q4 = q_ref[...].reshape(SWA_GROUP * SWA_BLOCK, SWA_HEAD_DIM)
    return (q4, kp_ref[...], kc_ref[...], kn_ref[...], vp_ref[...], vc_ref[...], vn_ref[...], b_ref[...], s_ref[...])


def _swa_fwd_body(q_ref, kp_ref, kc_ref, kn_ref, vp_ref, vc_ref, vn_ref, b_ref, s_ref, o_ref, *, seq):
    valid = _swa_valid(pl.program_id(1), seq)
    out = _swa_block(*_swa_operands(q_ref, kp_ref, kc_ref, kn_ref, vp_ref, vc_ref, vn_ref, b_ref, s_ref), valid=valid)
    o_ref[...] = out.reshape(SWA_GROUP, SWA_BLOCK, SWA_HEAD_DIM)


def _swa_bwd_body(q_ref, kp_ref, kc_ref, kn_ref, vp_ref, vc_ref, vn_ref, b_ref, s_ref, do_ref,
                  dq_ref, dk_ref, dv_ref, db_ref, ds_ref, *, seq):
    n = pl.program_id(1)

    @pl.when(n == 0)
    def _():
        db_ref[...] = jnp.zeros_like(db_ref)
        ds_ref[...] = jnp.zeros_like(ds_ref)

    valid = _swa_valid(n, seq)
    ops = _swa_operands(q_ref, kp_ref, kc_ref, kn_ref, vp_ref, vc_ref, vn_ref, b_ref, s_ref)
    _, vjp = jax.vjp(functools.partial(_swa_block, valid=valid), *ops)
    do = do_ref[...].reshape(SWA_GROUP * SWA_BLOCK, SWA_HEAD_DIM)
    dq4, dkp, dkc, dkn, dvp, dvc, dvn, dbias, dsink = vjp(do)
    dq_ref[...] = dq4.reshape(SWA_GROUP, SWA_BLOCK, SWA_HEAD_DIM)
    dk_ref[0] = dkp
    dk_ref[1] = dkc
    dk_ref[2] = dkn
    dv_ref[0] = dvp
    dv_ref[1] = dvc
    dv_ref[2] = dvn
    db_ref[...] += dbias
    ds_ref[...] += dsink


def _swa_in_specs(nb):
    blk = (None, SWA_BLOCK, SWA_HEAD_DIM)
    prev = lambda h, n: (h, jnp.maximum(n - 1, 0), 0)
    own = lambda h, n: (h, n, 0)
    nxt = lambda h, n: (h, jnp.minimum(n + 1, nb - 1), 0)
    rows = SWA_GROUP * SWA_BLOCK
    return [
        pl.BlockSpec((None, SWA_GROUP, SWA_BLOCK, SWA_HEAD_DIM), lambda h, n: (h, 0, n, 0)),
        pl.BlockSpec(blk, prev), pl.BlockSpec(blk, own), pl.BlockSpec(blk, nxt),
        pl.BlockSpec(blk, prev), pl.BlockSpec(blk, own), pl.BlockSpec(blk, nxt),
        pl.BlockSpec((None, rows, 3 * SWA_BLOCK), lambda h, n: (h, 0, 0)),
        pl.BlockSpec((None, rows, 1), lambda h, n: (h, 0, 0)),
    ]


def _swa_fwd_call(q, k, v, bias, sink):
    kv, g, seq, d = q.shape
    nb = seq // SWA_BLOCK
    return pl.pallas_call(
        functools.partial(_swa_fwd_body, seq=seq),
        name="swa_fwd",
        grid=(kv, nb),
        in_specs=_swa_in_specs(nb),
        out_specs=pl.BlockSpec((None, g, SWA_BLOCK, d), lambda h, n: (h, 0, n, 0)),
        out_shape=jax.ShapeDtypeStruct(q.shape, F32),
        compiler_params=_cparams("parallel", "parallel"),
    )(q, k, k, k, v, v, v, bias, sink)


def _swa_bwd_call(q, k, v, bias, sink, do):
    kv, g, seq, d = q.shape
    nb = seq // SWA_BLOCK
    rows = g * SWA_BLOCK
    part = jax.ShapeDtypeStruct((kv, nb, 3, SWA_BLOCK, d), F32)
    part_spec = pl.BlockSpec((None, None, 3, SWA_BLOCK, d), lambda h, n: (h, n, 0, 0, 0))
    dq, dkp, dvp, dbias, dsink = pl.pallas_call(
        functools.partial(_swa_bwd_body, seq=seq),
        name="swa_bwd",
        grid=(kv, nb),
        in_specs=_swa_in_specs(nb) + [pl.BlockSpec((None, g, SWA_BLOCK, d), lambda h, n: (h, 0, n, 0))],
        out_specs=[
            pl.BlockSpec((None, g, SWA_BLOCK, d), lambda h, n: (h, 0, n, 0)),
            part_spec, part_spec,
            pl.BlockSpec((None, rows, 3 * SWA_BLOCK), lambda h, n: (h, 0, 0)),
            pl.BlockSpec((None, rows, 1), lambda h, n: (h, 0, 0)),
        ],
        out_shape=[jax.ShapeDtypeStruct(q.shape, F32), part, part,
                   jax.ShapeDtypeStruct(bias.shape, F32), jax.ShapeDtypeStruct(sink.shape, F32)],
        compiler_params=_cparams("parallel", "arbitrary"),
    )(q, k, k, k, v, v, v, bias, sink, do)

    def fold(p):
        zero = jnp.zeros_like(p[:, :1, 0])
        total = p[:, :, 1] + jnp.concatenate([p[:, 1:, 0], zero], axis=1) + jnp.concatenate([zero, p[:, :-1, 2]], axis=1)
        return total.reshape(kv, seq, d)

    return dq, fold(dkp), fold(dvp), dbias, dsink


@jax.custom_vjp
def swa_attention(q, k, v, bias, sink):
    return _swa_fwd_call(q, k, v, bias, sink)


def _swa_attention_fwd(q, k, v, bias, sink):
    return _swa_fwd_call(q, k, v, bias, sink), (q, k, v, bias, sink)


def _swa_attention_bwd(res, do):
    return _swa_bwd_call(*res, do)


swa_attention.defvjp(_swa_attention_fwd, _swa_attention_bwd)


def _scan_tiles(n_tiles, reverse, tile_fn, init):
    def step(i, carry):
        ti = (n_tiles - 1 - i) if reverse else i
        return tile_fn(pl.multiple_of(ti * SUBLANES, SUBLANES), carry)

    return lax.fori_loop(0, n_tiles, step, init)


def _row_order(reverse):
    return tuple(reversed(range(SUBLANES))) if reverse else tuple(range(SUBLANES))


def _s5_fwd_dir(a_ref, bu_ref, s_ref, carry, *, reverse, tt, w):
    ar, ai = a_ref[0:1, :], a_ref[1:2, :]
    rowid = lax.broadcasted_iota(jnp.int32, (SUBLANES, w), 0)

    def tile(base, c):
        sr, si = c
        x = bu_ref[pl.ds(base, SUBLANES), :]
        xr, xi = x[:, :w], x[:, w:]
        out_r = jnp.zeros((SUBLANES, w), F32)
        out_i = jnp.zeros((SUBLANES, w), F32)
        for j in _row_order(reverse):
            nr = ar * sr - ai * si + xr[j:j + 1, :]
            ni = ar * si + ai * sr + xi[j:j + 1, :]
            out_r = jnp.where(rowid == j, nr, out_r)
            out_i = jnp.where(rowid == j, ni, out_i)
            sr, si = nr, ni
        s_ref[pl.ds(base, SUBLANES), :] = jnp.concatenate([out_r, out_i], axis=1)
        return sr, si

    sr, si = _scan_tiles(tt // SUBLANES, reverse, tile, (carry[0:1, :], carry[1:2, :]))
    carry[0:1, :] = sr
    carry[1:2, :] = si


def _s5_fwd_body(a_ref, bu_ref, s_ref, carry, *, tt, w):
    d = pl.program_id(0)

    @pl.when(pl.program_id(1) == 0)
    def _():
        carry[...] = jnp.zeros_like(carry)

    @pl.when(d == 0)
    def _():
        _s5_fwd_dir(a_ref, bu_ref, s_ref, carry, reverse=False, tt=tt, w=w)

    @pl.when(d == 1)
    def _():
        _s5_fwd_dir(a_ref, bu_ref, s_ref, carry, reverse=True, tt=tt, w=w)


def _s5_bwd_dir(a_ref, s_ref, ds_ref, dbu_ref, da_ref, carry, *, reverse, tt, w):
    ar, ai = a_ref[0:1, :], a_ref[1:2, :]
    rowid = lax.broadcasted_iota(jnp.int32, (SUBLANES, w), 0)

    def tile(base, c):
        lr, li, dar, dai = c
        s = s_ref[pl.ds(base, SUBLANES), :]
        g = ds_ref[pl.ds(base, SUBLANES), :]
        out_r = jnp.zeros((SUBLANES, w), F32)
        out_i = jnp.zeros((SUBLANES, w), F32)
        for j in _row_order(reverse):
            sr, si = s[j:j + 1, :w], s[j:j + 1, w:]
            dar = dar + sr * lr + si * li
            dai = dai + sr * li - si * lr
            nr = g[j:j + 1, :w] + ar * lr + ai * li
            ni = g[j:j + 1, w:] + ar * li - ai * lr
            out_r = jnp.where(rowid == j, nr, out_r)
            out_i = jnp.where(rowid == j, ni, out_i)
            lr, li = nr, ni
        dbu_ref[pl.ds(base, SUBLANES), :] = jnp.concatenate([out_r, out_i], axis=1)
        return lr, li, dar, dai

    zero = jnp.zeros((1, w), F32)
    lr, li, dar, dai = _scan_tiles(tt // SUBLANES, reverse, tile, (carry[0:1, :], carry[1:2, :], zero, zero))
    carry[0:1, :] = lr
    carry[1:2, :] = li
    da_ref[0:1, :] += dar
    da_ref[1:2, :] += dai


def _s5_bwd_body(a_ref, s_ref, ds_ref, dbu_ref, da_ref, carry, *, tt, w):
    d = pl.program_id(0)

    @pl.when(pl.program_id(1) == 0)
    def _():
        carry[...] = jnp.zeros_like(carry)
        da_ref[...] = jnp.zeros_like(da_ref)

    @pl.when(d == 0)
    def _():
        _s5_bwd_dir(a_ref, s_ref, ds_ref, dbu_ref, da_ref, carry, reverse=True, tt=tt, w=w)

    @pl.when(d == 1)
    def _():
        _s5_bwd_dir(a_ref, s_ref, ds_ref, dbu_ref, da_ref, carry, reverse=False, tt=tt, w=w)


def _s5_time_map(nt, flip_dir):
    def time_block(d, t):
        back = nt - 1 - t
        return jnp.where(d == flip_dir, back, t)

    return time_block


def _s5_fwd_call(a, bu):
    seq, w4 = bu.shape
    w = w4 // 4
    tt = _pick(seq, 256, SUBLANES)
    nt = seq // tt
    tb = _s5_time_map(nt, 1)
    return pl.pallas_call(
        functools.partial(_s5_fwd_body, tt=tt, w=w),
        name="s5_scan_fwd",
        grid=(2, nt),
        in_specs=[
            pl.BlockSpec((None, 2, w), lambda d, t: (d, 0, 0)),
            pl.BlockSpec((tt, 2 * w), lambda d, t: (tb(d, t), d)),
        ],
        out_specs=pl.BlockSpec((tt, 2 * w), lambda d, t: (tb(d, t), d)),
        out_shape=jax.ShapeDtypeStruct(bu.shape, F32),
        scratch_shapes=[pltpu.VMEM((2, w), F32)],
        compiler_params=_cparams("parallel", "arbitrary"),
    )(a, bu)


def _s5_bwd_call(a, s_prev, ds):
    seq, w4 = ds.shape
    w = w4 // 4
    tt = _pick(seq, 256, SUBLANES)
    nt = seq // tt
    tb = _s5_time_map(nt, 0)
    blk = pl.BlockSpec((tt, 2 * w), lambda d, t: (tb(d, t), d))
    return pl.pallas_call(
        functools.partial(_s5_bwd_body, tt=tt, w=w),
        name="s5_scan_bwd",
        grid=(2, nt),
        in_specs=[pl.BlockSpec((None, 2, w), lambda d, t: (d, 0, 0)), blk, blk],
        out_specs=[blk, pl.BlockSpec((None, 2, w), lambda d, t: (d, 0, 0))],
        out_shape=[jax.ShapeDtypeStruct(ds.shape, F32), jax.ShapeDtypeStruct(a.shape, F32)],
        scratch_shapes=[pltpu.VMEM((2, w), F32)],
        compiler_params=_cparams("parallel", "arbitrary"),
    )(a, s_prev, ds)


@jax.custom_vjp
def s5_scan(a, bu):
    return _s5_fwd_call(a, bu)


def _s5_scan_fwd(a, bu):
    s = _s5_fwd_call(a, bu)
    return s, (a, s)


def _s5_scan_bwd(res, ds):
    a, s = res
    dbu, da = _s5_bwd_call(a, s, ds)
    return da, dbu


s5_scan.defvjp(_s5_scan_fwd, _s5_scan_bwd)


def _dot3(a, b):
    ah = a.astype(BF16)
    bh = b.astype(BF16)
    al = a - ah.astype(F32)
    bl = b - bh.astype(F32)
    return _dot_nn(ah, bh) + _dot_nn(ah, bl) + _dot_nn(al, bh)


GDN_INV_BASE = 8


def _unit_lower_inverse(a, ri, ci):
    c = a.shape[0]

    def same_block(size):
        shift = int(math.log2(size))
        return lax.shift_right_logical(ri, shift) == lax.shift_right_logical(ci, shift)

    diag = jnp.where(same_block(GDN_INV_BASE), a, 0.0)
    inv = jnp.where(ri == ci, 1.0, 0.0) - diag
    power = diag
    for _ in range(int(math.log2(GDN_INV_BASE)) - 1):
        power = _dot3(power, power)
        inv = inv + _dot3(inv, power)
    size = GDN_INV_BASE
    while size < c:
        off = jnp.where(jnp.logical_and(same_block(2 * size), jnp.logical_not(same_block(size))), a, 0.0)
        inv = inv - _dot3(_dot3(inv, off), inv)
        size *= 2
    return inv


def _gdn_chunk(q, k, v, gc, gr, bc, state, *, reverse):
    c = q.shape[0]
    dv = v.shape[1]
    ri = lax.broadcasted_iota(jnp.int32, (c, c), 0)
    ci = lax.broadcasted_iota(jnp.int32, (c, c), 1)
    lower = (ri <= ci) if reverse else (ri >= ci)
    strict = (ri < ci) if reverse else (ri > ci)
    kb = k * bc
    decay = jnp.where(lower, jnp.exp(jnp.where(lower, gc - gr, 0.0)), 0.0)
    a = jnp.where(strict, _dot_nt(kb, k) * decay, 0.0)
    inv = _unit_lower_inverse(a, ri, ci)
    eg = jnp.exp(gc)
    sol = _dot3(inv, jnp.concatenate([v * bc, kb * eg], axis=1))
    u, w = sol[:, :dv], sol[:, dv:]
    attn = _dot_nt(q, k) * decay
    v_new = u - _dot_nn(w, state)
    o = _dot_nn(q * eg, state) + _dot_nn(attn, v_new)
    last = lax.broadcasted_iota(jnp.int32, (c, 1), 0) == (0 if reverse else c - 1)
    g_last = jnp.sum(jnp.where(last, gc, 0.0), axis=0, keepdims=True)
    new_state = state * jnp.exp(g_last) + _dot_tn(k * jnp.exp(g_last - gc), v_new)
    return o, new_state


def _gdn_fwd_body(qf, kf, vf, qb, kb, vb, gcf, grf, bcf, gcb, grb, bcb, of, ob, s0f, s0b, state, *, heads, dh):
    @pl.when(pl.program_id(0) == 0)
    def _():
        state[...] = jnp.zeros_like(state)

    directions = ((qf, kf, vf, gcf, grf, bcf, of, s0f), (qb, kb, vb, gcb, grb, bcb, ob, s0b))
    for d, (q_ref, k_ref, v_ref, gc_ref, gr_ref, bc_ref, o_ref, s0_ref) in enumerate(directions):
        for h in range(heads):
            cols = slice(h * dh, (h + 1) * dh)
            s0 = state[d, h]
            s0_ref[h] = s0
            o, new_state = _gdn_chunk(q_ref[:, cols], k_ref[:, cols], v_ref[:, cols], gc_ref[h], gr_ref[h], bc_ref[h],
                                      s0, reverse=d == 1)
            o_ref[:, cols] = o
            state[d, h] = new_state


def _gdn_bwd_body(qf, kf, vf, qb, kb, vb, gcf, grf, bcf, gcb, grb, bcb, s0f, s0b, dof, dob,
                  dqf, dkf, dvf, dqb, dkb, dvb, dgcf, dgrf, dbcf, dgcb, dgrb, dbcb, dstate, *, heads, dh):
    @pl.when(pl.program_id(0) == 0)
    def _():
        dstate[...] = jnp.zeros_like(dstate)

    directions = ((qf, kf, vf, gcf, grf, bcf, s0f, dof, dqf, dkf, dvf, dgcf, dgrf, dbcf),
                  (qb, kb, vb, gcb, grb, bcb, s0b, dob, dqb, dkb, dvb, dgcb, dgrb, dbcb))
    for d, refs in enumerate(directions):
        q_ref, k_ref, v_ref, gc_ref, gr_ref, bc_ref, s0_ref, do_ref, dq_ref, dk_ref, dv_ref, dgc_ref, dgr_ref, dbc_ref = refs
        for h in range(heads):
            cols = slice(h * dh, (h + 1) * dh)
            ops = (q_ref[:, cols], k_ref[:, cols], v_ref[:, cols], gc_ref[h], gr_ref[h], bc_ref[h], s0_ref[h])
            _, vjp = jax.vjp(functools.partial(_gdn_chunk, reverse=d == 1), *ops)
            dq, dk, dv, dgc, dgr, dbc, ds0 = vjp((do_ref[:, cols], dstate[d, h]))
            dq_ref[:, cols] = dq
            dk_ref[:, cols] = dk
            dv_ref[:, cols] = dv
            dgc_ref[h] = dgc
            dgr_ref[h] = dgr
            dbc_ref[h] = dbc
            dstate[d, h] = ds0


def _gdn_specs(nc, heads, dh, backward):
    c = GDN_CHUNK
    up = lambda n: n
    down = lambda n: nc - 1 - n
    out = []
    for chunk in ((down, up) if backward else (up, down)):
        out.append((
            pl.BlockSpec((c, heads * dh), lambda n, chunk=chunk: (chunk(n), 0)),
            pl.BlockSpec((heads, c, 1), lambda n, chunk=chunk: (0, chunk(n), 0)),
            pl.BlockSpec((heads, None, 1, c), lambda n, chunk=chunk: (0, chunk(n), 0, 0)),
            pl.BlockSpec((heads, None, dh, dh), lambda n, chunk=chunk: (0, chunk(n), 0, 0)),
        ))
    return out


def _gdn_fwd_call(q, k, v, gcf, grf, bcf, gcb, grb, bcb):
    seq, width = q.shape
    heads = gcf.shape[0]
    dh = width // heads
    nc = seq // GDN_CHUNK
    (seq_f, col_f, row_f, st_f), (seq_b, col_b, row_b, st_b) = _gdn_specs(nc, heads, dh, False)
    states = jax.ShapeDtypeStruct((heads, nc, dh, dh), F32)
    return pl.pallas_call(
        functools.partial(_gdn_fwd_body, heads=heads, dh=dh),
        name="gdn_fwd",
        grid=(nc,),
        in_specs=[seq_f] * 3 + [seq_b] * 3 + [col_f, row_f, col_f, col_b, row_b, col_b],
        out_specs=[seq_f, seq_b, st_f, st_b],
        out_shape=[jax.ShapeDtypeStruct(q.shape, F32)] * 2 + [states] * 2,
        scratch_shapes=[pltpu.VMEM((2, heads, dh, dh), F32)],
        compiler_params=_cparams("arbitrary"),
    )(q, k, v, q, k, v, gcf, grf, bcf, gcb, grb, bcb)


def _gdn_bwd_call(q, k, v, gcf, grf, bcf, gcb, grb, bcb, s0f, s0b, dof, dob):
    seq, width = q.shape
    heads = gcf.shape[0]
    dh = width // heads
    nc = seq // GDN_CHUNK
    (seq_f, col_f, row_f, st_f), (seq_b, col_b, row_b, st_b) = _gdn_specs(nc, heads, dh, True)
    like = lambda t: jax.ShapeDtypeStruct(t.shape, F32)
    return pl.pallas_call(
        functools.partial(_gdn_bwd_body, heads=heads, dh=dh),
        name="gdn_bwd",
        grid=(nc,),
        in_specs=[seq_f] * 3 + [seq_b] * 3 + [col_f, row_f, col_f, col_b, row_b, col_b, st_f, st_b, seq_f, seq_b],
        out_specs=[seq_f] * 3 + [seq_b] * 3 + [col_f, row_f, col_f, col_b, row_b, col_b],
        out_shape=[like(q)] * 6 + [like(gcf), like(grf), like(bcf), like(gcb), like(grb), like(bcb)],
        scratch_shapes=[pltpu.VMEM((2, heads, dh, dh), F32)],
        compiler_params=_cparams("arbitrary"),
    )(q, k, v, q, k, v, gcf, grf, bcf, gcb, grb, bcb, s0f, s0b, dof, dob)


@jax.custom_vjp
def gdn_delta_rule(q, k, v, gcf, grf, bcf, gcb, grb, bcb):
    return tuple(_gdn_fwd_call(q, k, v, gcf, grf, bcf, gcb, grb, bcb)[:2])


def _gdn_delta_rule_fwd(*ops):
    of, ob, s0f, s0b = _gdn_fwd_call(*ops)
    return (of, ob), (*ops, s0f, s0b)


def _gdn_delta_rule_bwd(res, do):
    dqf, dkf, dvf, dqb, dkb, dvb, *small = _gdn_bwd_call(*res, *do)
    return (dqf + dqb, dkf + dkb, dvf + dvb, *small)


gdn_delta_rule.defvjp(_gdn_delta_rule_fwd, _gdn_delta_rule_bwd)


def _mesh_position():
    return lax.axis_index("x"), lax.axis_index("y"), lax.axis_index("c")


def _all_gather_body(x_ref, out_ref, send_sems, recv_sems, local_sem):
    x, y, c = _mesh_position()
    me, sibling = (x, y, c), (x, y, 1 - c)
    chips = [(1 - x, y), (x, 1 - y), (1 - x, 1 - y)]

    def slot(px, py, pc):
        return out_ref.at[4 * px + 2 * py + pc]

    def copy(k, block, to, src=None):
        return pltpu.make_async_remote_copy(
            src_ref=slot(*block) if src is None else src, dst_ref=slot(*block),
            send_sem=send_sems.at[k], recv_sem=recv_sems.at[k], device_id=to, device_id_type=pl.DeviceIdType.MESH)

    mine = pltpu.make_async_copy(x_ref, slot(*me), local_sem)
    mine.start()
    first = [copy(0, me, sibling, src=x_ref)]
    first += [copy(1 + j, me, (*chip, c), src=x_ref) for j, chip in enumerate(chips)]
    for cp in first:
        cp.start()
    passed = [copy(4 + j, (*chip, c), sibling) for j, chip in enumerate(chips)]
    for j, chip in enumerate(chips):
        copy(1 + j, (*chip, c), me).wait_recv()
        passed[j].start()
    copy(0, sibling, me).wait_recv()
    for j, chip in enumerate(chips):
        copy(4 + j, (*chip, 1 - c), me).wait_recv()
    for cp in first + passed:
        cp.wait_send()
    mine.wait()


def all_gather(x, name):
    return pl.pallas_call(
        functools.partial(_all_gather_body),
        name=name,
        in_specs=[pl.BlockSpec(memory_space=pl.ANY)],
        out_specs=pl.BlockSpec(memory_space=pl.ANY),
        out_shape=jax.ShapeDtypeStruct((N_DEV,) + x.shape, x.dtype),
        scratch_shapes=[pltpu.SemaphoreType.DMA((N_DEV - 1,)), pltpu.SemaphoreType.DMA((N_DEV - 1,)), pltpu.SemaphoreType.DMA],
    )(x)


def _all_to_all_body(x_ref, out_ref, send_sems, recv_sems, local_sem):
    x, y, c = _mesh_position()
    my = 4 * x + 2 * y + c
    copies = []
    for m in range(1, N_DEV):
        px = 1 - x if m & 4 else x
        py = 1 - y if m & 2 else y
        pc = 1 - c if m & 1 else c
        copies.append(pltpu.make_async_remote_copy(
            src_ref=x_ref.at[4 * px + 2 * py + pc], dst_ref=out_ref.at[my],
            send_sem=send_sems.at[m - 1], recv_sem=recv_sems.at[m - 1],
            device_id=(px, py, pc), device_id_type=pl.DeviceIdType.MESH))
    for cp in copies:
        cp.start()
    mine = pltpu.make_async_copy(x_ref.at[my], out_ref.at[my], local_sem)
    mine.start()
    for cp in copies:
        cp.wait()
    mine.wait()


def all_to_all(x, name):
    return pl.pallas_call(
        functools.partial(_all_to_all_body),
        name=name,
        in_specs=[pl.BlockSpec(memory_space=pl.ANY)],
        out_specs=pl.BlockSpec(memory_space=pl.ANY),
        out_shape=jax.ShapeDtypeStruct(x.shape, x.dtype),
        scratch_shapes=[pltpu.SemaphoreType.DMA((N_DEV - 1,)), pltpu.SemaphoreType.DMA((N_DEV - 1,)), pltpu.SemaphoreType.DMA],
    )(x)


PACK_COLS = 1024
PACK_ROW_MULTIPLE = 512


def _pack(arrays, dtype):
    parts = []
    rows = 0
    for a in arrays:
        flat = a.reshape(-1).astype(dtype)
        n_rows = -(-flat.shape[0] // PACK_COLS)
        parts.append(jnp.pad(flat, (0, n_rows * PACK_COLS - flat.shape[0])).reshape(n_rows, PACK_COLS))
        rows += n_rows
    pad_rows = -rows % PACK_ROW_MULTIPLE
    if pad_rows:
        parts.append(jnp.zeros((pad_rows, PACK_COLS), dtype))
    return jnp.concatenate(parts, axis=0)


def _unpack(flat, shapes):
    lead = flat.shape[:-2]
    out = []
    row = 0
    for shape in shapes:
        size = int(np.prod(shape))
        n_rows = -(-size // PACK_COLS)
        seg = flat[..., row:row + n_rows, :].reshape(lead + (n_rows * PACK_COLS,))[..., :size]
        out.append(seg.reshape(lead + tuple(shape)))
        row += n_rows
    return out


W_IN_SMALL = 4608


def _split_w_in(w_in):
    d = w_in.shape[0]
    small = jnp.concatenate(
        [w_in[:, 0:2560], w_in[:, 2576:4304], w_in[:, 2560:2576], jnp.zeros((d, W_IN_SMALL - 4304), w_in.dtype)], axis=1)
    return small, w_in[:, 4304:]


def _s5_mixer(u, lam_re, lam_im, log_step, b_re, b_im, c_re, c_im, d_skip, w_glu, b_glu):
    g, p, hg = S5_GROUPS, S5_STATE, S5_GROUP
    lam_re = jnp.minimum(lam_re, -1e-4)
    dt = jnp.exp(log_step)[..., None]
    mag = jnp.exp(lam_re * dt)
    abar_r = mag * jnp.cos(lam_im * dt)
    abar_i = mag * jnp.sin(lam_im * dt)
    den = lam_re * lam_re + lam_im * lam_im
    xr = abar_r - 1.0
    xi = abar_i
    coef_r = (xr * lam_re + xi * lam_im) / den
    coef_i = (xi * lam_re - xr * lam_im) / den
    bbar_r = coef_r[..., None] * b_re - coef_i[..., None] * b_im
    bbar_i = coef_r[..., None] * b_im + coef_i[..., None] * b_re
    eye = jnp.eye(g, dtype=F32)

    def block_diag(t, rows, cols):
        return (eye[:, None, :, None] * t[:, :, None, :]).reshape(g * rows, g * cols)

    b_all = jnp.concatenate(
        [block_diag(t.transpose(0, 2, 1), hg, p) for t in (bbar_r[0], bbar_i[0], bbar_r[1], bbar_i[1])], axis=1)
    c_all = jnp.concatenate(
        [block_diag(t.transpose(0, 2, 1), p, hg) for t in (c_re[0], -c_im[0], c_re[1], -c_im[1])], axis=0)
    a_all = jnp.stack([abar_r.reshape(2, g * p), abar_i.reshape(2, g * p)], axis=1)
    s = s5_scan(a_all, mm(u, b_all))
    y = mm(s, c_all) + d_skip * u
    y = jax.nn.gelu(y)
    return y * jax.nn.sigmoid(wmm(y, *w_glu) + b_glu)


def _gdn_mixer(qkv, z, beta_logits, decay_logits, conv_w, a_log, dt_bias, o_gain):
    seq = qkv.shape[0]
    h, dh, c = GDN_HEADS, GDN_HEAD_DIM, GDN_CHUNK
    nc = seq // c
    padded = jnp.pad(qkv, ((GDN_CONV // 2, GDN_CONV - 1 - GDN_CONV // 2), (0, 0)))
    conv = sum(padded[j:j + seq] * conv_w[j] for j in range(GDN_CONV))
    q, k, v = jnp.split(jax.nn.silu(conv), 3, axis=-1)

    def l2(t):
        t = t.reshape(seq, h, dh)
        return (t * lax.rsqrt(jnp.sum(t * t, axis=-1, keepdims=True) + 1e-6)).reshape(seq, h * dh)

    q = l2(q) * (dh**-0.5)
    k = l2(k)
    beta = jax.nn.sigmoid(beta_logits).reshape(seq, 2, h)
    g = -jnp.exp(a_log) * jax.nn.softplus(decay_logits.reshape(seq, 2, h) + dt_bias)
    small = []
    for d in range(2):
        gcs = lax.cumsum(g[:, d].reshape(nc, c, h), axis=1, reverse=d == 1).transpose(2, 0, 1)
        small += [gcs.reshape(h, seq, 1), gcs.reshape(h, nc, 1, c), beta[:, d].T.reshape(h, seq, 1)]
    o_fwd, o_bwd = gdn_delta_rule(q, k, v, *small)
    o = rmsnorm((o_fwd + o_bwd).reshape(seq * h, dh), o_gain).reshape(seq, h * dh)
    return o * jax.nn.silu(z)


def _t5_bucket(rel):
    nb = T5_BUCKETS // 2
    max_exact = nb // 2
    ret = jnp.where(rel > 0, nb, 0)
    n = jnp.abs(rel)
    nf = jnp.maximum(n, 1).astype(F32)
    large = max_exact + (jnp.log(nf / max_exact) / math.log(T5_MAX_DISTANCE / max_exact) * (nb - max_exact)).astype(jnp.int32)
    large = jnp.minimum(large, nb - 1)
    return ret + jnp.where(n < max_exact, n, large)


def _swa_mixer(q, kv, sink, t5_bias):
    seq = q.shape[0]
    kvh, g, d, blk = SWA_KV_HEADS, SWA_GROUP, SWA_HEAD_DIM, SWA_BLOCK
    q4 = q.reshape(seq, kvh, g, d).transpose(1, 2, 0, 3)
    k, v = jnp.split(kv, 2, axis=-1)
    heads_first = lambda t: t.reshape(seq, kvh, d).transpose(1, 0, 2)
    rel = jnp.arange(3 * blk)[None, :] - blk - jnp.arange(blk)[:, None]
    onehot = (_t5_bucket(rel)[..., None] == jnp.arange(T5_BUCKETS)).astype(F32)
    bias = jnp.einsum("qsb,bh->hqs", onehot, t5_bias, precision=lax.Precision.HIGHEST).reshape(kvh, g * blk, 3 * blk)
    sink_col = jnp.broadcast_to(sink.reshape(kvh, g, 1, 1), (kvh, g, blk, 1)).reshape(kvh, g * blk, 1)
    o = swa_attention(q4, heads_first(k), heads_first(v), bias, sink_col)
    return o.transpose(2, 0, 1, 3).reshape(seq, kvh * g * d)


def _rope(t, cos, sin):
    t1, t2 = jnp.split(t, 2, axis=-1)
    return jnp.concatenate([t1 * cos - t2 * sin, t2 * cos + t1 * sin], axis=-1)


def _mla_mixer(c_q, c_kv, k_rope, q_gain, kv_gain, w_uq, w_ukv):
    seq = c_q.shape[0]
    h = MLA_HEADS
    q = wmm(rmsnorm(c_q, q_gain), *w_uq).reshape(seq, h, MLA_NOPE + MLA_ROPE)
    kv = wmm(rmsnorm(c_kv, kv_gain), *w_ukv).reshape(seq, h, MLA_NOPE + MLA_V)
    q_nope, q_pe = q[..., :MLA_NOPE], q[..., MLA_NOPE:]
    k_nope, v = kv[..., :MLA_NOPE], kv[..., MLA_NOPE:]
    pos = jnp.arange(seq, dtype=F32)
    inv_freq = ROPE_THETA ** (-jnp.arange(0, MLA_ROPE, 2, dtype=F32) / MLA_ROPE)
    ang = pos[:, None] * inv_freq[None, :]
    cos, sin = jnp.cos(ang)[:, None, :], jnp.sin(ang)[:, None, :]
    q_pe = _rope(q_pe, cos, sin)
    k_pe = _rope(k_rope[:, None, :], cos, sin)
    qf = jnp.concatenate([q_nope, q_pe], axis=-1)
    kf = jnp.concatenate([k_nope, jnp.broadcast_to(k_pe, (seq, h, MLA_ROPE))], axis=-1)
    o = mla_attention(qf.transpose(1, 0, 2), kf.transpose(1, 0, 2), v.transpose(1, 0, 2))
    return o.transpose(1, 0, 2).reshape(seq, h * MLA_V)


def _layer(x, p, t5_bias):
    d = x.shape[1]
    h = rmsnorm(x, p["mix_pre_gain"])
    (w_small, w_gate), (w_small_bf16, w_gate_bf16) = (_split_w_in(t) for t in p["w_in"])
    ps = wmm(h, w_small, w_small_bf16)
    gate_logits = wmm(h, w_gate, w_gate_bf16)
    y_a = _s5_mixer(ps[:, 0:512], p["s5_lam_re"], p["s5_lam_im"], p["s5_log_step"], p["s5_b_re"], p["s5_b_im"],
                    p["s5_c_re"], p["s5_c_im"], p["s5_d"], p["s5_w_glu"], p["s5_b_glu"])
    y_b = _gdn_mixer(ps[:, 512:2048], ps[:, 2048:2560], ps[:, 4288:4296], ps[:, 4296:4304], p["gdn_conv"],
                     p["gdn_a_log"], p["gdn_dt_bias"], p["gdn_o_gain"])
    y_c = _swa_mixer(ps[:, 2560:3072], ps[:, 3072:3328], p["swa_sink"], t5_bias)
    y_d = _mla_mixer(ps[:, 3328:3712], ps[:, 3712:4224], ps[:, 4224:4288], p["mla_q_gain"], p["mla_kv_gain"],
                     p["mla_w_uq"], p["mla_w_ukv"])
    merged = sum(jax.nn.sigmoid(gate_logits[:, b * d:(b + 1) * d]) * wmm(y, p["w_branch"][0][b], p["w_branch"][1][b])
                 for b, y in enumerate((y_a, y_b, y_c, y_d)))
    x = x + rmsnorm(wmm(merged, *p["w_out"]), p["mix_post_gain"])
    h = rmsnorm(x, p["mlp_pre_gain"])
    f = wmm(jnp.square(jax.nn.relu(wmm(h, *p["w_mlp_in"]))), *p["w_mlp_out"])
    return x + rmsnorm(f, p["mlp_post_gain"])


LAYER_WEIGHTS = ("w_in", "s5_lam_re", "s5_lam_im", "s5_log_step", "s5_b_re", "s5_b_im", "s5_c_re", "s5_c_im", "s5_d",
                 "s5_w_glu", "s5_b_glu", "gdn_conv", "gdn_a_log", "gdn_dt_bias", "gdn_o_gain", "swa_sink", "mla_q_gain",
                 "mla_kv_gain", "mla_w_uq", "mla_w_ukv", "w_branch", "w_out", "mix_pre_gain", "mix_post_gain",
                 "mlp_pre_gain", "mlp_post_gain", "w_mlp_in", "w_mlp_out")


def _forward(x, weights, values):
    for layer in range(DEPTH):
        p = {n: (weights[n][layer], values[n][layer]) if n in values else weights[n][layer] for n in LAYER_WEIGHTS}
        x = _layer(x, p, weights["t5_bias"])
    return x


WEIGHT_NAMES = ("w_in", "s5_lam_re", "s5_lam_im", "s5_log_step", "s5_b_re", "s5_b_im", "s5_c_re", "s5_c_im", "s5_d",
                "s5_w_glu", "s5_b_glu", "gdn_conv", "gdn_a_log", "gdn_dt_bias", "gdn_o_gain", "swa_sink", "t5_bias",
                "mla_q_gain", "mla_kv_gain", "mla_w_uq", "mla_w_ukv", "w_branch", "w_out", "mix_pre_gain", "mix_post_gain",
                "mlp_pre_gain", "mlp_post_gain", "w_mlp_in", "w_mlp_out")
SHARD_AXIS = {"w_in": 2, "s5_w_glu": 1, "mla_w_uq": 2, "mla_w_ukv": 2, "w_branch": 3, "w_out": 1, "w_mlp_in": 2,
              "w_mlp_out": 1}
CONV = "gdn_conv"
CONV_AXIS = 2
REPLICATED = tuple(n for n in WEIGHT_NAMES if n not in SHARD_AXIS and n != CONV)


def _step(x, target, w, m, v):
    big = tuple(SHARD_AXIS)
    x_pos, y_pos, c_pos = _mesh_position()
    my = 4 * x_pos + 2 * y_pos + c_pos

    shard_shapes = [w[n].shape for n in big]
    gathered = _unpack(all_gather(_pack([w[n] for n in big], BF16), "gather_weights"), shard_shapes)
    values = {n: jnp.concatenate([g[k] for k in range(N_DEV)], axis=SHARD_AXIS[n]) for n, g in zip(big, gathered)}
    full = {n: t.astype(F32) for n, t in values.items()}
    conv_all = _unpack(all_gather(_pack([w[CONV]], F32), "gather_conv"), [w[CONV].shape])[0]
    full[CONV] = jnp.concatenate([conv_all[k] for k in range(N_DEV)], axis=CONV_AXIS)
    for n in REPLICATED:
        full[n] = w[n]

    y, vjp = jax.vjp(lambda x_, full_: _forward(x_, full_, values), x, full)
    loss_rows, dy = loss_head(y, target)
    grad_x, grad_full = vjp(dy)
    loss_part = jnp.sum(loss_rows)

    def pieces(n, k):
        size = w[n].shape[SHARD_AXIS[n]]
        return lax.slice_in_dim(grad_full[n], k * size, (k + 1) * size, axis=SHARD_AXIS[n])

    send = jnp.stack([_pack([pieces(n, k) for n in big], BF16) for k in range(N_DEV)])
    grad_big = sum_leading(all_to_all(send, "exchange_grads"))
    small_names = REPLICATED + (CONV,)
    small = [grad_full[n] for n in small_names] + [loss_part.reshape(1)]
    small_sum = sum_leading(all_gather(_pack(small, F32), "gather_small_grads"))
    small_grads = _unpack(small_sum, [a.shape for a in small])
    loss = small_grads[-1][0]
    grads = dict(zip(small_names, small_grads[:-1]))
    conv_size = w[CONV].shape[CONV_AXIS]
    grads[CONV] = lax.dynamic_slice_in_dim(grads[CONV], my * conv_size, conv_size, axis=CONV_AXIS)
    grads.update(zip(big, _unpack(grad_big, shard_shapes)))

    big_rows = grad_big.shape[0]
    small_shapes = [w[n].shape for n in small_names]
    flat = lambda t: jnp.concatenate([_pack([t[n] for n in big], F32), _pack([t[n] for n in small_names], F32)], axis=0)
    g_flat = jnp.concatenate([grad_big, _pack([grads[n] for n in small_names], F32)], axis=0)

    def named(t):
        return dict(zip(big + small_names, _unpack(t[:big_rows], shard_shapes) + _unpack(t[big_rows:], small_shapes)))

    delta, new_m, new_v = (named(t) for t in adamw_flat(flat(w), g_flat, flat(m), flat(v)))
    return loss, grad_x, grads, delta, new_m, new_v


def kernel(x, w_in, s5_lam_re, s5_lam_im, s5_log_step, s5_b_re, s5_b_im, s5_c_re, s5_c_im, s5_d, s5_w_glu, s5_b_glu, gdn_conv, gdn_a_log, gdn_dt_bias, gdn_o_gain, swa_sink, t5_bias, mla_q_gain, mla_kv_gain, mla_w_uq, mla_w_ukv, w_branch, w_out, mix_pre_gain, mix_post_gain, mlp_pre_gain, mlp_post_gain, w_mlp_in, w_mlp_out, loss_target, m_w_in, m_s5_lam_re, m_s5_lam_im, m_s5_log_step, m_s5_b_re, m_s5_b_im, m_s5_c_re, m_s5_c_im, m_s5_d, m_s5_w_glu, m_s5_b_glu, m_gdn_conv, m_gdn_a_log, m_gdn_dt_bias, m_gdn_o_gain, m_swa_sink, m_t5_bias, m_mla_q_gain, m_mla_kv_gain, m_mla_w_uq, m_mla_w_ukv, m_w_branch, m_w_out, m_mix_pre_gain, m_mix_post_gain, m_mlp_pre_gain, m_mlp_post_gain, m_w_mlp_in, m_w_mlp_out, v_w_in, v_s5_lam_re, v_s5_lam_im, v_s5_log_step, v_s5_b_re, v_s5_b_im, v_s5_c_re, v_s5_c_im, v_s5_d, v_s5_w_glu, v_s5_b_glu, v_gdn_conv, v_gdn_a_log, v_gdn_dt_bias, v_gdn_o_gain, v_swa_sink, v_t5_bias, v_mla_q_gain, v_mla_kv_gain, v_mla_w_uq, v_mla_w_ukv, v_w_branch, v_w_out, v_mix_pre_gain, v_mix_post_gain, v_mlp_pre_gain, v_mlp_post_gain, v_w_mlp_in, v_w_mlp_out):
    args = locals()
    w = {n: args[n] for n in WEIGHT_NAMES}
    m = {n: args["m_" + n] for n in WEIGHT_NAMES}
    v = {n: args["v_" + n] for n in WEIGHT_NAMES}
    loss, grad_x, grads, delta, new_m, new_v = _step(x[0], loss_target[0], w, m, v)
    return (loss, grad_x[None], *[grads[n] for n in WEIGHT_NAMES], *[delta[n] for n in WEIGHT_NAMES],
            *[new_m[n] for n in WEIGHT_NAMES], *[new_v[n] for n in WEIGHT_NAMES])
```

```python
import functools
import math

import jax
import jax.numpy as jnp
import numpy as np
from jax import lax
from jax.experimental import pallas as pl
from jax.experimental.pallas import tpu as pltpu

F32 = jnp.float32
BF16 = jnp.bfloat16

VMEM_LIMIT_BYTES = 56 * 1024 * 1024
LANES = 128
SUBLANES = 8

N_DEV = 8
MESH_AXES = ("x", "y", "c")

DEPTH = 4
N_BRANCHES = 4
BRANCH_WIDTH = 512
NORM_EPS = 1e-6
S5_GROUP = 16
S5_GROUPS = 32
S5_STATE = 64
S5_WIDTH = S5_GROUPS * S5_STATE
GDN_HEAD_DIM = 128
GDN_HEADS = 4
GDN_CONV = 4
GDN_CHUNK = 64
SWA_HEAD_DIM = 64
SWA_HEADS = 8
SWA_KV_HEADS = 2
SWA_GROUP = SWA_HEADS // SWA_KV_HEADS
WINDOW = 128
SWA_BLOCK = 128
T5_BUCKETS = 32
T5_MAX_DISTANCE = 128
MLA_HEADS = 4
MLA_Q_RANK = 384
MLA_KV_RANK = 512
MLA_NOPE = 128
MLA_ROPE = 64
MLA_V = 128
ROPE_THETA = 10000.0

ADAM_LR = 0.001
ADAM_B1 = 0.9
ADAM_B2 = 0.999
ADAM_EPS = 1e-08
ADAM_WD = 0.01
ADAM_STEP = 10

NEG_BIG = -1e30


def _cparams(*sem):
    return pltpu.CompilerParams(dimension_semantics=sem if sem else None, vmem_limit_bytes=VMEM_LIMIT_BYTES)


def _pick(n, pref, unit):
    if n <= pref:
        return n
    t = (pref // unit) * unit
    while t >= unit:
        if n % t == 0:
            return t
        t -= unit
    return n


def _bdot(a, b, dims):
    return lax.dot_general(a.astype(BF16), b.astype(BF16), (dims, ((), ())), preferred_element_type=F32)


def _dot_nn(a, b):
    return _bdot(a, b, ((1,), (0,)))


def _dot_nt(a, b):
    return _bdot(a, b, ((1,), (1,)))


def _dot_tn(a, b):
    return _bdot(a, b, ((0,), (0,)))


def _mm_body(a_ref, b_ref, o_ref, *, ta, tb, nk):
    dims = ((0,) if ta else (1,), (1,) if tb else (0,))
    part = _bdot(a_ref[...], b_ref[...], dims)
    if nk == 1:
        o_ref[...] = part
    else:
        k = pl.program_id(2)

        @pl.when(k == 0)
        def _():
            o_ref[...] = part

        @pl.when(k > 0)
        def _():
            o_ref[...] += part


MM_BLOCK_BYTES = 32 * 1024 * 1024


def _mm_tiles(m, n, k, a_bytes, b_bytes):
    tm, tn = _pick(m, 1024, LANES), _pick(n, 1024, LANES)
    tk = LANES if k % LANES == 0 else k
    for cand in range(k, 0, -LANES) if k % LANES == 0 else (k,):
        if k % cand == 0 and 2 * cand * (tm * a_bytes + tn * b_bytes) + 2 * tm * tn * 4 <= MM_BLOCK_BYTES:
            tk = cand
            break
    return tm, tn, tk


def _mm_call(a, b, *, ta=False, tb=False, name):
    m, k = (a.shape[1], a.shape[0]) if ta else a.shape
    n = b.shape[0] if tb else b.shape[1]
    assert (b.shape[1] if tb else b.shape[0]) == k, (a.shape, b.shape, ta, tb)
    tm, tn, tk = _mm_tiles(m, n, k, a.dtype.itemsize, b.dtype.itemsize)
    nk = k // tk
    a_spec = pl.BlockSpec((tk, tm), lambda i, j, kk: (kk, i)) if ta else pl.BlockSpec((tm, tk), lambda i, j, kk: (i, kk))
    b_spec = pl.BlockSpec((tn, tk), lambda i, j, kk: (j, kk)) if tb else pl.BlockSpec((tk, tn), lambda i, j, kk: (kk, j))
    return pl.pallas_call(
        functools.partial(_mm_body, ta=ta, tb=tb, nk=nk),
        name=name,
        grid=(m // tm, n // tn, nk),
        in_specs=[a_spec, b_spec],
        out_specs=pl.BlockSpec((tm, tn), lambda i, j, kk: (i, j)),
        out_shape=jax.ShapeDtypeStruct((m, n), F32),
        compiler_params=_cparams("parallel", "parallel", "arbitrary"),
    )(a, b)


@jax.custom_vjp
def mm(a, b):
    return _mm_call(a, b, name="mm_fwd")


def _mm_fwd(a, b):
    return _mm_call(a, b, name="mm_fwd"), (a, b)


def _mm_bwd(res, g):
    a, b = res
    return _mm_call(g, b, tb=True, name="mm_da"), _mm_call(a, g, ta=True, name="mm_db")


mm.defvjp(_mm_fwd, _mm_bwd)


@jax.custom_vjp
def wmm(a, w, w_bf16):
    return _mm_call(a, w_bf16, name="wmm_fwd")


def _wmm_fwd(a, w, w_bf16):
    return _mm_call(a, w_bf16, name="wmm_fwd"), (a, w_bf16)


def _wmm_bwd(res, g):
    a, w_bf16 = res
    return _mm_call(g, w_bf16, tb=True, name="wmm_da"), _mm_call(a, g, ta=True, name="wmm_db"), jnp.zeros_like(w_bf16)


wmm.defvjp(_wmm_fwd, _wmm_bwd)


def _rms_fwd_body(x_ref, g_ref, y_ref):
    x = x_ref[...]
    r = lax.rsqrt(jnp.mean(x * x, axis=-1, keepdims=True) + NORM_EPS)
    y_ref[...] = x * r * g_ref[...]


def _rms_bwd_body(x_ref, g_ref, dy_ref, dx_ref, dg_ref, *, tr):
    i = pl.program_id(0)

    @pl.when(i == 0)
    def _():
        dg_ref[...] = jnp.zeros_like(dg_ref)

    x = x_ref[...]
    dy = dy_ref[...]
    r = lax.rsqrt(jnp.mean(x * x, axis=-1, keepdims=True) + NORM_EPS)
    xhat = x * r
    gy = dy * g_ref[...]
    dx_ref[...] = r * (gy - xhat * jnp.mean(gy * xhat, axis=-1, keepdims=True))
    dg_ref[...] += jnp.sum((dy * xhat).reshape(tr // SUBLANES, SUBLANES, x.shape[-1]), axis=0)


def _rms_rows(rows, cols):
    return _pick(rows, max(SUBLANES, (2 * 1024 * 1024) // (4 * cols)), SUBLANES)


def _rms_fwd_call(x, gain):
    rows, cols = x.shape
    tr = _rms_rows(rows, cols)
    return pl.pallas_call(
        functools.partial(_rms_fwd_body),
        name="rms_fwd",
        grid=(rows // tr,),
        in_specs=[pl.BlockSpec((tr, cols), lambda i: (i, 0)), pl.BlockSpec((1, cols), lambda i: (0, 0))],
        out_specs=pl.BlockSpec((tr, cols), lambda i: (i, 0)),
        out_shape=jax.ShapeDtypeStruct((rows, cols), F32),
        compiler_params=_cparams("parallel"),
    )(x, gain.reshape(1, cols))


def _rms_bwd_call(x, gain, dy):
    rows, cols = x.shape
    tr = _rms_rows(rows, cols)
    dx, dg = pl.pallas_call(
        functools.partial(_rms_bwd_body, tr=tr),
        name="rms_bwd",
        grid=(rows // tr,),
        in_specs=[
            pl.BlockSpec((tr, cols), lambda i: (i, 0)),
            pl.BlockSpec((1, cols), lambda i: (0, 0)),
            pl.BlockSpec((tr, cols), lambda i: (i, 0)),
        ],
        out_specs=[pl.BlockSpec((tr, cols), lambda i: (i, 0)), pl.BlockSpec((SUBLANES, cols), lambda i: (0, 0))],
        out_shape=[jax.ShapeDtypeStruct((rows, cols), F32), jax.ShapeDtypeStruct((SUBLANES, cols), F32)],
        compiler_params=_cparams("arbitrary"),
    )(x, gain.reshape(1, cols), dy)
    return dx, jnp.sum(dg, axis=0)


@jax.custom_vjp
def rmsnorm(x, gain):
    return _rms_fwd_call(x, gain)


def _rmsnorm_fwd(x, gain):
    return _rms_fwd_call(x, gain), (x, gain)


def _rmsnorm_bwd(res, dy):
    x, gain = res
    return _rms_bwd_call(x, gain, dy)


rmsnorm.defvjp(_rmsnorm_fwd, _rmsnorm_bwd)


def _loss_body(y_ref, t_ref, rows_ref, dy_ref):
    d = y_ref[...] - t_ref[...]
    rows_ref[...] = 0.5 * jnp.mean(d * d, axis=-1, keepdims=True)
    dy_ref[...] = d * (1.0 / d.shape[-1])


def loss_head(y, target):
    rows, cols = y.shape
    tr = _rms_rows(rows, cols)
    return pl.pallas_call(
        functools.partial(_loss_body),
        name="loss_head",
        grid=(rows // tr,),
        in_specs=[pl.BlockSpec((tr, cols), lambda i: (i, 0))] * 2,
        out_specs=[pl.BlockSpec((tr, 1), lambda i: (i, 0)), pl.BlockSpec((tr, cols), lambda i: (i, 0))],
        out_shape=[jax.ShapeDtypeStruct((rows, 1), F32), jax.ShapeDtypeStruct((rows, cols), F32)],
        compiler_params=_cparams("parallel"),
    )(y, target)


def _adamw_body(w_ref, g_ref, m_ref, v_ref, d_ref, nm_ref, nv_ref):
    g = g_ref[...]
    m = ADAM_B1 * m_ref[...] + (1.0 - ADAM_B1) * g
    v = ADAM_B2 * v_ref[...] + (1.0 - ADAM_B2) * (g * g)
    m_hat = m / (1.0 - ADAM_B1**ADAM_STEP)
    v_hat = v / (1.0 - ADAM_B2**ADAM_STEP)
    d_ref[...] = -ADAM_LR * (m_hat / (jnp.sqrt(v_hat) + ADAM_EPS) + ADAM_WD * w_ref[...])
    nm_ref[...] = m
    nv_ref[...] = v


def adamw_flat(w, g, m, v):
    rows, cols = w.shape
    tr = _rms_rows(rows, cols)
    spec = pl.BlockSpec((tr, cols), lambda i: (i, 0))
    return pl.pallas_call(
        functools.partial(_adamw_body),
        name="adamw",
        grid=(rows // tr,),
        in_specs=[spec] * 4,
        out_specs=[spec] * 3,
        out_shape=[jax.ShapeDtypeStruct((rows, cols), F32)] * 3,
        compiler_params=_cparams("parallel"),
    )(w, g, m, v)


def _sum_body(x_ref, o_ref, *, n):
    acc = x_ref[0].astype(F32)
    for k in range(1, n):
        acc = acc + x_ref[k].astype(F32)
    o_ref[...] = acc


def sum_leading(x):
    n, rows, cols = x.shape
    tr = _pick(rows, 256, 2 * SUBLANES)
    return pl.pallas_call(
        functools.partial(_sum_body, n=n),
        name="sum_leading",
        grid=(rows // tr,),
        in_specs=[pl.BlockSpec((n, tr, cols), lambda i: (0, i, 0))],
        out_specs=pl.BlockSpec((tr, cols), lambda i: (i, 0)),
        out_shape=jax.ShapeDtypeStruct((rows, cols), F32),
        compiler_params=_cparams("parallel"),
    )(x)


def _mla_fwd_body(q_ref, k_ref, v_ref, o_ref, lse_ref, m_sc, l_sc, acc_sc, *, scale, nk):
    ki = pl.program_id(2)

    @pl.when(ki == 0)
    def _():
        m_sc[...] = jnp.full_like(m_sc, NEG_BIG)
        l_sc[...] = jnp.zeros_like(l_sc)
        acc_sc[...] = jnp.zeros_like(acc_sc)

    s = _dot_nt(q_ref[...], k_ref[...]) * scale
    m_prev = m_sc[...]
    m_new = jnp.maximum(m_prev, jnp.max(s, axis=1, keepdims=True))
    alpha = jnp.exp(m_prev - m_new)
    p = jnp.exp(s - m_new)
    l_sc[...] = alpha * l_sc[...] + jnp.sum(p, axis=1, keepdims=True)
    acc_sc[...] = alpha * acc_sc[...] + _dot_nn(p, v_ref[...])
    m_sc[...] = m_new

    @pl.when(ki == nk - 1)
    def _():
        o_ref[...] = acc_sc[...] / l_sc[...]
        lse_ref[...] = m_sc[...] + jnp.log(l_sc[...])


def _mla_fwd_call(q, k, v):
    h, seq, dq = q.shape
    dv = v.shape[-1]
    tq, tk = _pick(seq, 512, LANES), _pick(seq, 1024, LANES)
    nk = seq // tk
    scale = dq**-0.5
    return pl.pallas_call(
        functools.partial(_mla_fwd_body, scale=scale, nk=nk),
        name="mla_fwd",
        grid=(h, seq // tq, nk),
        in_specs=[
            pl.BlockSpec((None, tq, dq), lambda hh, i, j: (hh, i, 0)),
            pl.BlockSpec((None, tk, dq), lambda hh, i, j: (hh, j, 0)),
            pl.BlockSpec((None, tk, dv), lambda hh, i, j: (hh, j, 0)),
        ],
        out_specs=[
            pl.BlockSpec((None, tq, dv), lambda hh, i, j: (hh, i, 0)),
            pl.BlockSpec((None, tq, 1), lambda hh, i, j: (hh, i, 0)),
        ],
        out_shape=[jax.ShapeDtypeStruct((h, seq, dv), F32), jax.ShapeDtypeStruct((h, seq, 1), F32)],
        scratch_shapes=[pltpu.VMEM((tq, 1), F32), pltpu.VMEM((tq, 1), F32), pltpu.VMEM((tq, dv), F32)],
        compiler_params=_cparams("parallel", "parallel", "arbitrary"),
    )(q, k, v)


def _mla_bwd_body(q_ref, k_ref, v_ref, do_ref, lse_ref, dl_ref, dq_ref, dk_ref, dv_ref, dk_sc, dv_sc, *, scale, nq, tq):
    ki = pl.program_id(1)
    qi = pl.program_id(2)

    @pl.when(jnp.logical_and(ki == 0, qi == 0))
    def _():
        dq_ref[...] = jnp.zeros_like(dq_ref)

    @pl.when(qi == 0)
    def _():
        dk_sc[...] = jnp.zeros_like(dk_sc)
        dv_sc[...] = jnp.zeros_like(dv_sc)

    q = q_ref[...]
    k = k_ref[...]
    do = do_ref[...]
    p = jnp.exp(_dot_nt(q, k) * scale - lse_ref[...])
    dv_sc[...] += _dot_tn(p, do)
    ds = p * (_dot_nt(do, v_ref[...]) - dl_ref[...]) * scale
    dk_sc[...] += _dot_tn(ds, q)
    rows = pl.ds(pl.multiple_of(qi * tq, tq), tq)
    dq_ref[rows, :] += _dot_nn(ds, k)

    @pl.when(qi == nq - 1)
    def _():
        dk_ref[...] = dk_sc[...]
        dv_ref[...] = dv_sc[...]


def _mla_bwd_call(q, k, v, do, lse, delta):
    h, seq, dq = q.shape
    dv = v.shape[-1]
    tq, tk = _pick(seq, 512, LANES), _pick(seq, 1024, LANES)
    nq = seq // tq
    scale = dq**-0.5
    return pl.pallas_call(
        functools.partial(_mla_bwd_body, scale=scale, nq=nq, tq=tq),
        name="mla_bwd",
        grid=(h, seq // tk, nq),
        in_specs=[
            pl.BlockSpec((None, tq, dq), lambda hh, j, i: (hh, i, 0)),
            pl.BlockSpec((None, tk, dq), lambda hh, j, i: (hh, j, 0)),
            pl.BlockSpec((None, tk, dv), lambda hh, j, i: (hh, j, 0)),
            pl.BlockSpec((None, tq, dv), lambda hh, j, i: (hh, i, 0)),
            pl.BlockSpec((None, tq, 1), lambda hh, j, i: (hh, i, 0)),
            pl.BlockSpec((None, tq, 1), lambda hh, j, i: (hh, i, 0)),
        ],
        out_specs=[
            pl.BlockSpec((None, seq, dq), lambda hh, j, i: (hh, 0, 0)),
            pl.BlockSpec((None, tk, dq), lambda hh, j, i: (hh, j, 0)),
            pl.BlockSpec((None, tk, dv), lambda hh, j, i: (hh, j, 0)),
        ],
        out_shape=[
            jax.ShapeDtypeStruct((h, seq, dq), F32),
            jax.ShapeDtypeStruct((h, seq, dq), F32),
            jax.ShapeDtypeStruct((h, seq, dv), F32),
        ],
        scratch_shapes=[pltpu.VMEM((tk, dq), F32), pltpu.VMEM((tk, dv), F32)],
        compiler_params=_cparams("parallel", "arbitrary", "arbitrary"),
    )(q, k, v, do, lse, delta)


@jax.custom_vjp
def mla_attention(q, k, v):
    return _mla_fwd_call(q, k, v)[0]


def _mla_attention_fwd(q, k, v):
    o, lse = _mla_fwd_call(q, k, v)
    return o, (q, k, v, o, lse)


def _mla_attention_bwd(res, do):
    q, k, v, o, lse = res
    delta = jnp.sum(do * o, axis=-1, keepdims=True)
    return tuple(_mla_bwd_call(q, k, v, do, lse, delta))


mla_attention.defvjp(_mla_attention_fwd, _mla_attention_bwd)


def _swa_block(q4, kp, kc, kn, vp, vc, vn, bias, sink, *, valid):
    kb = jnp.concatenate([kp, kc, kn], axis=0)
    vb = jnp.concatenate([vp, vc, vn], axis=0)
    s = _dot_nt(q4, kb) * (SWA_HEAD_DIM**-0.5) + bias
    s = jnp.where(valid, s, NEG_BIG)
    m = lax.stop_gradient(jnp.maximum(jnp.max(s, axis=1, keepdims=True), sink))
    p = jnp.exp(s - m)
    denom = jnp.sum(p, axis=1, keepdims=True) + jnp.exp(sink - m)
    return _dot_nn(p / denom, vb)


def _swa_valid(n, seq):
    rows = SWA_GROUP * SWA_BLOCK
    qi = lax.broadcasted_iota(jnp.int32, (rows, 3 * SWA_BLOCK), 0) % SWA_BLOCK
    sj = lax.broadcasted_iota(jnp.int32, (rows, 3 * SWA_BLOCK), 1)
    rel = sj - SWA_BLOCK - qi
    kpos = n * SWA_BLOCK + sj - SWA_BLOCK
    return (jnp.abs(rel) <= WINDOW) & (kpos >= 0) & (kpos < seq)


def _swa_operands(q_ref, kp_ref, kc_ref, kn_ref, vp_ref, vc_ref, vn_ref, b_ref, s_ref):
    q4 = q_ref[...].reshape(SWA_GROUP * SWA_BLOCK, SWA_HEAD_DIM)
    return (q4, kp_ref[...], kc_ref[...], kn_ref[...], vp_ref[...], vc_ref[...], vn_ref[...], b_ref[...], s_ref[...])


def _swa_fwd_body(q_ref, kp_ref, kc_ref, kn_ref, vp_ref, vc_ref, vn_ref, b_ref, s_ref, o_ref, *, seq):
    valid = _swa_valid(pl.program_id(1), seq)
    out = _swa_block(*_swa_operands(q_ref, kp_ref, kc_ref, kn_ref, vp_ref, vc_ref, vn_ref, b_ref, s_ref), valid=valid)
    o_ref[...] = out.reshape(SWA_GROUP, SWA_BLOCK, SWA_HEAD_DIM)


def _swa_bwd_body(q_ref, kp_ref, kc_ref, kn_ref, vp_ref, vc_ref, vn_ref, b_ref, s_ref, do_ref,
                  dq_ref, dk_ref, dv_ref, db_ref, ds_ref, *, seq):
    n = pl.program_id(1)

    @pl.when(n == 0)
    def _():
        db_ref[...] = jnp.zeros_like(db_ref)
        ds_ref[...] = jnp.zeros_like(ds_ref)

    valid = _swa_valid(n, seq)
    ops = _swa_operands(q_ref, kp_ref, kc_ref, kn_ref, vp_ref, vc_ref, vn_ref, b_ref, s_ref)
    _, vjp = jax.vjp(functools.partial(_swa_block, valid=valid), *ops)
    do = do_ref[...].reshape(SWA_GROUP * SWA_BLOCK, SWA_HEAD_DIM)
    dq4, dkp, dkc, dkn, dvp, dvc, dvn, dbias, dsink = vjp(do)
    dq_ref[...] = dq4.reshape(SWA_GROUP, SWA_BLOCK, SWA_HEAD_DIM)
    dk_ref[0] = dkp
    dk_ref[1] = dkc
    dk_ref[2] = dkn
    dv_ref[0] = dvp
    dv_ref[1] = dvc
    dv_ref[2] = dvn
    db_ref[...] += dbias
    ds_ref[...] += dsink


def _swa_in_specs(nb):
    blk = (None, SWA_BLOCK, SWA_HEAD_DIM)
    prev = lambda h, n: (h, jnp.maximum(n - 1, 0), 0)
    own = lambda h, n: (h, n, 0)
    nxt = lambda h, n: (h, jnp.minimum(n + 1, nb - 1), 0)
    rows = SWA_GROUP * SWA_BLOCK
    return [
        pl.BlockSpec((None, SWA_GROUP, SWA_BLOCK, SWA_HEAD_DIM), lambda h, n: (h, 0, n, 0)),
        pl.BlockSpec(blk, prev), pl.BlockSpec(blk, own), pl.BlockSpec(blk, nxt),
        pl.BlockSpec(blk, prev), pl.BlockSpec(blk, own), pl.BlockSpec(blk, nxt),
        pl.BlockSpec((None, rows, 3 * SWA_BLOCK), lambda h, n: (h, 0, 0)),
        pl.BlockSpec((None, rows, 1), lambda h, n: (h, 0, 0)),
    ]


def _swa_fwd_call(q, k, v, bias, sink):
    kv, g, seq, d = q.shape
    nb = seq // SWA_BLOCK
    return pl.pallas_call(
        functools.partial(_swa_fwd_body, seq=seq),
        name="swa_fwd",
        grid=(kv, nb),
        in_specs=_swa_in_specs(nb),
        out_specs=pl.BlockSpec((None, g, SWA_BLOCK, d), lambda h, n: (h, 0, n, 0)),
        out_shape=jax.ShapeDtypeStruct(q.shape, F32),
        compiler_params=_cparams("parallel", "parallel"),
    )(q, k, k, k, v, v, v, bias, sink)


def _swa_bwd_call(q, k, v, bias, sink, do):
    kv, g, seq, d = q.shape
    nb = seq // SWA_BLOCK
    rows = g * SWA_BLOCK
    part = jax.ShapeDtypeStruct((kv, nb, 3, SWA_BLOCK, d), F32)
    part_spec = pl.BlockSpec((None, None, 3, SWA_BLOCK, d), lambda h, n: (h, n, 0, 0, 0))
    dq, dkp, dvp, dbias, dsink = pl.pallas_call(
        functools.partial(_swa_bwd_body, seq=seq),
        name="swa_bwd",
        grid=(kv, nb),
        in_specs=_swa_in_specs(nb) + [pl.BlockSpec((None, g, SWA_BLOCK, d), lambda h, n: (h, 0, n, 0))],
        out_specs=[
            pl.BlockSpec((None, g, SWA_BLOCK, d), lambda h, n: (h, 0, n, 0)),
            part_spec, part_spec,
            pl.BlockSpec((None, rows, 3 * SWA_BLOCK), lambda h, n: (h, 0, 0)),
            pl.BlockSpec((None, rows, 1), lambda h, n: (h, 0, 0)),
        ],
        out_shape=[jax.ShapeDtypeStruct(q.shape, F32), part, part,
                   jax.ShapeDtypeStruct(bias.shape, F32), jax.ShapeDtypeStruct(sink.shape, F32)],
        compiler_params=_cparams("parallel", "arbitrary"),
    )(q, k, k, k, v, v, v, bias, sink, do)

    def fold(p):
        zero = jnp.zeros_like(p[:, :1, 0])
        total = p[:, :, 1] + jnp.concatenate([p[:, 1:, 0], zero], axis=1) + jnp.concatenate([zero, p[:, :-1, 2]], axis=1)
        return total.reshape(kv, seq, d)

    return dq, fold(dkp), fold(dvp), dbias, dsink


@jax.custom_vjp
def swa_attention(q, k, v, bias, sink):
    return _swa_fwd_call(q, k, v, bias, sink)


def _swa_attention_fwd(q, k, v, bias, sink):
    return _swa_fwd_call(q, k, v, bias, sink), (q, k, v, bias, sink)


def _swa_attention_bwd(res, do):
    return _swa_bwd_call(*res, do)


swa_attention.defvjp(_swa_attention_fwd, _swa_attention_bwd)


def _scan_tiles(n_tiles, reverse, tile_fn, init):
    def step(i, carry):
        ti = (n_tiles - 1 - i) if reverse else i
        return tile_fn(pl.multiple_of(ti * SUBLANES, SUBLANES), carry)

    return lax.fori_loop(0, n_tiles, step, init)


def _row_order(reverse):
    return tuple(reversed(range(SUBLANES))) if reverse else tuple(range(SUBLANES))


def _s5_fwd_dir(a_ref, bu_ref, s_ref, carry, *, reverse, tt, w):
    ar, ai = a_ref[0:1, :], a_ref[1:2, :]
    rowid = lax.broadcasted_iota(jnp.int32, (SUBLANES, w), 0)

    def tile(base, c):
        sr, si = c
        x = bu_ref[pl.ds(base, SUBLANES), :]
        xr, xi = x[:, :w], x[:, w:]
        out_r = jnp.zeros((SUBLANES, w), F32)
        out_i = jnp.zeros((SUBLANES, w), F32)
        for j in _row_order(reverse):
            nr = ar * sr - ai * si + xr[j:j + 1, :]
            ni = ar * si + ai * sr + xi[j:j + 1, :]
            out_r = jnp.where(rowid == j, nr, out_r)
            out_i = jnp.where(rowid == j, ni, out_i)
            sr, si = nr, ni
        s_ref[pl.ds(base, SUBLANES), :] = jnp.concatenate([out_r, out_i], axis=1)
        return sr, si

    sr, si = _scan_tiles(tt // SUBLANES, reverse, tile, (carry[0:1, :], carry[1:2, :]))
    carry[0:1, :] = sr
    carry[1:2, :] = si


def _s5_fwd_body(a_ref, bu_ref, s_ref, carry, *, tt, w):
    d = pl.program_id(0)

    @pl.when(pl.program_id(1) == 0)
    def _():
        carry[...] = jnp.zeros_like(carry)

    @pl.when(d == 0)
    def _():
        _s5_fwd_dir(a_ref, bu_ref, s_ref, carry, reverse=False, tt=tt, w=w)

    @pl.when(d == 1)
    def _():
        _s5_fwd_dir(a_ref, bu_ref, s_ref, carry, reverse=True, tt=tt, w=w)


def _s5_bwd_dir(a_ref, s_ref, ds_ref, dbu_ref, da_ref, carry, acc, *, reverse, tt, w):
    ar, ai = a_ref[0:1, :], a_ref[1:2, :]
    rowid = lax.broadcasted_iota(jnp.int32, (SUBLANES, w), 0)

    first = _row_order(reverse)[0]
    acc[...] = jnp.zeros_like(acc)

    def tile(base, c):
        lr, li = lam_in_r, lam_in_i = c
        s = s_ref[pl.ds(base, SUBLANES), :]
        g = ds_ref[pl.ds(base, SUBLANES), :]
        out_r = jnp.zeros((SUBLANES, w), F32)
        out_i = jnp.zeros((SUBLANES, w), F32)
        for j in _row_order(reverse):
            nr = g[j:j + 1, :w] + ar * lr + ai * li
            ni = g[j:j + 1, w:] + ar * li - ai * lr
            out_r = jnp.where(rowid == j, nr, out_r)
            out_i = jnp.where(rowid == j, ni, out_i)
            lr, li = nr, ni
        dbu_ref[pl.ds(base, SUBLANES), :] = jnp.concatenate([out_r, out_i], axis=1)
        shift = SUBLANES - 1 if reverse else 1
        next_r = jnp.where(rowid == first, lam_in_r, pltpu.roll(out_r, shift, axis=0))
        next_i = jnp.where(rowid == first, lam_in_i, pltpu.roll(out_i, shift, axis=0))
        sr, si = s[:, :w], s[:, w:]
        acc[0] += sr * next_r + si * next_i
        acc[1] += sr * next_i - si * next_r
        return lr, li

    lr, li = _scan_tiles(tt // SUBLANES, reverse, tile, (carry[0:1, :], carry[1:2, :]))
    carry[0:1, :] = lr
    carry[1:2, :] = li
    da_ref[0:1, :] += jnp.sum(acc[0], axis=0, keepdims=True)
    da_ref[1:2, :] += jnp.sum(acc[1], axis=0, keepdims=True)


def _s5_bwd_body(a_ref, s_ref, ds_ref, dbu_ref, da_ref, carry, acc, *, tt, w):
    d = pl.program_id(0)

    @pl.when(pl.program_id(1) == 0)
    def _():
        carry[...] = jnp.zeros_like(carry)
        da_ref[...] = jnp.zeros_like(da_ref)

    @pl.when(d == 0)
    def _():
        _s5_bwd_dir(a_ref, s_ref, ds_ref, dbu_ref, da_ref, carry, acc, reverse=True, tt=tt, w=w)

    @pl.when(d == 1)
    def _():
        _s5_bwd_dir(a_ref, s_ref, ds_ref, dbu_ref, da_ref, carry, acc, reverse=False, tt=tt, w=w)


def _s5_time_map(nt, flip_dir):
    def time_block(d, t):
        back = nt - 1 - t
        return jnp.where(d == flip_dir, back, t)

    return time_block


def _s5_fwd_call(a, bu):
    seq, w4 = bu.shape
    w = w4 // 4
    tt = _pick(seq, 256, SUBLANES)
    nt = seq // tt
    tb = _s5_time_map(nt, 1)
    return pl.pallas_call(
        functools.partial(_s5_fwd_body, tt=tt, w=w),
        name="s5_scan_fwd",
        grid=(2, nt),
        in_specs=[
            pl.BlockSpec((None, 2, w), lambda d, t: (d, 0, 0)),
            pl.BlockSpec((tt, 2 * w), lambda d, t: (tb(d, t), d)),
        ],
        out_specs=pl.BlockSpec((tt, 2 * w), lambda d, t: (tb(d, t), d)),
        out_shape=jax.ShapeDtypeStruct(bu.shape, F32),
        scratch_shapes=[pltpu.VMEM((2, w), F32)],
        compiler_params=_cparams("parallel", "arbitrary"),
    )(a, bu)


def _s5_bwd_call(a, s_prev, ds):
    seq, w4 = ds.shape
    w = w4 // 4
    tt = _pick(seq, 256, SUBLANES)
    nt = seq // tt
    tb = _s5_time_map(nt, 0)
    blk = pl.BlockSpec((tt, 2 * w), lambda d, t: (tb(d, t), d))
    return pl.pallas_call(
        functools.partial(_s5_bwd_body, tt=tt, w=w),
        name="s5_scan_bwd",
        grid=(2, nt),
        in_specs=[pl.BlockSpec((None, 2, w), lambda d, t: (d, 0, 0)), blk, blk],
        out_specs=[blk, pl.BlockSpec((None, 2, w), lambda d, t: (d, 0, 0))],
        out_shape=[jax.ShapeDtypeStruct(ds.shape, F32), jax.ShapeDtypeStruct(a.shape, F32)],
        scratch_shapes=[pltpu.VMEM((2, w), F32), pltpu.VMEM((2, SUBLANES, w), F32)],
        compiler_params=_cparams("parallel", "arbitrary"),
    )(a, s_prev, ds)


@jax.custom_vjp
def s5_scan(a, bu):
    return _s5_fwd_call(a, bu)


def _s5_scan_fwd(a, bu):
    s = _s5_fwd_call(a, bu)
    return s, (a, s)


def _s5_scan_bwd(res, ds):
    a, s = res
    dbu, da = _s5_bwd_call(a, s, ds)
    return da, dbu


s5_scan.defvjp(_s5_scan_fwd, _s5_scan_bwd)


NN, NT, TN = ((1,), (0,)), ((1,), (1,)), ((0,), (0,))


def _bmm(a, b, dims):
    return jnp.stack([_bdot(a[i], b[i], dims) for i in range(a.shape[0])])


def _bmm3(a, b):
    ah = a.astype(BF16)
    bh = b.astype(BF16)
    al = a - ah.astype(F32)
    bl = b - bh.astype(F32)
    return _bmm(ah, bh, NN) + _bmm(ah, bl, NN) + _bmm(al, bh, NN)


GDN_INV_BASE = 8


def _unit_triangular_inverse(a, ri, ci):
    c = a.shape[-1]

    def same_block(size):
        shift = int(math.log2(size))
        return lax.shift_right_logical(ri, shift) == lax.shift_right_logical(ci, shift)

    diag = jnp.where(same_block(GDN_INV_BASE), a, 0.0)
    inv = jnp.where(ri == ci, 1.0, 0.0) - diag
    power = diag
    for _ in range(int(math.log2(GDN_INV_BASE)) - 1):
        power = _bmm3(power, power)
        inv = inv + _bmm3(inv, power)
    size = GDN_INV_BASE
    while size < c:
        off = jnp.where(jnp.logical_and(same_block(2 * size), jnp.logical_not(same_block(size))), a, 0.0)
        inv = inv - _bmm3(_bmm3(inv, off), inv)
        size *= 2
    return inv


def _gdn_chunks(q, k, v, gr, br, state, *, n_fwd):
    b, c, _ = q.shape
    dv = v.shape[2]
    ri = lax.broadcasted_iota(jnp.int32, (b, c, c), 1)
    ci = lax.broadcasted_iota(jnp.int32, (b, c, c), 2)
    fwd = jnp.stack([jnp.full((c, c), 1 if i < n_fwd else 0, jnp.int32) for i in range(b)]) == 1
    lower = jnp.logical_or(jnp.logical_and(fwd, ri >= ci), jnp.logical_and(jnp.logical_not(fwd), ri <= ci))
    strict = jnp.logical_and(lower, ri != ci)
    as_column = lambda row: jnp.sum(jnp.where(ri == ci, row, 0.0), axis=2, keepdims=True)
    gc, bc = as_column(gr), as_column(br)
    kb = k * bc
    decay = jnp.where(lower, jnp.exp(jnp.where(lower, gc - gr, 0.0)), 0.0)
    a = jnp.where(strict, _bmm(kb, k, NT) * decay, 0.0)
    inv = _unit_triangular_inverse(a, ri, ci)
    eg = jnp.exp(gc)
    sol = _bmm3(inv, jnp.concatenate([v * bc, kb * eg], axis=2))
    u, w = sol[:, :, :dv], sol[:, :, dv:]
    attn = _bmm(q, k, NT) * decay
    v_new = u - _bmm(w, state, NN)
    o = _bmm(q * eg, state, NN) + _bmm(attn, v_new, NN)
    row = lax.broadcasted_iota(jnp.int32, (b, c, 1), 1)
    fwd_col = jnp.stack([jnp.full((c, 1), 1 if i < n_fwd else 0, jnp.int32) for i in range(b)]) == 1
    last = jnp.logical_or(jnp.logical_and(fwd_col, row == c - 1), jnp.logical_and(jnp.logical_not(fwd_col), row == 0))
    g_last = jnp.sum(jnp.where(last, gc, 0.0), axis=1, keepdims=True)
    new_state = state * jnp.exp(g_last) + _bmm(k * jnp.exp(g_last - gc), v_new, TN)
    return o, new_state


def _gdn_heads(refs, heads, dh):
    return jnp.stack([r[:, h * dh:(h + 1) * dh] for r in refs for h in range(heads)])


def _gdn_operands(q_refs, k_refs, v_refs, gr_refs, br_refs, heads, dh):
    rows = lambda refs: jnp.concatenate([r[...] for r in refs], axis=0)
    return (_gdn_heads(q_refs, heads, dh), _gdn_heads(k_refs, heads, dh), _gdn_heads(v_refs, heads, dh),
            rows(gr_refs), rows(br_refs))


def _gdn_fwd_body(qf, kf, vf, qb, kb, vb, grf, brf, grb, brb, of, ob, s0f, s0b, state, *, heads, dh):
    @pl.when(pl.program_id(0) == 0)
    def _():
        state[...] = jnp.zeros_like(state)

    s0 = state[...]
    s0f[...] = s0[:heads]
    s0b[...] = s0[heads:]
    o, new_state = _gdn_chunks(*_gdn_operands((qf, qb), (kf, kb), (vf, vb), (grf, grb), (brf, brb), heads, dh), s0,
                               n_fwd=heads)
    state[...] = new_state
    for i in range(2 * heads):
        (of, ob)[i // heads][:, (i % heads) * dh:(i % heads + 1) * dh] = o[i]


def _gdn_bwd_body(qf, kf, vf, qb, kb, vb, grf, brf, grb, brb, s0f, s0b, dof, dob,
                  dqf, dkf, dvf, dqb, dkb, dvb, dgrf, dbrf, dgrb, dbrb, dstate, *, heads, dh):
    @pl.when(pl.program_id(0) == 0)
    def _():
        dstate[...] = jnp.zeros_like(dstate)

    ops = _gdn_operands((qf, qb), (kf, kb), (vf, vb), (grf, grb), (brf, brb), heads, dh)
    s0 = jnp.concatenate([s0f[...], s0b[...]], axis=0)
    do = _gdn_heads((dof, dob), heads, dh)
    _, vjp = jax.vjp(functools.partial(_gdn_chunks, n_fwd=heads), *ops, s0)
    dq, dk, dv, dgr, dbr, ds0 = vjp((do, dstate[...]))
    dstate[...] = ds0
    for i in range(2 * heads):
        d, cols = i // heads, slice((i % heads) * dh, (i % heads + 1) * dh)
        (dqf, dqb)[d][:, cols] = dq[i]
        (dkf, dkb)[d][:, cols] = dk[i]
        (dvf, dvb)[d][:, cols] = dv[i]
    dgrf[...] = dgr[:heads]
    dgrb[...] = dgr[heads:]
    dbrf[...] = dbr[:heads]
    dbrb[...] = dbr[heads:]


def _gdn_specs(nc, heads, dh, backward):
    c = GDN_CHUNK
    up = lambda n: n
    down = lambda n: nc - 1 - n
    out = []
    for chunk in ((down, up) if backward else (up, down)):
        out.append((
            pl.BlockSpec((c, heads * dh), lambda n, chunk=chunk: (chunk(n), 0)),
            pl.BlockSpec((heads, None, 1, c), lambda n, chunk=chunk: (0, chunk(n), 0, 0)),
            pl.BlockSpec((heads, None, dh, dh), lambda n, chunk=chunk: (0, chunk(n), 0, 0)),
        ))
    return out


def _gdn_fwd_call(q, k, v, grf, brf, grb, brb):
    seq, width = q.shape
    heads = grf.shape[0]
    dh = width // heads
    nc = seq // GDN_CHUNK
    (seq_f, row_f, st_f), (seq_b, row_b, st_b) = _gdn_specs(nc, heads, dh, False)
    states = jax.ShapeDtypeStruct((heads, nc, dh, dh), F32)
    return pl.pallas_call(
        functools.partial(_gdn_fwd_body, heads=heads, dh=dh),
        name="gdn_fwd",
        grid=(nc,),
        in_specs=[seq_f] * 3 + [seq_b] * 3 + [row_f, row_f, row_b, row_b],
        out_specs=[seq_f, seq_b, st_f, st_b],
        out_shape=[jax.ShapeDtypeStruct(q.shape, F32)] * 2 + [states] * 2,
        scratch_shapes=[pltpu.VMEM((2 * heads, dh, dh), F32)],
        compiler_params=_cparams("arbitrary"),
    )(q, k, v, q, k, v, grf, brf, grb, brb)


def _gdn_bwd_call(q, k, v, grf, brf, grb, brb, s0f, s0b, dof, dob):
    seq, width = q.shape
    heads = grf.shape[0]
    dh = width // heads
    nc = seq // GDN_CHUNK
    (seq_f, row_f, st_f), (seq_b, row_b, st_b) = _gdn_specs(nc, heads, dh, True)
    like = lambda t: jax.ShapeDtypeStruct(t.shape, F32)
    return pl.pallas_call(
        functools.partial(_gdn_bwd_body, heads=heads, dh=dh),
        name="gdn_bwd",
        grid=(nc,),
        in_specs=[seq_f] * 3 + [seq_b] * 3 + [row_f, row_f, row_b, row_b, st_f, st_b, seq_f, seq_b],
        out_specs=[seq_f] * 3 + [seq_b] * 3 + [row_f, row_f, row_b, row_b],
        out_shape=[like(q)] * 6 + [like(grf), like(brf), like(grb), like(brb)],
        scratch_shapes=[pltpu.VMEM((2 * heads, dh, dh), F32)],
        compiler_params=_cparams("arbitrary"),
    )(q, k, v, q, k, v, grf, brf, grb, brb, s0f, s0b, dof, dob)


@jax.custom_vjp
def gdn_delta_rule(q, k, v, grf, brf, grb, brb):
    return tuple(_gdn_fwd_call(q, k, v, grf, brf, grb, brb)[:2])


def _gdn_delta_rule_fwd(*ops):
    of, ob, s0f, s0b = _gdn_fwd_call(*ops)
    return (of, ob), (*ops, s0f, s0b)


def _gdn_delta_rule_bwd(res, do):
    dqf, dkf, dvf, dqb, dkb, dvb, *small = _gdn_bwd_call(*res, *do)
    return (dqf + dqb, dkf + dkb, dvf + dvb, *small)


gdn_delta_rule.defvjp(_gdn_delta_rule_fwd, _gdn_delta_rule_bwd)


def _mesh_position():
    return lax.axis_index("x"), lax.axis_index("y"), lax.axis_index("c")


def _all_gather_body(x_ref, out_ref, send_sems, recv_sems, local_sem):
    x, y, c = _mesh_position()
    me, sibling = (x, y, c), (x, y, 1 - c)
    chips = [(1 - x, y), (x, 1 - y), (1 - x, 1 - y)]

    def slot(px, py, pc):
        return out_ref.at[4 * px + 2 * py + pc]

    def copy(k, block, to, src=None):
        return pltpu.make_async_remote_copy(
            src_ref=slot(*block) if src is None else src, dst_ref=slot(*block),
            send_sem=send_sems.at[k], recv_sem=recv_sems.at[k], device_id=to, device_id_type=pl.DeviceIdType.MESH)

    mine = pltpu.make_async_copy(x_ref, slot(*me), local_sem)
    mine.start()
    first = [copy(0, me, sibling, src=x_ref)]
    first += [copy(1 + j, me, (*chip, c), src=x_ref) for j, chip in enumerate(chips)]
    for cp in first:
        cp.start()
    passed = [copy(4 + j, (*chip, c), sibling) for j, chip in enumerate(chips)]
    for j, chip in enumerate(chips):
        copy(1 + j, (*chip, c), me).wait_recv()
        passed[j].start()
    copy(0, sibling, me).wait_recv()
    for j, chip in enumerate(chips):
        copy(4 + j, (*chip, 1 - c), me).wait_recv()
    for cp in first + passed:
        cp.wait_send()
    mine.wait()


def all_gather(x, name):
    return pl.pallas_call(
        functools.partial(_all_gather_body),
        name=name,
        in_specs=[pl.BlockSpec(memory_space=pl.ANY)],
        out_specs=pl.BlockSpec(memory_space=pl.ANY),
        out_shape=jax.ShapeDtypeStruct((N_DEV,) + x.shape, x.dtype),
        scratch_shapes=[pltpu.SemaphoreType.DMA((N_DEV - 1,)), pltpu.SemaphoreType.DMA((N_DEV - 1,)), pltpu.SemaphoreType.DMA],
    )(x)


def _all_to_all_body(x_ref, out_ref, send_sems, recv_sems, local_sem):
    x, y, c = _mesh_position()
    my = 4 * x + 2 * y + c
    copies = []
    for m in range(1, N_DEV):
        px = 1 - x if m & 4 else x
        py = 1 - y if m & 2 else y
        pc = 1 - c if m & 1 else c
        copies.append(pltpu.make_async_remote_copy(
            src_ref=x_ref.at[4 * px + 2 * py + pc], dst_ref=out_ref.at[my],
            send_sem=send_sems.at[m - 1], recv_sem=recv_sems.at[m - 1],
            device_id=(px, py, pc), device_id_type=pl.DeviceIdType.MESH))
    for cp in copies:
        cp.start()
    mine = pltpu.make_async_copy(x_ref.at[my], out_ref.at[my], local_sem)
    mine.start()
    for cp in copies:
        cp.wait()
    mine.wait()


def all_to_all(x, name):
    return pl.pallas_call(
        functools.partial(_all_to_all_body),
        name=name,
        in_specs=[pl.BlockSpec(memory_space=pl.ANY)],
        out_specs=pl.BlockSpec(memory_space=pl.ANY),
        out_shape=jax.ShapeDtypeStruct(x.shape, x.dtype),
        scratch_shapes=[pltpu.SemaphoreType.DMA((N_DEV - 1,)), pltpu.SemaphoreType.DMA((N_DEV - 1,)), pltpu.SemaphoreType.DMA],
    )(x)


PACK_COLS = 1024
PACK_ROW_MULTIPLE = 512


def _pack(arrays, dtype):
    parts = []
    rows = 0
    for a in arrays:
        flat = a.reshape(-1).astype(dtype)
        n_rows = -(-flat.shape[0] // PACK_COLS)
        parts.append(jnp.pad(flat, (0, n_rows * PACK_COLS - flat.shape[0])).reshape(n_rows, PACK_COLS))
        rows += n_rows
    pad_rows = -rows % PACK_ROW_MULTIPLE
    if pad_rows:
        parts.append(jnp.zeros((pad_rows, PACK_COLS), dtype))
    return jnp.concatenate(parts, axis=0)


def _unpack(flat, shapes):
    lead = flat.shape[:-2]
    out = []
    row = 0
    for shape in shapes:
        size = int(np.prod(shape))
        n_rows = -(-size // PACK_COLS)
        seg = flat[..., row:row + n_rows, :].reshape(lead + (n_rows * PACK_COLS,))[..., :size]
        out.append(seg.reshape(lead + tuple(shape)))
        row += n_rows
    return out


W_IN_SMALL = 4608


def _split_w_in(w_in):
    d = w_in.shape[0]
    small = jnp.concatenate(
        [w_in[:, 0:2560], w_in[:, 2576:4304], w_in[:, 2560:2576], jnp.zeros((d, W_IN_SMALL - 4304), w_in.dtype)], axis=1)
    return small, w_in[:, 4304:]


def _s5_mixer(u, lam_re, lam_im, log_step, b_re, b_im, c_re, c_im, d_skip, w_glu, b_glu):
    g, p, hg = S5_GROUPS, S5_STATE, S5_GROUP
    lam_re = jnp.minimum(lam_re, -1e-4)
    dt = jnp.exp(log_step)[..., None]
    mag = jnp.exp(lam_re * dt)
    abar_r = mag * jnp.cos(lam_im * dt)
    abar_i = mag * jnp.sin(lam_im * dt)
    den = lam_re * lam_re + lam_im * lam_im
    xr = abar_r - 1.0
    xi = abar_i
    coef_r = (xr * lam_re + xi * lam_im) / den
    coef_i = (xi * lam_re - xr * lam_im) / den
    bbar_r = coef_r[..., None] * b_re - coef_i[..., None] * b_im
    bbar_i = coef_r[..., None] * b_im + coef_i[..., None] * b_re
    eye = jnp.eye(g, dtype=F32)

    def block_diag(t, rows, cols):
        return (eye[:, None, :, None] * t[:, :, None, :]).reshape(g * rows, g * cols)

    b_all = jnp.concatenate(
        [block_diag(t.transpose(0, 2, 1), hg, p) for t in (bbar_r[0], bbar_i[0], bbar_r[1], bbar_i[1])], axis=1)
    c_all = jnp.concatenate(
        [block_diag(t.transpose(0, 2, 1), p, hg) for t in (c_re[0], -c_im[0], c_re[1], -c_im[1])], axis=0)
    a_all = jnp.stack([abar_r.reshape(2, g * p), abar_i.reshape(2, g * p)], axis=1)
    s = s5_scan(a_all, mm(u, b_all))
    y = mm(s, c_all) + d_skip * u
    y = jax.nn.gelu(y)
    return y * jax.nn.sigmoid(wmm(y, *w_glu) + b_glu)


def _gdn_mixer(qkv, z, beta_logits, decay_logits, conv_w, a_log, dt_bias, o_gain):
    seq = qkv.shape[0]
    h, dh, c = GDN_HEADS, GDN_HEAD_DIM, GDN_CHUNK
    nc = seq // c
    padded = jnp.pad(qkv, ((GDN_CONV // 2, GDN_CONV - 1 - GDN_CONV // 2), (0, 0)))
    conv = sum(padded[j:j + seq] * conv_w[j] for j in range(GDN_CONV))
    q, k, v = jnp.split(jax.nn.silu(conv), 3, axis=-1)

    def l2(t):
        t = t.reshape(seq, h, dh)
        return (t * lax.rsqrt(jnp.sum(t * t, axis=-1, keepdims=True) + 1e-6)).reshape(seq, h * dh)

    q = l2(q) * (dh**-0.5)
    k = l2(k)
    beta = jax.nn.sigmoid(beta_logits).reshape(seq, 2, h)
    g = -jnp.exp(a_log) * jax.nn.softplus(decay_logits.reshape(seq, 2, h) + dt_bias)
    small = []
    for d in range(2):
        gcs = lax.cumsum(g[:, d].reshape(nc, c, h), axis=1, reverse=d == 1).transpose(2, 0, 1)
        small += [gcs.reshape(h, nc, 1, c), beta[:, d].T.reshape(h, nc, 1, c)]
    o_fwd, o_bwd = gdn_delta_rule(q, k, v, *small)
    o = rmsnorm((o_fwd + o_bwd).reshape(seq * h, dh), o_gain).reshape(seq, h * dh)
    return o * jax.nn.silu(z)


def _t5_bucket(rel):
    nb = T5_BUCKETS // 2
    max_exact = nb // 2
    ret = jnp.where(rel > 0, nb, 0)
    n = jnp.abs(rel)
    nf = jnp.maximum(n, 1).astype(F32)
    large = max_exact + (jnp.log(nf / max_exact) / math.log(T5_MAX_DISTANCE / max_exact) * (nb - max_exact)).astype(jnp.int32)
    large = jnp.minimum(large, nb - 1)
    return ret + jnp.where(n < max_exact, n, large)


def _swa_mixer(q, kv, sink, t5_bias):
    seq = q.shape[0]
    kvh, g, d, blk = SWA_KV_HEADS, SWA_GROUP, SWA_HEAD_DIM, SWA_BLOCK
    q4 = q.reshape(seq, kvh, g, d).transpose(1, 2, 0, 3)
    k, v = jnp.split(kv, 2, axis=-1)
    heads_first = lambda t: t.reshape(seq, kvh, d).transpose(1, 0, 2)
    rel = jnp.arange(3 * blk)[None, :] - blk - jnp.arange(blk)[:, None]
    onehot = (_t5_bucket(rel)[..., None] == jnp.arange(T5_BUCKETS)).astype(F32)
    bias = jnp.einsum("qsb,bh->hqs", onehot, t5_bias, precision=lax.Precision.HIGHEST).reshape(kvh, g * blk, 3 * blk)
    sink_col = jnp.broadcast_to(sink.reshape(kvh, g, 1, 1), (kvh, g, blk, 1)).reshape(kvh, g * blk, 1)
    o = swa_attention(q4, heads_first(k), heads_first(v), bias, sink_col)
    return o.transpose(2, 0, 1, 3).reshape(seq, kvh * g * d)


def _rope(t, cos, sin):
    t1, t2 = jnp.split(t, 2, axis=-1)
    return jnp.concatenate([t1 * cos - t2 * sin, t2 * cos + t1 * sin], axis=-1)


def _mla_mixer(c_q, c_kv, k_rope, q_gain, kv_gain, w_uq, w_ukv):
    seq = c_q.shape[0]
    h = MLA_HEADS
    q = wmm(rmsnorm(c_q, q_gain), *w_uq).reshape(seq, h, MLA_NOPE + MLA_ROPE)
    kv = wmm(rmsnorm(c_kv, kv_gain), *w_ukv).reshape(seq, h, MLA_NOPE + MLA_V)
    q_nope, q_pe = q[..., :MLA_NOPE], q[..., MLA_NOPE:]
    k_nope, v = kv[..., :MLA_NOPE], kv[..., MLA_NOPE:]
    pos = jnp.arange(seq, dtype=F32)
    inv_freq = ROPE_THETA ** (-jnp.arange(0, MLA_ROPE, 2, dtype=F32) / MLA_ROPE)
    ang = pos[:, None] * inv_freq[None, :]
    cos, sin = jnp.cos(ang)[:, None, :], jnp.sin(ang)[:, None, :]
    q_pe = _rope(q_pe, cos, sin)
    k_pe = _rope(k_rope[:, None, :], cos, sin)
    qf = jnp.concatenate([q_nope, q_pe], axis=-1)
    kf = jnp.concatenate([k_nope, jnp.broadcast_to(k_pe, (seq, h, MLA_ROPE))], axis=-1)
    o = mla_attention(qf.transpose(1, 0, 2), kf.transpose(1, 0, 2), v.transpose(1, 0, 2))
    return o.transpose(1, 0, 2).reshape(seq, h * MLA_V)


def _layer(x, p, t5_bias):
    d = x.shape[1]
    h = rmsnorm(x, p["mix_pre_gain"])
    (w_small, w_gate), (w_small_bf16, w_gate_bf16) = (_split_w_in(t) for t in p["w_in"])
    ps = wmm(h, w_small, w_small_bf16)
    gate_logits = wmm(h, w_gate, w_gate_bf16)
    y_a = _s5_mixer(ps[:, 0:512], p["s5_lam_re"], p["s5_lam_im"], p["s5_log_step"], p["s5_b_re"], p["s5_b_im"],
                    p["s5_c_re"], p["s5_c_im"], p["s5_d"], p["s5_w_glu"], p["s5_b_glu"])
    y_b = _gdn_mixer(ps[:, 512:2048], ps[:, 2048:2560], ps[:, 4288:4296], ps[:, 4296:4304], p["gdn_conv"],
                     p["gdn_a_log"], p["gdn_dt_bias"], p["gdn_o_gain"])
    y_c = _swa_mixer(ps[:, 2560:3072], ps[:, 3072:3328], p["swa_sink"], t5_bias)
    y_d = _mla_mixer(ps[:, 3328:3712], ps[:, 3712:4224], ps[:, 4224:4288], p["mla_q_gain"], p["mla_kv_gain"],
                     p["mla_w_uq"], p["mla_w_ukv"])
    merged = sum(jax.nn.sigmoid(gate_logits[:, b * d:(b + 1) * d]) * wmm(y, p["w_branch"][0][b], p["w_branch"][1][b])
                 for b, y in enumerate((y_a, y_b, y_c, y_d)))
    x = x + rmsnorm(wmm(merged, *p["w_out"]), p["mix_post_gain"])
    h = rmsnorm(x, p["mlp_pre_gain"])
    f = wmm(jnp.square(jax.nn.relu(wmm(h, *p["w_mlp_in"]))), *p["w_mlp_out"])
    return x + rmsnorm(f, p["mlp_post_gain"])


LAYER_WEIGHTS = ("w_in", "s5_lam_re", "s5_lam_im", "s5_log_step", "s5_b_re", "s5_b_im", "s5_c_re", "s5_c_im", "s5_d",
                 "s5_w_glu", "s5_b_glu", "gdn_conv", "gdn_a_log", "gdn_dt_bias", "gdn_o_gain", "swa_sink", "mla_q_gain",
                 "mla_kv_gain", "mla_w_uq", "mla_w_ukv", "w_branch", "w_out", "mix_pre_gain", "mix_post_gain",
                 "mlp_pre_gain", "mlp_post_gain", "w_mlp_in", "w_mlp_out")


def _forward(x, weights, values):
    for layer in range(DEPTH):
        p = {n: (weights[n][layer], values[n][layer]) if n in values else weights[n][layer] for n in LAYER_WEIGHTS}
        x = _layer(x, p, weights["t5_bias"])
    return x


WEIGHT_NAMES = ("w_in", "s5_lam_re", "s5_lam_im", "s5_log_step", "s5_b_re", "s5_b_im", "s5_c_re", "s5_c_im", "s5_d",
                "s5_w_glu", "s5_b_glu", "gdn_conv", "gdn_a_log", "gdn_dt_bias", "gdn_o_gain", "swa_sink", "t5_bias",
                "mla_q_gain", "mla_kv_gain", "mla_w_uq", "mla_w_ukv", "w_branch", "w_out", "mix_pre_gain", "mix_post_gain",
                "mlp_pre_gain", "mlp_post_gain", "w_mlp_in", "w_mlp_out")
SHARD_AXIS = {"w_in": 2, "s5_w_glu": 1, "mla_w_uq": 2, "mla_w_ukv": 2, "w_branch": 3, "w_out": 1, "w_mlp_in": 2,
              "w_mlp_out": 1}
CONV = "gdn_conv"
CONV_AXIS = 2
REPLICATED = tuple(n for n in WEIGHT_NAMES if n not in SHARD_AXIS and n != CONV)


def _step(x, target, w, m, v):
    big = tuple(SHARD_AXIS)
    x_pos, y_pos, c_pos = _mesh_position()
    my = 4 * x_pos + 2 * y_pos + c_pos

    shard_shapes = [w[n].shape for n in big]
    gathered = _unpack(all_gather(_pack([w[n] for n in big], BF16), "gather_weights"), shard_shapes)
    values = {n: jnp.concatenate([g[k] for k in range(N_DEV)], axis=SHARD_AXIS[n]) for n, g in zip(big, gathered)}
    full = {n: t.astype(F32) for n, t in values.items()}
    conv_all = _unpack(all_gather(_pack([w[CONV]], F32), "gather_conv"), [w[CONV].shape])[0]
    full[CONV] = jnp.concatenate([conv_all[k] for k in range(N_DEV)], axis=CONV_AXIS)
    for n in REPLICATED:
        full[n] = w[n]

    y, vjp = jax.vjp(lambda x_, full_: _forward(x_, full_, values), x, full)
    loss_rows, dy = loss_head(y, target)
    grad_x, grad_full = vjp(dy)
    loss_part = jnp.sum(loss_rows)

    def pieces(n, k):
        size = w[n].shape[SHARD_AXIS[n]]
        return lax.slice_in_dim(grad_full[n], k * size, (k + 1) * size, axis=SHARD_AXIS[n])

    send = jnp.stack([_pack([pieces(n, k) for n in big], BF16) for k in range(N_DEV)])
    grad_big = sum_leading(all_to_all(send, "exchange_grads"))
    small_names = REPLICATED + (CONV,)
    small = [grad_full[n] for n in small_names] + [loss_part.reshape(1)]
    small_sum = sum_leading(all_gather(_pack(small, F32), "gather_small_grads"))
    small_grads = _unpack(small_sum, [a.shape for a in small])
    loss = small_grads[-1][0]
    grads = dict(zip(small_names, small_grads[:-1]))
    conv_size = w[CONV].shape[CONV_AXIS]
    grads[CONV] = lax.dynamic_slice_in_dim(grads[CONV], my * conv_size, conv_size, axis=CONV_AXIS)
    grads.update(zip(big, _unpack(grad_big, shard_shapes)))

    small_shapes = [w[n].shape for n in small_names]
    packed = lambda t: _pack([t[n] for n in small_names], F32)
    delta, new_m, new_v = (dict(zip(small_names, _unpack(t, small_shapes)))
                           for t in adamw_flat(packed(w), packed(grads), packed(m), packed(v)))
    for n in big:
        rows2d = lambda t: t.reshape(-1, t.shape[-1])
        outs = adamw_flat(rows2d(w[n]), rows2d(grads[n]), rows2d(m[n]), rows2d(v[n]))
        delta[n], new_m[n], new_v[n] = (t.reshape(w[n].shape) for t in outs)
    return loss, grad_x, grads, delta, new_m, new_v


def kernel(x, w_in, s5_lam_re, s5_lam_im, s5_log_step, s5_b_re, s5_b_im, s5_c_re, s5_c_im, s5_d, s5_w_glu, s5_b_glu, gdn_conv, gdn_a_log, gdn_dt_bias, gdn_o_gain, swa_sink, t5_bias, mla_q_gain, mla_kv_gain, mla_w_uq, mla_w_ukv, w_branch, w_out, mix_pre_gain, mix_post_gain, mlp_pre_gain, mlp_post_gain, w_mlp_in, w_mlp_out, loss_target, m_w_in, m_s5_lam_re, m_s5_lam_im, m_s5_log_step, m_s5_b_re, m_s5_b_im, m_s5_c_re, m_s5_c_im, m_s5_d, m_s5_w_glu, m_s5_b_glu, m_gdn_conv, m_gdn_a_log, m_gdn_dt_bias, m_gdn_o_gain, m_swa_sink, m_t5_bias, m_mla_q_gain, m_mla_kv_gain, m_mla_w_uq, m_mla_w_ukv, m_w_branch, m_w_out, m_mix_pre_gain, m_mix_post_gain, m_mlp_pre_gain, m_mlp_post_gain, m_w_mlp_in, m_w_mlp_out, v_w_in, v_s5_lam_re, v_s5_lam_im, v_s5_log_step, v_s5_b_re, v_s5_b_im, v_s5_c_re, v_s5_c_im, v_s5_d, v_s5_w_glu, v_s5_b_glu, v_gdn_conv, v_gdn_a_log, v_gdn_dt_bias, v_gdn_o_gain, v_swa_sink, v_t5_bias, v_mla_q_gain, v_mla_kv_gain, v_mla_w_uq, v_mla_w_ukv, v_w_branch, v_w_out, v_mix_pre_gain, v_mix_post_gain, v_mlp_pre_gain, v_mlp_post_gain, v_w_mlp_in, v_w_mlp_out):
    args = locals()
    w = {n: args[n] for n in WEIGHT_NAMES}
    m = {n: args["m_" + n] for n in WEIGHT_NAMES}
    v = {n: args["v_" + n] for n in WEIGHT_NAMES}
    loss, grad_x, grads, delta, new_m, new_v = _step(x[0], loss_target[0], w, m, v)
    return (loss, grad_x[None], *[grads[n] for n in WEIGHT_NAMES], *[delta[n] for n in WEIGHT_NAMES],
            *[new_m[n] for n in WEIGHT_NAMES], *[new_v[n] for n in WEIGHT_NAMES])
```

```python
import functools
import math

import jax
import jax.numpy as jnp
import numpy as np
from jax import lax
from jax.experimental import pallas as pl
from jax.experimental.pallas import tpu as pltpu

F32 = jnp.float32
BF16 = jnp.bfloat16

VMEM_LIMIT_BYTES = 56 * 1024 * 1024
LANES = 128
SUBLANES = 8

N_DEV = 8
MESH_AXES = ("x", "y", "c")

DEPTH = 4
N_BRANCHES = 4
BRANCH_WIDTH = 512
NORM_EPS = 1e-6
S5_GROUP = 16
S5_GROUPS = 32
S5_STATE = 64
S5_WIDTH = S5_GROUPS * S5_STATE
GDN_HEAD_DIM = 128
GDN_HEADS = 4
GDN_CONV = 4
GDN_CHUNK = 64
SWA_HEAD_DIM = 64
SWA_HEADS = 8
SWA_KV_HEADS = 2
SWA_GROUP = SWA_HEADS // SWA_KV_HEADS
WINDOW = 128
SWA_BLOCK = 128
T5_BUCKETS = 32
T5_MAX_DISTANCE = 128
MLA_HEADS = 4
MLA_Q_RANK = 384
MLA_KV_RANK = 512
MLA_NOPE = 128
MLA_ROPE = 64
MLA_V = 128
ROPE_THETA = 10000.0

ADAM_LR = 0.001
ADAM_B1 = 0.9
ADAM_B2 = 0.999
ADAM_EPS = 1e-08
ADAM_WD = 0.01
ADAM_STEP = 10

NEG_BIG = -1e30


def _cparams(*sem):
    return pltpu.CompilerParams(dimension_semantics=sem if sem else None, vmem_limit_bytes=VMEM_LIMIT_BYTES)


def _pick(n, pref, unit):
    if n <= pref:
        return n
    t = (pref // unit) * unit
    while t >= unit:
        if n % t == 0:
            return t
        t -= unit
    return n


def _bdot(a, b, dims):
    return lax.dot_general(a.astype(BF16), b.astype(BF16), (dims, ((), ())), preferred_element_type=F32)


def _dot_nn(a, b):
    return _bdot(a, b, ((1,), (0,)))


def _dot_nt(a, b):
    return _bdot(a, b, ((1,), (1,)))


def _dot_tn(a, b):
    return _bdot(a, b, ((0,), (0,)))


def _mm_body(a_ref, b_ref, o_ref, *, ta, tb, nk):
    dims = ((0,) if ta else (1,), (1,) if tb else (0,))
    part = _bdot(a_ref[...], b_ref[...], dims)
    if nk == 1:
        o_ref[...] = part
    else:
        k = pl.program_id(2)

        @pl.when(k == 0)
        def _():
            o_ref[...] = part

        @pl.when(k > 0)
        def _():
            o_ref[...] += part


MM_BLOCK_BYTES = 32 * 1024 * 1024


def _mm_tiles(m, n, k, a_bytes, b_bytes):
    tm, tn = _pick(m, 1024, LANES), _pick(n, 1024, LANES)
    tk = LANES if k % LANES == 0 else k
    for cand in range(k, 0, -LANES) if k % LANES == 0 else (k,):
        if k % cand == 0 and 2 * cand * (tm * a_bytes + tn * b_bytes) + 2 * tm * tn * 4 <= MM_BLOCK_BYTES:
            tk = cand
            break
    return tm, tn, tk


def _mm_call(a, b, *, ta=False, tb=False, name):
    m, k = (a.shape[1], a.shape[0]) if ta else a.shape
    n = b.shape[0] if tb else b.shape[1]
    assert (b.shape[1] if tb else b.shape[0]) == k, (a.shape, b.shape, ta, tb)
    tm, tn, tk = _mm_tiles(m, n, k, a.dtype.itemsize, b.dtype.itemsize)
    nk = k // tk
    a_spec = pl.BlockSpec((tk, tm), lambda i, j, kk: (kk, i)) if ta else pl.BlockSpec((tm, tk), lambda i, j, kk: (i, kk))
    b_spec = pl.BlockSpec((tn, tk), lambda i, j, kk: (j, kk)) if tb else pl.BlockSpec((tk, tn), lambda i, j, kk: (kk, j))
    return pl.pallas_call(
        functools.partial(_mm_body, ta=ta, tb=tb, nk=nk),
        name=name,
        grid=(m // tm, n // tn, nk),
        in_specs=[a_spec, b_spec],
        out_specs=pl.BlockSpec((tm, tn), lambda i, j, kk: (i, j)),
        out_shape=jax.ShapeDtypeStruct((m, n), F32),
        compiler_params=_cparams("parallel", "parallel", "arbitrary"),
    )(a, b)


@jax.custom_vjp
def mm(a, b):
    return _mm_call(a, b, name="mm_fwd")


def _mm_fwd(a, b):
    return _mm_call(a, b, name="mm_fwd"), (a, b)


def _mm_bwd(res, g):
    a, b = res
    return _mm_call(g, b, tb=True, name="mm_da"), _mm_call(a, g, ta=True, name="mm_db")


mm.defvjp(_mm_fwd, _mm_bwd)


@jax.custom_vjp
def wmm(a, w, w_bf16):
    return _mm_call(a, w_bf16, name="wmm_fwd")


def _wmm_fwd(a, w, w_bf16):
    return _mm_call(a, w_bf16, name="wmm_fwd"), (a, w_bf16)


def _wmm_bwd(res, g):
    a, w_bf16 = res
    return _mm_call(g, w_bf16, tb=True, name="wmm_da"), _mm_call(a, g, ta=True, name="wmm_db"), jnp.zeros_like(w_bf16)


wmm.defvjp(_wmm_fwd, _wmm_bwd)


def _rms_fwd_body(x_ref, g_ref, y_ref):
    x = x_ref[...]
    r = lax.rsqrt(jnp.mean(x * x, axis=-1, keepdims=True) + NORM_EPS)
    y_ref[...] = x * r * g_ref[...]


def _rms_bwd_body(x_ref, g_ref, dy_ref, dx_ref, dg_ref, *, tr):
    i = pl.program_id(0)

    @pl.when(i == 0)
    def _():
        dg_ref[...] = jnp.zeros_like(dg_ref)

    x = x_ref[...]
    dy = dy_ref[...]
    r = lax.rsqrt(jnp.mean(x * x, axis=-1, keepdims=True) + NORM_EPS)
    xhat = x * r
    gy = dy * g_ref[...]
    dx_ref[...] = r * (gy - xhat * jnp.mean(gy * xhat, axis=-1, keepdims=True))
    dg_ref[...] += jnp.sum((dy * xhat).reshape(tr // SUBLANES, SUBLANES, x.shape[-1]), axis=0)


def _rms_rows(rows, cols):
    return _pick(rows, max(SUBLANES, (2 * 1024 * 1024) // (4 * cols)), SUBLANES)


def _rms_fwd_call(x, gain):
    rows, cols = x.shape
    tr = _rms_rows(rows, cols)
    return pl.pallas_call(
        functools.partial(_rms_fwd_body),
        name="rms_fwd",
        grid=(rows // tr,),
        in_specs=[pl.BlockSpec((tr, cols), lambda i: (i, 0)), pl.BlockSpec((1, cols), lambda i: (0, 0))],
        out_specs=pl.BlockSpec((tr, cols), lambda i: (i, 0)),
        out_shape=jax.ShapeDtypeStruct((rows, cols), F32),
        compiler_params=_cparams("parallel"),
    )(x, gain.reshape(1, cols))


def _rms_bwd_call(x, gain, dy):
    rows, cols = x.shape
    tr = _rms_rows(rows, cols)
    dx, dg = pl.pallas_call(
        functools.partial(_rms_bwd_body, tr=tr),
        name="rms_bwd",
        grid=(rows // tr,),
        in_specs=[
            pl.BlockSpec((tr, cols), lambda i: (i, 0)),
            pl.BlockSpec((1, cols), lambda i: (0, 0)),
            pl.BlockSpec((tr, cols), lambda i: (i, 0)),
        ],
        out_specs=[pl.BlockSpec((tr, cols), lambda i: (i, 0)), pl.BlockSpec((SUBLANES, cols), lambda i: (0, 0))],
        out_shape=[jax.ShapeDtypeStruct((rows, cols), F32), jax.ShapeDtypeStruct((SUBLANES, cols), F32)],
        compiler_params=_cparams("arbitrary"),
    )(x, gain.reshape(1, cols), dy)
    return dx, jnp.sum(dg, axis=0)


@jax.custom_vjp
def rmsnorm(x, gain):
    return _rms_fwd_call(x, gain)


def _rmsnorm_fwd(x, gain):
    return _rms_fwd_call(x, gain), (x, gain)


def _rmsnorm_bwd(res, dy):
    x, gain = res
    return _rms_bwd_call(x, gain, dy)


rmsnorm.defvjp(_rmsnorm_fwd, _rmsnorm_bwd)


def _loss_body(y_ref, t_ref, rows_ref, dy_ref):
    d = y_ref[...] - t_ref[...]
    rows_ref[...] = 0.5 * jnp.mean(d * d, axis=-1, keepdims=True)
    dy_ref[...] = d * (1.0 / d.shape[-1])


def loss_head(y, target):
    rows, cols = y.shape
    tr = _rms_rows(rows, cols)
    return pl.pallas_call(
        functools.partial(_loss_body),
        name="loss_head",
        grid=(rows // tr,),
        in_specs=[pl.BlockSpec((tr, cols), lambda i: (i, 0))] * 2,
        out_specs=[pl.BlockSpec((tr, 1), lambda i: (i, 0)), pl.BlockSpec((tr, cols), lambda i: (i, 0))],
        out_shape=[jax.ShapeDtypeStruct((rows, 1), F32), jax.ShapeDtypeStruct((rows, cols), F32)],
        compiler_params=_cparams("parallel"),
    )(y, target)


def _adamw_body(w_ref, g_ref, m_ref, v_ref, d_ref, nm_ref, nv_ref):
    g = g_ref[...]
    m = ADAM_B1 * m_ref[...] + (1.0 - ADAM_B1) * g
    v = ADAM_B2 * v_ref[...] + (1.0 - ADAM_B2) * (g * g)
    m_hat = m / (1.0 - ADAM_B1**ADAM_STEP)
    v_hat = v / (1.0 - ADAM_B2**ADAM_STEP)
    d_ref[...] = -ADAM_LR * (m_hat / (jnp.sqrt(v_hat) + ADAM_EPS) + ADAM_WD * w_ref[...])
    nm_ref[...] = m
    nv_ref[...] = v


def adamw_flat(w, g, m, v):
    rows, cols = w.shape
    tr = _rms_rows(rows, cols)
    spec = pl.BlockSpec((tr, cols), lambda i: (i, 0))
    return pl.pallas_call(
        functools.partial(_adamw_body),
        name="adamw",
        grid=(rows // tr,),
        in_specs=[spec] * 4,
        out_specs=[spec] * 3,
        out_shape=[jax.ShapeDtypeStruct((rows, cols), F32)] * 3,
        compiler_params=_cparams("parallel"),
    )(w, g, m, v)


def _sum_body(x_ref, o_ref, *, n):
    acc = x_ref[0].astype(F32)
    for k in range(1, n):
        acc = acc + x_ref[k].astype(F32)
    o_ref[...] = acc


def sum_leading(x):
    n, rows, cols = x.shape
    tr = _pick(rows, 256, 2 * SUBLANES)
    return pl.pallas_call(
        functools.partial(_sum_body, n=n),
        name="sum_leading",
        grid=(rows // tr,),
        in_specs=[pl.BlockSpec((n, tr, cols), lambda i: (0, i, 0))],
        out_specs=pl.BlockSpec((tr, cols), lambda i: (i, 0)),
        out_shape=jax.ShapeDtypeStruct((rows, cols), F32),
        compiler_params=_cparams("parallel"),
    )(x)


def _mla_fwd_body(q_ref, k_ref, v_ref, o_ref, lse_ref, m_sc, l_sc, acc_sc, *, scale, nk):
    ki = pl.program_id(2)

    @pl.when(ki == 0)
    def _():
        m_sc[...] = jnp.full_like(m_sc, NEG_BIG)
        l_sc[...] = jnp.zeros_like(l_sc)
        acc_sc[...] = jnp.zeros_like(acc_sc)

    s = _dot_nt(q_ref[...], k_ref[...]) * scale
    m_prev = m_sc[...]
    m_new = jnp.maximum(m_prev, jnp.max(s, axis=1, keepdims=True))
    alpha = jnp.exp(m_prev - m_new)
    p = jnp.exp(s - m_new)
    l_sc[...] = alpha * l_sc[...] + jnp.sum(p, axis=1, keepdims=True)
    acc_sc[...] = alpha * acc_sc[...] + _dot_nn(p, v_ref[...])
    m_sc[...] = m_new

    @pl.when(ki == nk - 1)
    def _():
        o_ref[...] = acc_sc[...] / l_sc[...]
        lse_ref[...] = m_sc[...] + jnp.log(l_sc[...])


def _mla_fwd_call(q, k, v):
    h, seq, dq = q.shape
    dv = v.shape[-1]
    tq, tk = _pick(seq, 1024, LANES), _pick(seq, 1024, LANES)
    nk = seq // tk
    scale = dq**-0.5
    return pl.pallas_call(
        functools.partial(_mla_fwd_body, scale=scale, nk=nk),
        name="mla_fwd",
        grid=(h, seq // tq, nk),
        in_specs=[
            pl.BlockSpec((None, tq, dq), lambda hh, i, j: (hh, i, 0)),
            pl.BlockSpec((None, tk, dq), lambda hh, i, j: (hh, j, 0)),
            pl.BlockSpec((None, tk, dv), lambda hh, i, j: (hh, j, 0)),
        ],
        out_specs=[
            pl.BlockSpec((None, tq, dv), lambda hh, i, j: (hh, i, 0)),
            pl.BlockSpec((None, tq, 1), lambda hh, i, j: (hh, i, 0)),
        ],
        out_shape=[jax.ShapeDtypeStruct((h, seq, dv), F32), jax.ShapeDtypeStruct((h, seq, 1), F32)],
        scratch_shapes=[pltpu.VMEM((tq, 1), F32), pltpu.VMEM((tq, 1), F32), pltpu.VMEM((tq, dv), F32)],
        compiler_params=_cparams("parallel", "parallel", "arbitrary"),
    )(q, k, v)


def _mla_bwd_body(q_ref, k_ref, v_ref, do_ref, lse_ref, dl_ref, dq_ref, dk_ref, dv_ref, dk_sc, dv_sc, *, scale, nq, tq):
    ki = pl.program_id(1)
    qi = pl.program_id(2)

    @pl.when(jnp.logical_and(ki == 0, qi == 0))
    def _():
        dq_ref[...] = jnp.zeros_like(dq_ref)

    @pl.when(qi == 0)
    def _():
        dk_sc[...] = jnp.zeros_like(dk_sc)
        dv_sc[...] = jnp.zeros_like(dv_sc)

    q = q_ref[...]
    k = k_ref[...]
    do = do_ref[...]
    p = jnp.exp(_dot_nt(q, k) * scale - lse_ref[...])
    dv_sc[...] += _dot_tn(p, do)
    ds = p * (_dot_nt(do, v_ref[...]) - dl_ref[...]) * scale
    dk_sc[...] += _dot_tn(ds, q)
    rows = pl.ds(pl.multiple_of(qi * tq, tq), tq)
    dq_ref[rows, :] += _dot_nn(ds, k)

    @pl.when(qi == nq - 1)
    def _():
        dk_ref[...] = dk_sc[...]
        dv_ref[...] = dv_sc[...]


def _mla_bwd_call(q, k, v, do, lse, delta):
    h, seq, dq = q.shape
    dv = v.shape[-1]
    tq, tk = _pick(seq, 512, LANES), _pick(seq, 1024, LANES)
    nq = seq // tq
    scale = dq**-0.5
    return pl.pallas_call(
        functools.partial(_mla_bwd_body, scale=scale, nq=nq, tq=tq),
        name="mla_bwd",
        grid=(h, seq // tk, nq),
        in_specs=[
            pl.BlockSpec((None, tq, dq), lambda hh, j, i: (hh, i, 0)),
            pl.BlockSpec((None, tk, dq), lambda hh, j, i: (hh, j, 0)),
            pl.BlockSpec((None, tk, dv), lambda hh, j, i: (hh, j, 0)),
            pl.BlockSpec((None, tq, dv), lambda hh, j, i: (hh, i, 0)),
            pl.BlockSpec((None, tq, 1), lambda hh, j, i: (hh, i, 0)),
            pl.BlockSpec((None, tq, 1), lambda hh, j, i: (hh, i, 0)),
        ],
        out_specs=[
            pl.BlockSpec((None, seq, dq), lambda hh, j, i: (hh, 0, 0)),
            pl.BlockSpec((None, tk, dq), lambda hh, j, i: (hh, j, 0)),
            pl.BlockSpec((None, tk, dv), lambda hh, j, i: (hh, j, 0)),
        ],
        out_shape=[
            jax.ShapeDtypeStruct((h, seq, dq), F32),
            jax.ShapeDtypeStruct((h, seq, dq), F32),
            jax.ShapeDtypeStruct((h, seq, dv), F32),
        ],
        scratch_shapes=[pltpu.VMEM((tk, dq), F32), pltpu.VMEM((tk, dv), F32)],
        compiler_params=_cparams("parallel", "arbitrary", "arbitrary"),
    )(q, k, v, do, lse, delta)


@jax.custom_vjp
def mla_attention(q, k, v):
    return _mla_fwd_call(q, k, v)[0]


def _mla_attention_fwd(q, k, v):
    o, lse = _mla_fwd_call(q, k, v)
    return o, (q, k, v, o, lse)


def _mla_attention_bwd(res, do):
    q, k, v, o, lse = res
    delta = jnp.sum(do * o, axis=-1, keepdims=True)
    return tuple(_mla_bwd_call(q, k, v, do, lse, delta))


mla_attention.defvjp(_mla_attention_fwd, _mla_attention_bwd)


def _swa_block(q4, kp, kc, kn, vp, vc, vn, bias, sink, *, valid):
    kb = jnp.concatenate([kp, kc, kn], axis=0)
    vb = jnp.concatenate([vp, vc, vn], axis=0)
    s = _dot_nt(q4, kb) * (SWA_HEAD_DIM**-0.5) + bias
    s = jnp.where(valid, s, NEG_BIG)
    m = lax.stop_gradient(jnp.maximum(jnp.max(s, axis=1, keepdims=True), sink))
    p = jnp.exp(s - m)
    denom = jnp.sum(p, axis=1, keepdims=True) + jnp.exp(sink - m)
    return _dot_nn(p / denom, vb)


def _swa_valid(n, seq):
    rows = SWA_GROUP * SWA_BLOCK
    qi = lax.broadcasted_iota(jnp.int32, (rows, 3 * SWA_BLOCK), 0) % SWA_BLOCK
    sj = lax.broadcasted_iota(jnp.int32, (rows, 3 * SWA_BLOCK), 1)
    rel = sj - SWA_BLOCK - qi
    kpos = n * SWA_BLOCK + sj - SWA_BLOCK
    return (jnp.abs(rel) <= WINDOW) & (kpos >= 0) & (kpos < seq)


def _swa_operands(q_ref, kp_ref, kc_ref, kn_ref, vp_ref, vc_ref, vn_ref, b_ref, s_ref):
    q4 = q_ref[...].reshape(SWA_GROUP * SWA_BLOCK, SWA_HEAD_DIM)
    return (q4, kp_ref[...], kc_ref[...], kn_ref[...], vp_ref[...], vc_ref[...], vn_ref[...], b_ref[...], s_ref[...])


def _swa_fwd_body(q_ref, kp_ref, kc_ref, kn_ref, vp_ref, vc_ref, vn_ref, b_ref, s_ref, o_ref, *, seq):
    valid = _swa_valid(pl.program_id(1), seq)
    out = _swa_block(*_swa_operands(q_ref, kp_ref, kc_ref, kn_ref, vp_ref, vc_ref, vn_ref, b_ref, s_ref), valid=valid)
    o_ref[...] = out.reshape(SWA_GROUP, SWA_BLOCK, SWA_HEAD_DIM)


def _swa_bwd_body(q_ref, kp_ref, kc_ref, kn_ref, vp_ref, vc_ref, vn_ref, b_ref, s_ref, do_ref,
                  dq_ref, dk_ref, dv_ref, db_ref, ds_ref, *, seq):
    n = pl.program_id(1)

    @pl.when(n == 0)
    def _():
        db_ref[...] = jnp.zeros_like(db_ref)
        ds_ref[...] = jnp.zeros_like(ds_ref)

    valid = _swa_valid(n, seq)
    ops = _swa_operands(q_ref, kp_ref, kc_ref, kn_ref, vp_ref, vc_ref, vn_ref, b_ref, s_ref)
    _, vjp = jax.vjp(functools.partial(_swa_block, valid=valid), *ops)
    do = do_ref[...].reshape(SWA_GROUP * SWA_BLOCK, SWA_HEAD_DIM)
    dq4, dkp, dkc, dkn, dvp, dvc, dvn, dbias, dsink = vjp(do)
    dq_ref[...] = dq4.reshape(SWA_GROUP, SWA_BLOCK, SWA_HEAD_DIM)
    dk_ref[0] = dkp
    dk_ref[1] = dkc
    dk_ref[2] = dkn
    dv_ref[0] = dvp
    dv_ref[1] = dvc
    dv_ref[2] = dvn
    db_ref[...] += dbias
    ds_ref[...] += dsink


def _swa_in_specs(nb):
    blk = (None, SWA_BLOCK, SWA_HEAD_DIM)
    prev = lambda h, n: (h, jnp.maximum(n - 1, 0), 0)
    own = lambda h, n: (h, n, 0)
    nxt = lambda h, n: (h, jnp.minimum(n + 1, nb - 1), 0)
    rows = SWA_GROUP * SWA_BLOCK
    return [
        pl.BlockSpec((None, SWA_GROUP, SWA_BLOCK, SWA_HEAD_DIM), lambda h, n: (h, 0, n, 0)),
        pl.BlockSpec(blk, prev), pl.BlockSpec(blk, own), pl.BlockSpec(blk, nxt),
        pl.BlockSpec(blk, prev), pl.BlockSpec(blk, own), pl.BlockSpec(blk, nxt),
        pl.BlockSpec((None, rows, 3 * SWA_BLOCK), lambda h, n: (h, 0, 0)),
        pl.BlockSpec((None, rows, 1), lambda h, n: (h, 0, 0)),
    ]


def _swa_fwd_call(q, k, v, bias, sink):
    kv, g, seq, d = q.shape
    nb = seq // SWA_BLOCK
    return pl.pallas_call(
        functools.partial(_swa_fwd_body, seq=seq),
        name="swa_fwd",
        grid=(kv, nb),
        in_specs=_swa_in_specs(nb),
        out_specs=pl.BlockSpec((None, g, SWA_BLOCK, d), lambda h, n: (h, 0, n, 0)),
        out_shape=jax.ShapeDtypeStruct(q.shape, F32),
        compiler_params=_cparams("parallel", "parallel"),
    )(q, k, k, k, v, v, v, bias, sink)


def _swa_bwd_call(q, k, v, bias, sink, do):
    kv, g, seq, d = q.shape
    nb = seq // SWA_BLOCK
    rows = g * SWA_BLOCK
    part = jax.ShapeDtypeStruct((kv, nb, 3, SWA_BLOCK, d), F32)
    part_spec = pl.BlockSpec((None, None, 3, SWA_BLOCK, d), lambda h, n: (h, n, 0, 0, 0))
    dq, dkp, dvp, dbias, dsink = pl.pallas_call(
        functools.partial(_swa_bwd_body, seq=seq),
        name="swa_bwd",
        grid=(kv, nb),
        in_specs=_swa_in_specs(nb) + [pl.BlockSpec((None, g, SWA_BLOCK, d), lambda h, n: (h, 0, n, 0))],
        out_specs=[
            pl.BlockSpec((None, g, SWA_BLOCK, d), lambda h, n: (h, 0, n, 0)),
            part_spec, part_spec,
            pl.BlockSpec((None, rows, 3 * SWA_BLOCK), lambda h, n: (h, 0, 0)),
            pl.BlockSpec((None, rows, 1), lambda h, n: (h, 0, 0)),
        ],
        out_shape=[jax.ShapeDtypeStruct(q.shape, F32), part, part,
                   jax.ShapeDtypeStruct(bias.shape, F32), jax.ShapeDtypeStruct(sink.shape, F32)],
        compiler_params=_cparams("parallel", "arbitrary"),
    )(q, k, k, k, v, v, v, bias, sink, do)

    def fold(p):
        zero = jnp.zeros_like(p[:, :1, 0])
        total = p[:, :, 1] + jnp.concatenate([p[:, 1:, 0], zero], axis=1) + jnp.concatenate([zero, p[:, :-1, 2]], axis=1)
        return total.reshape(kv, seq, d)

    return dq, fold(dkp), fold(dvp), dbias, dsink


@jax.custom_vjp
def swa_attention(q, k, v, bias, sink):
    return _swa_fwd_call(q, k, v, bias, sink)


def _swa_attention_fwd(q, k, v, bias, sink):
    return _swa_fwd_call(q, k, v, bias, sink), (q, k, v, bias, sink)


def _swa_attention_bwd(res, do):
    return _swa_bwd_call(*res, do)


swa_attention.defvjp(_swa_attention_fwd, _swa_attention_bwd)


def _scan_tiles(n_tiles, reverse, tile_fn, init):
    def step(i, carry):
        ti = (n_tiles - 1 - i) if reverse else i
        return tile_fn(pl.multiple_of(ti * SUBLANES, SUBLANES), carry)

    return lax.fori_loop(0, n_tiles, step, init)


def _row_order(reverse):
    return tuple(reversed(range(SUBLANES))) if reverse else tuple(range(SUBLANES))


def _s5_fwd_dir(a_ref, bu_ref, s_ref, carry, *, reverse, tt, w):
    ar, ai = a_ref[0:1, :], a_ref[1:2, :]
    rowid = lax.broadcasted_iota(jnp.int32, (SUBLANES, w), 0)

    def tile(base, c):
        sr, si = c
        x = bu_ref[pl.ds(base, SUBLANES), :]
        xr, xi = x[:, :w], x[:, w:]
        out_r = jnp.zeros((SUBLANES, w), F32)
        out_i = jnp.zeros((SUBLANES, w), F32)
        for j in _row_order(reverse):
            nr = ar * sr - ai * si + xr[j:j + 1, :]
            ni = ar * si + ai * sr + xi[j:j + 1, :]
            out_r = jnp.where(rowid == j, nr, out_r)
            out_i = jnp.where(rowid == j, ni, out_i)
            sr, si = nr, ni
        s_ref[pl.ds(base, SUBLANES), :] = jnp.concatenate([out_r, out_i], axis=1)
        return sr, si

    sr, si = _scan_tiles(tt // SUBLANES, reverse, tile, (carry[0:1, :], carry[1:2, :]))
    carry[0:1, :] = sr
    carry[1:2, :] = si


def _s5_fwd_body(a_ref, bu_ref, s_ref, carry, *, tt, w):
    d = pl.program_id(0)

    @pl.when(pl.program_id(1) == 0)
    def _():
        carry[...] = jnp.zeros_like(carry)

    @pl.when(d == 0)
    def _():
        _s5_fwd_dir(a_ref, bu_ref, s_ref, carry, reverse=False, tt=tt, w=w)

    @pl.when(d == 1)
    def _():
        _s5_fwd_dir(a_ref, bu_ref, s_ref, carry, reverse=True, tt=tt, w=w)


def _s5_bwd_dir(a_ref, s_ref, ds_ref, dbu_ref, da_ref, carry, acc, *, reverse, tt, w):
    ar, ai = a_ref[0:1, :], a_ref[1:2, :]
    rowid = lax.broadcasted_iota(jnp.int32, (SUBLANES, w), 0)

    first = _row_order(reverse)[0]
    acc[...] = jnp.zeros_like(acc)

    def tile(base, c):
        lr, li = lam_in_r, lam_in_i = c
        s = s_ref[pl.ds(base, SUBLANES), :]
        g = ds_ref[pl.ds(base, SUBLANES), :]
        out_r = jnp.zeros((SUBLANES, w), F32)
        out_i = jnp.zeros((SUBLANES, w), F32)
        for j in _row_order(reverse):
            nr = g[j:j + 1, :w] + ar * lr + ai * li
            ni = g[j:j + 1, w:] + ar * li - ai * lr
            out_r = jnp.where(rowid == j, nr, out_r)
            out_i = jnp.where(rowid == j, ni, out_i)
            lr, li = nr, ni
        dbu_ref[pl.ds(base, SUBLANES), :] = jnp.concatenate([out_r, out_i], axis=1)
        shift = SUBLANES - 1 if reverse else 1
        next_r = jnp.where(rowid == first, lam_in_r, pltpu.roll(out_r, shift, axis=0))
        next_i = jnp.where(rowid == first, lam_in_i, pltpu.roll(out_i, shift, axis=0))
        sr, si = s[:, :w], s[:, w:]
        acc[0] += sr * next_r + si * next_i
        acc[1] += sr * next_i - si * next_r
        return lr, li

    lr, li = _scan_tiles(tt // SUBLANES, reverse, tile, (carry[0:1, :], carry[1:2, :]))
    carry[0:1, :] = lr
    carry[1:2, :] = li
    da_ref[0:1, :] += jnp.sum(acc[0], axis=0, keepdims=True)
    da_ref[1:2, :] += jnp.sum(acc[1], axis=0, keepdims=True)


def _s5_bwd_body(a_ref, s_ref, ds_ref, dbu_ref, da_ref, carry, acc, *, tt, w):
    d = pl.program_id(0)

    @pl.when(pl.program_id(1) == 0)
    def _():
        carry[...] = jnp.zeros_like(carry)
        da_ref[...] = jnp.zeros_like(da_ref)

    @pl.when(d == 0)
    def _():
        _s5_bwd_dir(a_ref, s_ref, ds_ref, dbu_ref, da_ref, carry, acc, reverse=True, tt=tt, w=w)

    @pl.when(d == 1)
    def _():
        _s5_bwd_dir(a_ref, s_ref, ds_ref, dbu_ref, da_ref, carry, acc, reverse=False, tt=tt, w=w)


def _s5_time_map(nt, flip_dir):
    def time_block(d, t):
        back = nt - 1 - t
        return jnp.where(d == flip_dir, back, t)

    return time_block


def _s5_fwd_call(a, bu):
    seq, w4 = bu.shape
    w = w4 // 4
    tt = _pick(seq, 256, SUBLANES)
    nt = seq // tt
    tb = _s5_time_map(nt, 1)
    return pl.pallas_call(
        functools.partial(_s5_fwd_body, tt=tt, w=w),
        name="s5_scan_fwd",
        grid=(2, nt),
        in_specs=[
            pl.BlockSpec((None, 2, w), lambda d, t: (d, 0, 0)),
            pl.BlockSpec((tt, 2 * w), lambda d, t: (tb(d, t), d)),
        ],
        out_specs=pl.BlockSpec((tt, 2 * w), lambda d, t: (tb(d, t), d)),
        out_shape=jax.ShapeDtypeStruct(bu.shape, F32),
        scratch_shapes=[pltpu.VMEM((2, w), F32)],
        compiler_params=_cparams("parallel", "arbitrary"),
    )(a, bu)


def _s5_bwd_call(a, s_prev, ds):
    seq, w4 = ds.shape
    w = w4 // 4
    tt = _pick(seq, 256, SUBLANES)
    nt = seq // tt
    tb = _s5_time_map(nt, 0)
    blk = pl.BlockSpec((tt, 2 * w), lambda d, t: (tb(d, t), d))
    return pl.pallas_call(
        functools.partial(_s5_bwd_body, tt=tt, w=w),
        name="s5_scan_bwd",
        grid=(2, nt),
        in_specs=[pl.BlockSpec((None, 2, w), lambda d, t: (d, 0, 0)), blk, blk],
        out_specs=[blk, pl.BlockSpec((None, 2, w), lambda d, t: (d, 0, 0))],
        out_shape=[jax.ShapeDtypeStruct(ds.shape, F32), jax.ShapeDtypeStruct(a.shape, F32)],
        scratch_shapes=[pltpu.VMEM((2, w), F32), pltpu.VMEM((2, SUBLANES, w), F32)],
        compiler_params=_cparams("parallel", "arbitrary"),
    )(a, s_prev, ds)


@jax.custom_vjp
def s5_scan(a, bu):
    return _s5_fwd_call(a, bu)


def _s5_scan_fwd(a, bu):
    s = _s5_fwd_call(a, bu)
    return s, (a, s)


def _s5_scan_bwd(res, ds):
    a, s = res
    dbu, da = _s5_bwd_call(a, s, ds)
    return da, dbu


s5_scan.defvjp(_s5_scan_fwd, _s5_scan_bwd)


NN, NT, TN = ((1,), (0,)), ((1,), (1,)), ((0,), (0,))


def _bmm(a, b, dims):
    return jnp.stack([_bdot(a[i], b[i], dims) for i in range(a.shape[0])])


def _bmm3(a, b):
    ah = a.astype(BF16)
    bh = b.astype(BF16)
    al = a - ah.astype(F32)
    bl = b - bh.astype(F32)
    return _bmm(ah, bh, NN) + _bmm(ah, bl, NN) + _bmm(al, bh, NN)


GDN_INV_BASE = 8


def _unit_triangular_inverse(a, ri, ci):
    c = a.shape[-1]

    def same_block(size):
        shift = int(math.log2(size))
        return lax.shift_right_logical(ri, shift) == lax.shift_right_logical(ci, shift)

    diag = jnp.where(same_block(GDN_INV_BASE), a, 0.0)
    inv = jnp.where(ri == ci, 1.0, 0.0) - diag
    power = diag
    for _ in range(int(math.log2(GDN_INV_BASE)) - 1):
        power = _bmm3(power, power)
        inv = inv + _bmm3(inv, power)
    size = GDN_INV_BASE
    while size < c:
        off = jnp.where(jnp.logical_and(same_block(2 * size), jnp.logical_not(same_block(size))), a, 0.0)
        inv = inv - _bmm3(_bmm3(inv, off), inv)
        size *= 2
    return inv


def _gdn_chunks(q, k, v, gr, br, state, *, n_fwd):
    b, c, _ = q.shape
    dv = v.shape[2]
    ri = lax.broadcasted_iota(jnp.int32, (b, c, c), 1)
    ci = lax.broadcasted_iota(jnp.int32, (b, c, c), 2)
    fwd = jnp.stack([jnp.full((c, c), 1 if i < n_fwd else 0, jnp.int32) for i in range(b)]) == 1
    lower = jnp.logical_or(jnp.logical_and(fwd, ri >= ci), jnp.logical_and(jnp.logical_not(fwd), ri <= ci))
    strict = jnp.logical_and(lower, ri != ci)
    as_column = lambda row: jnp.sum(jnp.where(ri == ci, row, 0.0), axis=2, keepdims=True)
    gc, bc = as_column(gr), as_column(br)
    kb = k * bc
    decay = jnp.where(lower, jnp.exp(jnp.where(lower, gc - gr, 0.0)), 0.0)
    a = jnp.where(strict, _bmm(kb, k, NT) * decay, 0.0)
    inv = _unit_triangular_inverse(a, ri, ci)
    eg = jnp.exp(gc)
    sol = _bmm3(inv, jnp.concatenate([v * bc, kb * eg], axis=2))
    u, w = sol[:, :, :dv], sol[:, :, dv:]
    attn = _bmm(q, k, NT) * decay
    v_new = u - _bmm(w, state, NN)
    o = _bmm(q * eg, state, NN) + _bmm(attn, v_new, NN)
    row = lax.broadcasted_iota(jnp.int32, (b, c, 1), 1)
    fwd_col = jnp.stack([jnp.full((c, 1), 1 if i < n_fwd else 0, jnp.int32) for i in range(b)]) == 1
    last = jnp.logical_or(jnp.logical_and(fwd_col, row == c - 1), jnp.logical_and(jnp.logical_not(fwd_col), row == 0))
    g_last = jnp.sum(jnp.where(last, gc, 0.0), axis=1, keepdims=True)
    new_state = state * jnp.exp(g_last) + _bmm(k * jnp.exp(g_last - gc), v_new, TN)
    return o, new_state


def _gdn_heads(refs, heads, dh):
    return jnp.stack([r[:, h * dh:(h + 1) * dh] for r in refs for h in range(heads)])


def _gdn_operands(q_refs, k_refs, v_refs, gr_refs, br_refs, heads, dh):
    rows = lambda refs: jnp.concatenate([r[...] for r in refs], axis=0)
    return (_gdn_heads(q_refs, heads, dh), _gdn_heads(k_refs, heads, dh), _gdn_heads(v_refs, heads, dh),
            rows(gr_refs), rows(br_refs))


def _gdn_fwd_body(qf, kf, vf, qb, kb, vb, grf, brf, grb, brb, of, ob, s0f, s0b, state, *, heads, dh):
    @pl.when(pl.program_id(0) == 0)
    def _():
        state[...] = jnp.zeros_like(state)

    s0 = state[...]
    s0f[...] = s0[:heads]
    s0b[...] = s0[heads:]
    o, new_state = _gdn_chunks(*_gdn_operands((qf, qb), (kf, kb), (vf, vb), (grf, grb), (brf, brb), heads, dh), s0,
                               n_fwd=heads)
    state[...] = new_state
    for i in range(2 * heads):
        (of, ob)[i // heads][:, (i % heads) * dh:(i % heads + 1) * dh] = o[i]


def _gdn_bwd_body(qf, kf, vf, qb, kb, vb, grf, brf, grb, brb, s0f, s0b, dof, dob,
                  dqf, dkf, dvf, dqb, dkb, dvb, dgrf, dbrf, dgrb, dbrb, dstate, *, heads, dh):
    @pl.when(pl.program_id(0) == 0)
    def _():
        dstate[...] = jnp.zeros_like(dstate)

    ops = _gdn_operands((qf, qb), (kf, kb), (vf, vb), (grf, grb), (brf, brb), heads, dh)
    s0 = jnp.concatenate([s0f[...], s0b[...]], axis=0)
    do = _gdn_heads((dof, dob), heads, dh)
    _, vjp = jax.vjp(functools.partial(_gdn_chunks, n_fwd=heads), *ops, s0)
    dq, dk, dv, dgr, dbr, ds0 = vjp((do, dstate[...]))
    dstate[...] = ds0
    for i in range(2 * heads):
        d, cols = i // heads, slice((i % heads) * dh, (i % heads + 1) * dh)
        (dqf, dqb)[d][:, cols] = dq[i]
        (dkf, dkb)[d][:, cols] = dk[i]
        (dvf, dvb)[d][:, cols] = dv[i]
    dgrf[...] = dgr[:heads]
    dgrb[...] = dgr[heads:]
    dbrf[...] = dbr[:heads]
    dbrb[...] = dbr[heads:]


def _gdn_specs(nc, heads, dh, backward):
    c = GDN_CHUNK
    up = lambda n: n
    down = lambda n: nc - 1 - n
    out = []
    for chunk in ((down, up) if backward else (up, down)):
        out.append((
            pl.BlockSpec((c, heads * dh), lambda n, chunk=chunk: (chunk(n), 0)),
            pl.BlockSpec((heads, None, 1, c), lambda n, chunk=chunk: (0, chunk(n), 0, 0)),
            pl.BlockSpec((heads, None, dh, dh), lambda n, chunk=chunk: (0, chunk(n), 0, 0)),
        ))
    return out


def _gdn_fwd_call(q, k, v, grf, brf, grb, brb):
    seq, width = q.shape
    heads = grf.shape[0]
    dh = width // heads
    nc = seq // GDN_CHUNK
    (seq_f, row_f, st_f), (seq_b, row_b, st_b) = _gdn_specs(nc, heads, dh, False)
    states = jax.ShapeDtypeStruct((heads, nc, dh, dh), F32)
    return pl.pallas_call(
        functools.partial(_gdn_fwd_body, heads=heads, dh=dh),
        name="gdn_fwd",
        grid=(nc,),
        in_specs=[seq_f] * 3 + [seq_b] * 3 + [row_f, row_f, row_b, row_b],
        out_specs=[seq_f, seq_b, st_f, st_b],
        out_shape=[jax.ShapeDtypeStruct(q.shape, F32)] * 2 + [states] * 2,
        scratch_shapes=[pltpu.VMEM((2 * heads, dh, dh), F32)],
        compiler_params=_cparams("arbitrary"),
    )(q, k, v, q, k, v, grf, brf, grb, brb)


def _gdn_bwd_call(q, k, v, grf, brf, grb, brb, s0f, s0b, dof, dob):
    seq, width = q.shape
    heads = grf.shape[0]
    dh = width // heads
    nc = seq // GDN_CHUNK
    (seq_f, row_f, st_f), (seq_b, row_b, st_b) = _gdn_specs(nc, heads, dh, True)
    like = lambda t: jax.ShapeDtypeStruct(t.shape, F32)
    return pl.pallas_call(
        functools.partial(_gdn_bwd_body, heads=heads, dh=dh),
        name="gdn_bwd",
        grid=(nc,),
        in_specs=[seq_f] * 3 + [seq_b] * 3 + [row_f, row_f, row_b, row_b, st_f, st_b, seq_f, seq_b],
        out_specs=[seq_f] * 3 + [seq_b] * 3 + [row_f, row_f, row_b, row_b],
        out_shape=[like(q)] * 6 + [like(grf), like(brf), like(grb), like(brb)],
        scratch_shapes=[pltpu.VMEM((2 * heads, dh, dh), F32)],
        compiler_params=_cparams("arbitrary"),
    )(q, k, v, q, k, v, grf, brf, grb, brb, s0f, s0b, dof, dob)


@jax.custom_vjp
def gdn_delta_rule(q, k, v, grf, brf, grb, brb):
    return tuple(_gdn_fwd_call(q, k, v, grf, brf, grb, brb)[:2])


def _gdn_delta_rule_fwd(*ops):
    of, ob, s0f, s0b = _gdn_fwd_call(*ops)
    return (of, ob), (*ops, s0f, s0b)


def _gdn_delta_rule_bwd(res, do):
    dqf, dkf, dvf, dqb, dkb, dvb, *small = _gdn_bwd_call(*res, *do)
    return (dqf + dqb, dkf + dkb, dvf + dvb, *small)


gdn_delta_rule.defvjp(_gdn_delta_rule_fwd, _gdn_delta_rule_bwd)


def _mesh_position():
    return lax.axis_index("x"), lax.axis_index("y"), lax.axis_index("c")


def _all_gather_body(x_ref, out_ref, send_sems, recv_sems, local_sem):
    x, y, c = _mesh_position()
    me, sibling = (x, y, c), (x, y, 1 - c)
    chips = [(1 - x, y), (x, 1 - y), (1 - x, 1 - y)]

    def slot(px, py, pc):
        return out_ref.at[4 * px + 2 * py + pc]

    def copy(k, block, to, src=None):
        return pltpu.make_async_remote_copy(
            src_ref=slot(*block) if src is None else src, dst_ref=slot(*block),
            send_sem=send_sems.at[k], recv_sem=recv_sems.at[k], device_id=to, device_id_type=pl.DeviceIdType.MESH)

    mine = pltpu.make_async_copy(x_ref, slot(*me), local_sem)
    mine.start()
    first = [copy(0, me, sibling, src=x_ref)]
    first += [copy(1 + j, me, (*chip, c), src=x_ref) for j, chip in enumerate(chips)]
    for cp in first:
        cp.start()
    passed = [copy(4 + j, (*chip, c), sibling) for j, chip in enumerate(chips)]
    for j, chip in enumerate(chips):
        copy(1 + j, (*chip, c), me).wait_recv()
        passed[j].start()
    copy(0, sibling, me).wait_recv()
    for j, chip in enumerate(chips):
        copy(4 + j, (*chip, 1 - c), me).wait_recv()
    for cp in first + passed:
        cp.wait_send()
    mine.wait()


def all_gather(x, name):
    return pl.pallas_call(
        functools.partial(_all_gather_body),
        name=name,
        in_specs=[pl.BlockSpec(memory_space=pl.ANY)],
        out_specs=pl.BlockSpec(memory_space=pl.ANY),
        out_shape=jax.ShapeDtypeStruct((N_DEV,) + x.shape, x.dtype),
        scratch_shapes=[pltpu.SemaphoreType.DMA((N_DEV - 1,)), pltpu.SemaphoreType.DMA((N_DEV - 1,)), pltpu.SemaphoreType.DMA],
    )(x)


N_CHIPS = 4


def _swap_cores_body(x_ref, own_ref, got_ref, send_sem, recv_sem, local_sem):
    x, y, c = _mesh_position()
    keep = pltpu.make_async_copy(x_ref.at[c], own_ref, local_sem)
    keep.start()
    swap = pltpu.make_async_remote_copy(
        src_ref=x_ref.at[1 - c], dst_ref=got_ref, send_sem=send_sem, recv_sem=recv_sem,
        device_id=(x, y, 1 - c), device_id_type=pl.DeviceIdType.MESH)
    swap.start()
    swap.wait()
    keep.wait()


def swap_cores(x, name):
    half = jax.ShapeDtypeStruct(x.shape[1:], x.dtype)
    return pl.pallas_call(
        functools.partial(_swap_cores_body),
        name=name,
        in_specs=[pl.BlockSpec(memory_space=pl.ANY)],
        out_specs=[pl.BlockSpec(memory_space=pl.ANY)] * 2,
        out_shape=[half, half],
        scratch_shapes=[pltpu.SemaphoreType.DMA, pltpu.SemaphoreType.DMA, pltpu.SemaphoreType.DMA],
    )(x)


def _exchange_chips_body(x_ref, out_ref, send_sems, recv_sems, local_sem):
    x, y, c = _mesh_position()
    my_chip = 2 * x + y
    copies = []
    for m in range(1, N_CHIPS):
        px = 1 - x if m & 2 else x
        py = 1 - y if m & 1 else y
        copies.append(pltpu.make_async_remote_copy(
            src_ref=x_ref.at[2 * px + py], dst_ref=out_ref.at[my_chip],
            send_sem=send_sems.at[m - 1], recv_sem=recv_sems.at[m - 1],
            device_id=(px, py, c), device_id_type=pl.DeviceIdType.MESH))
    for cp in copies:
        cp.start()
    mine = pltpu.make_async_copy(x_ref.at[my_chip], out_ref.at[my_chip], local_sem)
    mine.start()
    for cp in copies:
        cp.wait()
    mine.wait()


def exchange_chips(x, name):
    return pl.pallas_call(
        functools.partial(_exchange_chips_body),
        name=name,
        in_specs=[pl.BlockSpec(memory_space=pl.ANY)],
        out_specs=pl.BlockSpec(memory_space=pl.ANY),
        out_shape=jax.ShapeDtypeStruct(x.shape, x.dtype),
        scratch_shapes=[pltpu.SemaphoreType.DMA((N_CHIPS - 1,)), pltpu.SemaphoreType.DMA((N_CHIPS - 1,)),
                        pltpu.SemaphoreType.DMA],
    )(x)


def _add_pairs_body(a_ref, b_ref, o_ref):
    o_ref[...] = (a_ref[...].astype(F32) + b_ref[...].astype(F32)).astype(o_ref.dtype)


def add_pairs(a, b):
    n, rows, cols = a.shape
    tr = _pick(rows, 512, 2 * SUBLANES)
    spec = pl.BlockSpec((None, tr, cols), lambda k, i: (k, i, 0))
    return pl.pallas_call(
        functools.partial(_add_pairs_body),
        name="add_pairs",
        grid=(n, rows // tr),
        in_specs=[spec, spec],
        out_specs=spec,
        out_shape=jax.ShapeDtypeStruct(a.shape, a.dtype),
        compiler_params=_cparams("parallel", "parallel"),
    )(a, b)


def reduce_scatter(parts):
    own, got = swap_cores(parts, "exchange_grads_cores")
    return sum_leading(exchange_chips(add_pairs(own, got), "exchange_grads_chips"))


PACK_COLS = 1024
PACK_ROW_MULTIPLE = 512


def _pack(arrays, dtype):
    parts = []
    rows = 0
    for a in arrays:
        flat = a.reshape(-1).astype(dtype)
        n_rows = -(-flat.shape[0] // PACK_COLS)
        parts.append(jnp.pad(flat, (0, n_rows * PACK_COLS - flat.shape[0])).reshape(n_rows, PACK_COLS))
        rows += n_rows
    pad_rows = -rows % PACK_ROW_MULTIPLE
    if pad_rows:
        parts.append(jnp.zeros((pad_rows, PACK_COLS), dtype))
    return jnp.concatenate(parts, axis=0)


def _unpack(flat, shapes):
    lead = flat.shape[:-2]
    out = []
    row = 0
    for shape in shapes:
        size = int(np.prod(shape))
        n_rows = -(-size // PACK_COLS)
        seg = flat[..., row:row + n_rows, :].reshape(lead + (n_rows * PACK_COLS,))[..., :size]
        out.append(seg.reshape(lead + tuple(shape)))
        row += n_rows
    return out


W_IN_SMALL = 4608


def _split_w_in(w_in):
    d = w_in.shape[0]
    small = jnp.concatenate(
        [w_in[:, 0:2560], w_in[:, 2576:4304], w_in[:, 2560:2576], jnp.zeros((d, W_IN_SMALL - 4304), w_in.dtype)], axis=1)
    return small, w_in[:, 4304:]


def _s5_mixer(u, lam_re, lam_im, log_step, b_re, b_im, c_re, c_im, d_skip, w_glu, b_glu):
    g, p, hg = S5_GROUPS, S5_STATE, S5_GROUP
    lam_re = jnp.minimum(lam_re, -1e-4)
    dt = jnp.exp(log_step)[..., None]
    mag = jnp.exp(lam_re * dt)
    abar_r = mag * jnp.cos(lam_im * dt)
    abar_i = mag * jnp.sin(lam_im * dt)
    den = lam_re * lam_re + lam_im * lam_im
    xr = abar_r - 1.0
    xi = abar_i
    coef_r = (xr * lam_re + xi * lam_im) / den
    coef_i = (xi * lam_re - xr * lam_im) / den
    bbar_r = coef_r[..., None] * b_re - coef_i[..., None] * b_im
    bbar_i = coef_r[..., None] * b_im + coef_i[..., None] * b_re
    eye = jnp.eye(g, dtype=F32)

    def block_diag(t, rows, cols):
        return (eye[:, None, :, None] * t[:, :, None, :]).reshape(g * rows, g * cols)

    b_all = jnp.concatenate(
        [block_diag(t.transpose(0, 2, 1), hg, p) for t in (bbar_r[0], bbar_i[0], bbar_r[1], bbar_i[1])], axis=1)
    c_all = jnp.concatenate(
        [block_diag(t.transpose(0, 2, 1), p, hg) for t in (c_re[0], -c_im[0], c_re[1], -c_im[1])], axis=0)
    a_all = jnp.stack([abar_r.reshape(2, g * p), abar_i.reshape(2, g * p)], axis=1)
    s = s5_scan(a_all, mm(u, b_all))
    y = mm(s, c_all) + d_skip * u
    y = jax.nn.gelu(y)
    return y * jax.nn.sigmoid(wmm(y, *w_glu) + b_glu)


def _gdn_mixer(qkv, z, beta_logits, decay_logits, conv_w, a_log, dt_bias, o_gain):
    seq = qkv.shape[0]
    h, dh, c = GDN_HEADS, GDN_HEAD_DIM, GDN_CHUNK
    nc = seq // c
    padded = jnp.pad(qkv, ((GDN_CONV // 2, GDN_CONV - 1 - GDN_CONV // 2), (0, 0)))
    conv = sum(padded[j:j + seq] * conv_w[j] for j in range(GDN_CONV))
    q, k, v = jnp.split(jax.nn.silu(conv), 3, axis=-1)

    def l2(t):
        t = t.reshape(seq, h, dh)
        return (t * lax.rsqrt(jnp.sum(t * t, axis=-1, keepdims=True) + 1e-6)).reshape(seq, h * dh)

    q = l2(q) * (dh**-0.5)
    k = l2(k)
    beta = jax.nn.sigmoid(beta_logits).reshape(seq, 2, h)
    g = -jnp.exp(a_log) * jax.nn.softplus(decay_logits.reshape(seq, 2, h) + dt_bias)
    small = []
    for d in range(2):
        gcs = lax.cumsum(g[:, d].reshape(nc, c, h), axis=1, reverse=d == 1).transpose(2, 0, 1)
        small += [gcs.reshape(h, nc, 1, c), beta[:, d].T.reshape(h, nc, 1, c)]
    o_fwd, o_bwd = gdn_delta_rule(q, k, v, *small)
    o = rmsnorm((o_fwd + o_bwd).reshape(seq * h, dh), o_gain).reshape(seq, h * dh)
    return o * jax.nn.silu(z)


def _t5_bucket(rel):
    nb = T5_BUCKETS // 2
    max_exact = nb // 2
    ret = jnp.where(rel > 0, nb, 0)
    n = jnp.abs(rel)
    nf = jnp.maximum(n, 1).astype(F32)
    large = max_exact + (jnp.log(nf / max_exact) / math.log(T5_MAX_DISTANCE / max_exact) * (nb - max_exact)).astype(jnp.int32)
    large = jnp.minimum(large, nb - 1)
    return ret + jnp.where(n < max_exact, n, large)


def _swa_mixer(q, kv, sink, t5_bias):
    seq = q.shape[0]
    kvh, g, d, blk = SWA_KV_HEADS, SWA_GROUP, SWA_HEAD_DIM, SWA_BLOCK
    q4 = q.reshape(seq, kvh, g, d).transpose(1, 2, 0, 3)
    k, v = jnp.split(kv, 2, axis=-1)
    heads_first = lambda t: t.reshape(seq, kvh, d).transpose(1, 0, 2)
    rel = jnp.arange(3 * blk)[None, :] - blk - jnp.arange(blk)[:, None]
    onehot = (_t5_bucket(rel)[..., None] == jnp.arange(T5_BUCKETS)).astype(F32)
    bias = jnp.einsum("qsb,bh->hqs", onehot, t5_bias, precision=lax.Precision.HIGHEST).reshape(kvh, g * blk, 3 * blk)
    sink_col = jnp.broadcast_to(sink.reshape(kvh, g, 1, 1), (kvh, g, blk, 1)).reshape(kvh, g * blk, 1)
    o = swa_attention(q4, heads_first(k), heads_first(v), bias, sink_col)
    return o.transpose(2, 0, 1, 3).reshape(seq, kvh * g * d)


def _rope(t, cos, sin):
    t1, t2 = jnp.split(t, 2, axis=-1)
    return jnp.concatenate([t1 * cos - t2 * sin, t2 * cos + t1 * sin], axis=-1)


def _mla_mixer(c_q, c_kv, k_rope, q_gain, kv_gain, w_uq, w_ukv):
    seq = c_q.shape[0]
    h = MLA_HEADS
    q = wmm(rmsnorm(c_q, q_gain), *w_uq).reshape(seq, h, MLA_NOPE + MLA_ROPE)
    kv = wmm(rmsnorm(c_kv, kv_gain), *w_ukv).reshape(seq, h, MLA_NOPE + MLA_V)
    q_nope, q_pe = q[..., :MLA_NOPE], q[..., MLA_NOPE:]
    k_nope, v = kv[..., :MLA_NOPE], kv[..., MLA_NOPE:]
    pos = jnp.arange(seq, dtype=F32)
    inv_freq = ROPE_THETA ** (-jnp.arange(0, MLA_ROPE, 2, dtype=F32) / MLA_ROPE)
    ang = pos[:, None] * inv_freq[None, :]
    cos, sin = jnp.cos(ang)[:, None, :], jnp.sin(ang)[:, None, :]
    q_pe = _rope(q_pe, cos, sin)
    k_pe = _rope(k_rope[:, None, :], cos, sin)
    qf = jnp.concatenate([q_nope, q_pe], axis=-1)
    kf = jnp.concatenate([k_nope, jnp.broadcast_to(k_pe, (seq, h, MLA_ROPE))], axis=-1)
    o = mla_attention(qf.transpose(1, 0, 2), kf.transpose(1, 0, 2), v.transpose(1, 0, 2))
    return o.transpose(1, 0, 2).reshape(seq, h * MLA_V)


def _layer(x, p, t5_bias):
    d = x.shape[1]
    h = rmsnorm(x, p["mix_pre_gain"])
    (w_small, w_gate), (w_small_bf16, w_gate_bf16) = (_split_w_in(t) for t in p["w_in"])
    ps = wmm(h, w_small, w_small_bf16)
    gate_logits = wmm(h, w_gate, w_gate_bf16)
    y_a = _s5_mixer(ps[:, 0:512], p["s5_lam_re"], p["s5_lam_im"], p["s5_log_step"], p["s5_b_re"], p["s5_b_im"],
                    p["s5_c_re"], p["s5_c_im"], p["s5_d"], p["s5_w_glu"], p["s5_b_glu"])
    y_b = _gdn_mixer(ps[:, 512:2048], ps[:, 2048:2560], ps[:, 4288:4296], ps[:, 4296:4304], p["gdn_conv"],
                     p["gdn_a_log"], p["gdn_dt_bias"], p["gdn_o_gain"])
    y_c = _swa_mixer(ps[:, 2560:3072], ps[:, 3072:3328], p["swa_sink"], t5_bias)
    y_d = _mla_mixer(ps[:, 3328:3712], ps[:, 3712:4224], ps[:, 4224:4288], p["mla_q_gain"], p["mla_kv_gain"],
                     p["mla_w_uq"], p["mla_w_ukv"])
    merged = sum(jax.nn.sigmoid(gate_logits[:, b * d:(b + 1) * d]) * wmm(y, p["w_branch"][0][b], p["w_branch"][1][b])
                 for b, y in enumerate((y_a, y_b, y_c, y_d)))
    x = x + rmsnorm(wmm(merged, *p["w_out"]), p["mix_post_gain"])
    h = rmsnorm(x, p["mlp_pre_gain"])
    f = wmm(jnp.square(jax.nn.relu(wmm(h, *p["w_mlp_in"]))), *p["w_mlp_out"])
    return x + rmsnorm(f, p["mlp_post_gain"])


LAYER_WEIGHTS = ("w_in", "s5_lam_re", "s5_lam_im", "s5_log_step", "s5_b_re", "s5_b_im", "s5_c_re", "s5_c_im", "s5_d",
                 "s5_w_glu", "s5_b_glu", "gdn_conv", "gdn_a_log", "gdn_dt_bias", "gdn_o_gain", "swa_sink", "mla_q_gain",
                 "mla_kv_gain", "mla_w_uq", "mla_w_ukv", "w_branch", "w_out", "mix_pre_gain", "mix_post_gain",
                 "mlp_pre_gain", "mlp_post_gain", "w_mlp_in", "w_mlp_out")


def _forward(x, weights, values):
    for layer in range(DEPTH):
        p = {n: (weights[n][layer], values[n][layer]) if n in values else weights[n][layer] for n in LAYER_WEIGHTS}
        x = _layer(x, p, weights["t5_bias"])
    return x


WEIGHT_NAMES = ("w_in", "s5_lam_re", "s5_lam_im", "s5_log_step", "s5_b_re", "s5_b_im", "s5_c_re", "s5_c_im", "s5_d",
                "s5_w_glu", "s5_b_glu", "gdn_conv", "gdn_a_log", "gdn_dt_bias", "gdn_o_gain", "swa_sink", "t5_bias",
                "mla_q_gain", "mla_kv_gain", "mla_w_uq", "mla_w_ukv", "w_branch", "w_out", "mix_pre_gain", "mix_post_gain",
                "mlp_pre_gain", "mlp_post_gain", "w_mlp_in", "w_mlp_out")
SHARD_AXIS = {"w_in": 2, "s5_w_glu": 1, "mla_w_uq": 2, "mla_w_ukv": 2, "w_branch": 3, "w_out": 1, "w_mlp_in": 2,
              "w_mlp_out": 1}
CONV = "gdn_conv"
CONV_AXIS = 2
REPLICATED = tuple(n for n in WEIGHT_NAMES if n not in SHARD_AXIS and n != CONV)


def _step(x, target, w, m, v):
    big = tuple(SHARD_AXIS)
    x_pos, y_pos, c_pos = _mesh_position()
    my = 4 * x_pos + 2 * y_pos + c_pos

    shard_shapes = [w[n].shape for n in big]
    gathered = _unpack(all_gather(_pack([w[n] for n in big], BF16), "gather_weights"), shard_shapes)
    values = {n: jnp.concatenate([g[k] for k in range(N_DEV)], axis=SHARD_AXIS[n]) for n, g in zip(big, gathered)}
    full = {n: t.astype(F32) for n, t in values.items()}
    conv_all = _unpack(all_gather(_pack([w[CONV]], F32), "gather_conv"), [w[CONV].shape])[0]
    full[CONV] = jnp.concatenate([conv_all[k] for k in range(N_DEV)], axis=CONV_AXIS)
    for n in REPLICATED:
        full[n] = w[n]

    y, vjp = jax.vjp(lambda x_, full_: _forward(x_, full_, values), x, full)
    loss_rows, dy = loss_head(y, target)
    grad_x, grad_full = vjp(dy)
    loss_part = jnp.sum(loss_rows)

    def pieces(n, k):
        size = w[n].shape[SHARD_AXIS[n]]
        return lax.slice_in_dim(grad_full[n], k * size, (k + 1) * size, axis=SHARD_AXIS[n])

    send = jnp.stack([jnp.stack([_pack([pieces(n, 2 * chip + core) for n in big], BF16) for chip in range(N_CHIPS)])
                      for core in range(2)])
    grad_big = reduce_scatter(send)
    small_names = REPLICATED + (CONV,)
    small = [grad_full[n] for n in small_names] + [loss_part.reshape(1)]
    small_sum = sum_leading(all_gather(_pack(small, F32), "gather_small_grads"))
    small_grads = _unpack(small_sum, [a.shape for a in small])
    loss = small_grads[-1][0]
    grads = dict(zip(small_names, small_grads[:-1]))
    conv_size = w[CONV].shape[CONV_AXIS]
    grads[CONV] = lax.dynamic_slice_in_dim(grads[CONV], my * conv_size, conv_size, axis=CONV_AXIS)
    grads.update(zip(big, _unpack(grad_big, shard_shapes)))

    small_shapes = [w[n].shape for n in small_names]
    packed = lambda t: _pack([t[n] for n in small_names], F32)
    delta, new_m, new_v = (dict(zip(small_names, _unpack(t, small_shapes)))
                           for t in adamw_flat(packed(w), packed(grads), packed(m), packed(v)))
    for n in big:
        rows2d = lambda t: t.reshape(-1, t.shape[-1])
        outs = adamw_flat(rows2d(w[n]), rows2d(grads[n]), rows2d(m[n]), rows2d(v[n]))
        delta[n], new_m[n], new_v[n] = (t.reshape(w[n].shape) for t in outs)
    return loss, grad_x, grads, delta, new_m, new_v


def kernel(x, w_in, s5_lam_re, s5_lam_im, s5_log_step, s5_b_re, s5_b_im, s5_c_re, s5_c_im, s5_d, s5_w_glu, s5_b_glu, gdn_conv, gdn_a_log, gdn_dt_bias, gdn_o_gain, swa_sink, t5_bias, mla_q_gain, mla_kv_gain, mla_w_uq, mla_w_ukv, w_branch, w_out, mix_pre_gain, mix_post_gain, mlp_pre_gain, mlp_post_gain, w_mlp_in, w_mlp_out, loss_target, m_w_in, m_s5_lam_re, m_s5_lam_im, m_s5_log_step, m_s5_b_re, m_s5_b_im, m_s5_c_re, m_s5_c_im, m_s5_d, m_s5_w_glu, m_s5_b_glu, m_gdn_conv, m_gdn_a_log, m_gdn_dt_bias, m_gdn_o_gain, m_swa_sink, m_t5_bias, m_mla_q_gain, m_mla_kv_gain, m_mla_w_uq, m_mla_w_ukv, m_w_branch, m_w_out, m_mix_pre_gain, m_mix_post_gain, m_mlp_pre_gain, m_mlp_post_gain, m_w_mlp_in, m_w_mlp_out, v_w_in, v_s5_lam_re, v_s5_lam_im, v_s5_log_step, v_s5_b_re, v_s5_b_im, v_s5_c_re, v_s5_c_im, v_s5_d, v_s5_w_glu, v_s5_b_glu, v_gdn_conv, v_gdn_a_log, v_gdn_dt_bias, v_gdn_o_gain, v_swa_sink, v_t5_bias, v_mla_q_gain, v_mla_kv_gain, v_mla_w_uq, v_mla_w_ukv, v_w_branch, v_w_out, v_mix_pre_gain, v_mix_post_gain, v_mlp_pre_gain, v_mlp_post_gain, v_w_mlp_in, v_w_mlp_out):
    args = locals()
    w = {n: args[n] for n in WEIGHT_NAMES}
    m = {n: args["m_" + n] for n in WEIGHT_NAMES}
    v = {n: args["v_" + n] for n in WEIGHT_NAMES}
    loss, grad_x, grads, delta, new_m, new_v = _step(x[0], loss_target[0], w, m, v)
    return (loss, grad_x[None], *[grads[n] for n in WEIGHT_NAMES], *[delta[n] for n in WEIGHT_NAMES],
            *[new_m[n] for n in WEIGHT_NAMES], *[new_v[n] for n in WEIGHT_NAMES])
```

```python
import functools
import math

import jax
import jax.numpy as jnp
import numpy as np
from jax import lax
from jax.experimental import pallas as pl
from jax.experimental.pallas import tpu as pltpu

F32 = jnp.float32
BF16 = jnp.bfloat16

VMEM_LIMIT_BYTES = 56 * 1024 * 1024
LANES = 128
SUBLANES = 8

N_DEV = 8
MESH_AXES = ("x", "y", "c")

DEPTH = 4
N_BRANCHES = 4
BRANCH_WIDTH = 512
NORM_EPS = 1e-6
S5_GROUP = 16
S5_GROUPS = 32
S5_STATE = 64
S5_WIDTH = S5_GROUPS * S5_STATE
GDN_HEAD_DIM = 128
GDN_HEADS = 4
GDN_CONV = 4
GDN_CHUNK = 64
SWA_HEAD_DIM = 64
SWA_HEADS = 8
SWA_KV_HEADS = 2
SWA_GROUP = SWA_HEADS // SWA_KV_HEADS
WINDOW = 128
SWA_BLOCK = 128
T5_BUCKETS = 32
T5_MAX_DISTANCE = 128
MLA_HEADS = 4
MLA_Q_RANK = 384
MLA_KV_RANK = 512
MLA_NOPE = 128
MLA_ROPE = 64
MLA_V = 128
ROPE_THETA = 10000.0

ADAM_LR = 0.001
ADAM_B1 = 0.9
ADAM_B2 = 0.999
ADAM_EPS = 1e-08
ADAM_WD = 0.01
ADAM_STEP = 10

NEG_BIG = -1e30


def _cparams(*sem):
    return pltpu.CompilerParams(dimension_semantics=sem if sem else None, vmem_limit_bytes=VMEM_LIMIT_BYTES)


def _pick(n, pref, unit):
    if n <= pref:
        return n
    t = (pref // unit) * unit
    while t >= unit:
        if n % t == 0:
            return t
        t -= unit
    return n


def _bdot(a, b, dims):
    return lax.dot_general(a.astype(BF16), b.astype(BF16), (dims, ((), ())), preferred_element_type=F32)


def _dot_nn(a, b):
    return _bdot(a, b, ((1,), (0,)))


def _dot_nt(a, b):
    return _bdot(a, b, ((1,), (1,)))


def _dot_tn(a, b):
    return _bdot(a, b, ((0,), (0,)))


def _mm_body(a_ref, b_ref, o_ref, *, ta, tb, nk):
    dims = ((0,) if ta else (1,), (1,) if tb else (0,))
    part = _bdot(a_ref[...], b_ref[...], dims)
    if nk == 1:
        o_ref[...] = part
    else:
        k = pl.program_id(2)

        @pl.when(k == 0)
        def _():
            o_ref[...] = part

        @pl.when(k > 0)
        def _():
            o_ref[...] += part


MM_BLOCK_BYTES = 32 * 1024 * 1024


def _mm_tiles(m, n, k, a_bytes, b_bytes):
    tm, tn = _pick(m, 1024, LANES), _pick(n, 1024, LANES)
    tk = LANES if k % LANES == 0 else k
    for cand in range(k, 0, -LANES) if k % LANES == 0 else (k,):
        if k % cand == 0 and 2 * cand * (tm * a_bytes + tn * b_bytes) + 2 * tm * tn * 4 <= MM_BLOCK_BYTES:
            tk = cand
            break
    return tm, tn, tk


def _mm_call(a, b, *, ta=False, tb=False, name):
    m, k = (a.shape[1], a.shape[0]) if ta else a.shape
    n = b.shape[0] if tb else b.shape[1]
    assert (b.shape[1] if tb else b.shape[0]) == k, (a.shape, b.shape, ta, tb)
    tm, tn, tk = _mm_tiles(m, n, k, a.dtype.itemsize, b.dtype.itemsize)
    nk = k // tk
    a_spec = pl.BlockSpec((tk, tm), lambda i, j, kk: (kk, i)) if ta else pl.BlockSpec((tm, tk), lambda i, j, kk: (i, kk))
    b_spec = pl.BlockSpec((tn, tk), lambda i, j, kk: (j, kk)) if tb else pl.BlockSpec((tk, tn), lambda i, j, kk: (kk, j))
    return pl.pallas_call(
        functools.partial(_mm_body, ta=ta, tb=tb, nk=nk),
        name=name,
        grid=(m // tm, n // tn, nk),
        in_specs=[a_spec, b_spec],
        out_specs=pl.BlockSpec((tm, tn), lambda i, j, kk: (i, j)),
        out_shape=jax.ShapeDtypeStruct((m, n), F32),
        compiler_params=_cparams("parallel", "parallel", "arbitrary"),
    )(a, b)


@jax.custom_vjp
def mm(a, b):
    return _mm_call(a, b, name="mm_fwd")


def _mm_fwd(a, b):
    return _mm_call(a, b, name="mm_fwd"), (a, b)


def _mm_bwd(res, g):
    a, b = res
    return _mm_call(g, b, tb=True, name="mm_da"), _mm_call(a, g, ta=True, name="mm_db")


mm.defvjp(_mm_fwd, _mm_bwd)


@jax.custom_vjp
def wmm(a, w, w_bf16):
    return _mm_call(a, w_bf16, name="wmm_fwd")


def _wmm_fwd(a, w, w_bf16):
    return _mm_call(a, w_bf16, name="wmm_fwd"), (a, w_bf16)


def _wmm_bwd(res, g):
    a, w_bf16 = res
    return _mm_call(g, w_bf16, tb=True, name="wmm_da"), _mm_call(a, g, ta=True, name="wmm_db"), jnp.zeros_like(w_bf16)


wmm.defvjp(_wmm_fwd, _wmm_bwd)


def _rms_fwd_body(x_ref, g_ref, y_ref):
    x = x_ref[...]
    r = lax.rsqrt(jnp.mean(x * x, axis=-1, keepdims=True) + NORM_EPS)
    y_ref[...] = x * r * g_ref[...]


def _rms_bwd_body(x_ref, g_ref, dy_ref, dx_ref, dg_ref, *, tr):
    i = pl.program_id(0)

    @pl.when(i == 0)
    def _():
        dg_ref[...] = jnp.zeros_like(dg_ref)

    x = x_ref[...]
    dy = dy_ref[...]
    r = lax.rsqrt(jnp.mean(x * x, axis=-1, keepdims=True) + NORM_EPS)
    xhat = x * r
    gy = dy * g_ref[...]
    dx_ref[...] = r * (gy - xhat * jnp.mean(gy * xhat, axis=-1, keepdims=True))
    dg_ref[...] += jnp.sum((dy * xhat).reshape(tr // SUBLANES, SUBLANES, x.shape[-1]), axis=0)


def _rms_rows(rows, cols):
    return _pick(rows, max(SUBLANES, (2 * 1024 * 1024) // (4 * cols)), SUBLANES)


def _rms_fwd_call(x, gain):
    rows, cols = x.shape
    tr = _rms_rows(rows, cols)
    return pl.pallas_call(
        functools.partial(_rms_fwd_body),
        name="rms_fwd",
        grid=(rows // tr,),
        in_specs=[pl.BlockSpec((tr, cols), lambda i: (i, 0)), pl.BlockSpec((1, cols), lambda i: (0, 0))],
        out_specs=pl.BlockSpec((tr, cols), lambda i: (i, 0)),
        out_shape=jax.ShapeDtypeStruct((rows, cols), F32),
        compiler_params=_cparams("parallel"),
    )(x, gain.reshape(1, cols))


def _rms_bwd_call(x, gain, dy):
    rows, cols = x.shape
    tr = _rms_rows(rows, cols)
    dx, dg = pl.pallas_call(
        functools.partial(_rms_bwd_body, tr=tr),
        name="rms_bwd",
        grid=(rows // tr,),
        in_specs=[
            pl.BlockSpec((tr, cols), lambda i: (i, 0)),
            pl.BlockSpec((1, cols), lambda i: (0, 0)),
            pl.BlockSpec((tr, cols), lambda i: (i, 0)),
        ],
        out_specs=[pl.BlockSpec((tr, cols), lambda i: (i, 0)), pl.BlockSpec((SUBLANES, cols), lambda i: (0, 0))],
        out_shape=[jax.ShapeDtypeStruct((rows, cols), F32), jax.ShapeDtypeStruct((SUBLANES, cols), F32)],
        compiler_params=_cparams("arbitrary"),
    )(x, gain.reshape(1, cols), dy)
    return dx, jnp.sum(dg, axis=0)


@jax.custom_vjp
def rmsnorm(x, gain):
    return _rms_fwd_call(x, gain)


def _rmsnorm_fwd(x, gain):
    return _rms_fwd_call(x, gain), (x, gain)


def _rmsnorm_bwd(res, dy):
    x, gain = res
    return _rms_bwd_call(x, gain, dy)


rmsnorm.defvjp(_rmsnorm_fwd, _rmsnorm_bwd)


def _loss_body(y_ref, t_ref, rows_ref, dy_ref):
    d = y_ref[...] - t_ref[...]
    rows_ref[...] = 0.5 * jnp.mean(d * d, axis=-1, keepdims=True)
    dy_ref[...] = d * (1.0 / d.shape[-1])


def loss_head(y, target):
    rows, cols = y.shape
    tr = _rms_rows(rows, cols)
    return pl.pallas_call(
        functools.partial(_loss_body),
        name="loss_head",
        grid=(rows // tr,),
        in_specs=[pl.BlockSpec((tr, cols), lambda i: (i, 0))] * 2,
        out_specs=[pl.BlockSpec((tr, 1), lambda i: (i, 0)), pl.BlockSpec((tr, cols), lambda i: (i, 0))],
        out_shape=[jax.ShapeDtypeStruct((rows, 1), F32), jax.ShapeDtypeStruct((rows, cols), F32)],
        compiler_params=_cparams("parallel"),
    )(y, target)


def _adamw_body(w_ref, g_ref, m_ref, v_ref, d_ref, nm_ref, nv_ref):
    g = g_ref[...]
    m = ADAM_B1 * m_ref[...] + (1.0 - ADAM_B1) * g
    v = ADAM_B2 * v_ref[...] + (1.0 - ADAM_B2) * (g * g)
    m_hat = m / (1.0 - ADAM_B1**ADAM_STEP)
    v_hat = v / (1.0 - ADAM_B2**ADAM_STEP)
    d_ref[...] = -ADAM_LR * (m_hat / (jnp.sqrt(v_hat) + ADAM_EPS) + ADAM_WD * w_ref[...])
    nm_ref[...] = m
    nv_ref[...] = v


def adamw_flat(w, g, m, v):
    rows, cols = w.shape
    tr = _rms_rows(rows, cols)
    spec = pl.BlockSpec((tr, cols), lambda i: (i, 0))
    return pl.pallas_call(
        functools.partial(_adamw_body),
        name="adamw",
        grid=(rows // tr,),
        in_specs=[spec] * 4,
        out_specs=[spec] * 3,
        out_shape=[jax.ShapeDtypeStruct((rows, cols), F32)] * 3,
        compiler_params=_cparams("parallel"),
    )(w, g, m, v)


def _sum_body(x_ref, o_ref, *, n):
    acc = x_ref[0].astype(F32)
    for k in range(1, n):
        acc = acc + x_ref[k].astype(F32)
    o_ref[...] = acc


def sum_leading(x):
    n, rows, cols = x.shape
    tr = _pick(rows, 256, 2 * SUBLANES)
    return pl.pallas_call(
        functools.partial(_sum_body, n=n),
        name="sum_leading",
        grid=(rows // tr,),
        in_specs=[pl.BlockSpec((n, tr, cols), lambda i: (0, i, 0))],
        out_specs=pl.BlockSpec((tr, cols), lambda i: (i, 0)),
        out_shape=jax.ShapeDtypeStruct((rows, cols), F32),
        compiler_params=_cparams("parallel"),
    )(x)


def _mla_fwd_body(q_ref, k_ref, v_ref, o_ref, lse_ref, m_sc, l_sc, acc_sc, *, scale, nk):
    ki = pl.program_id(2)

    @pl.when(ki == 0)
    def _():
        m_sc[...] = jnp.full_like(m_sc, NEG_BIG)
        l_sc[...] = jnp.zeros_like(l_sc)
        acc_sc[...] = jnp.zeros_like(acc_sc)

    s = _dot_nt(q_ref[...], k_ref[...]) * scale
    m_prev = m_sc[...]
    m_new = jnp.maximum(m_prev, jnp.max(s, axis=1, keepdims=True))
    alpha = jnp.exp(m_prev - m_new)
    p = jnp.exp(s - m_new)
    l_sc[...] = alpha * l_sc[...] + jnp.sum(p, axis=1, keepdims=True)
    acc_sc[...] = alpha * acc_sc[...] + _dot_nn(p, v_ref[...])
    m_sc[...] = m_new

    @pl.when(ki == nk - 1)
    def _():
        o_ref[...] = acc_sc[...] / l_sc[...]
        lse_ref[...] = m_sc[...] + jnp.log(l_sc[...])


def _mla_fwd_call(q, k, v):
    h, seq, dq = q.shape
    dv = v.shape[-1]
    tq, tk = _pick(seq, 1024, LANES), _pick(seq, 1024, LANES)
    nk = seq // tk
    scale = dq**-0.5
    return pl.pallas_call(
        functools.partial(_mla_fwd_body, scale=scale, nk=nk),
        name="mla_fwd",
        grid=(h, seq // tq, nk),
        in_specs=[
            pl.BlockSpec((None, tq, dq), lambda hh, i, j: (hh, i, 0)),
            pl.BlockSpec((None, tk, dq), lambda hh, i, j: (hh, j, 0)),
            pl.BlockSpec((None, tk, dv), lambda hh, i, j: (hh, j, 0)),
        ],
        out_specs=[
            pl.BlockSpec((None, tq, dv), lambda hh, i, j: (hh, i, 0)),
            pl.BlockSpec((None, tq, 1), lambda hh, i, j: (hh, i, 0)),
        ],
        out_shape=[jax.ShapeDtypeStruct((h, seq, dv), F32), jax.ShapeDtypeStruct((h, seq, 1), F32)],
        scratch_shapes=[pltpu.VMEM((tq, 1), F32), pltpu.VMEM((tq, 1), F32), pltpu.VMEM((tq, dv), F32)],
        compiler_params=_cparams("parallel", "parallel", "arbitrary"),
    )(q, k, v)


def _mla_bwd_body(q_ref, k_ref, v_ref, do_ref, lse_ref, dl_ref, dq_ref, dk_ref, dv_ref, dk_sc, dv_sc, *, scale, nq, tq):
    ki = pl.program_id(1)
    qi = pl.program_id(2)

    @pl.when(jnp.logical_and(ki == 0, qi == 0))
    def _():
        dq_ref[...] = jnp.zeros_like(dq_ref)

    @pl.when(qi == 0)
    def _():
        dk_sc[...] = jnp.zeros_like(dk_sc)
        dv_sc[...] = jnp.zeros_like(dv_sc)

    q = q_ref[...]
    k = k_ref[...]
    do = do_ref[...]
    p = jnp.exp(_dot_nt(q, k) * scale - lse_ref[...])
    dv_sc[...] += _dot_tn(p, do)
    ds = p * (_dot_nt(do, v_ref[...]) - dl_ref[...]) * scale
    dk_sc[...] += _dot_tn(ds, q)
    rows = pl.ds(pl.multiple_of(qi * tq, tq), tq)
    dq_ref[rows, :] += _dot_nn(ds, k)

    @pl.when(qi == nq - 1)
    def _():
        dk_ref[...] = dk_sc[...]
        dv_ref[...] = dv_sc[...]


def _mla_bwd_call(q, k, v, do, lse, delta):
    h, seq, dq = q.shape
    dv = v.shape[-1]
    tq, tk = _pick(seq, 1024, LANES), _pick(seq, 1024, LANES)
    nq = seq // tq
    scale = dq**-0.5
    return pl.pallas_call(
        functools.partial(_mla_bwd_body, scale=scale, nq=nq, tq=tq),
        name="mla_bwd",
        grid=(h, seq // tk, nq),
        in_specs=[
            pl.BlockSpec((None, tq, dq), lambda hh, j, i: (hh, i, 0)),
            pl.BlockSpec((None, tk, dq), lambda hh, j, i: (hh, j, 0)),
            pl.BlockSpec((None, tk, dv), lambda hh, j, i: (hh, j, 0)),
            pl.BlockSpec((None, tq, dv), lambda hh, j, i: (hh, i, 0)),
            pl.BlockSpec((None, tq, 1), lambda hh, j, i: (hh, i, 0)),
            pl.BlockSpec((None, tq, 1), lambda hh, j, i: (hh, i, 0)),
        ],
        out_specs=[
            pl.BlockSpec((None, seq, dq), lambda hh, j, i: (hh, 0, 0)),
            pl.BlockSpec((None, tk, dq), lambda hh, j, i: (hh, j, 0)),
            pl.BlockSpec((None, tk, dv), lambda hh, j, i: (hh, j, 0)),
        ],
        out_shape=[
            jax.ShapeDtypeStruct((h, seq, dq), F32),
            jax.ShapeDtypeStruct((h, seq, dq), F32),
            jax.ShapeDtypeStruct((h, seq, dv), F32),
        ],
        scratch_shapes=[pltpu.VMEM((tk, dq), F32), pltpu.VMEM((tk, dv), F32)],
        compiler_params=_cparams("parallel", "arbitrary", "arbitrary"),
    )(q, k, v, do, lse, delta)


@jax.custom_vjp
def mla_attention(q, k, v):
    return _mla_fwd_call(q, k, v)[0]


def _mla_attention_fwd(q, k, v):
    o, lse = _mla_fwd_call(q, k, v)
    return o, (q, k, v, o, lse)


def _mla_attention_bwd(res, do):
    q, k, v, o, lse = res
    delta = jnp.sum(do * o, axis=-1, keepdims=True)
    return tuple(_mla_bwd_call(q, k, v, do, lse, delta))


mla_attention.defvjp(_mla_attention_fwd, _mla_attention_bwd)


def _swa_block(q4, kp, kc, kn, vp, vc, vn, bias, sink, *, valid):
    kb = jnp.concatenate([kp, kc, kn], axis=0)
    vb = jnp.concatenate([vp, vc, vn], axis=0)
    s = _dot_nt(q4, kb) * (SWA_HEAD_DIM**-0.5) + bias
    s = jnp.where(valid, s, NEG_BIG)
    m = lax.stop_gradient(jnp.maximum(jnp.max(s, axis=1, keepdims=True), sink))
    p = jnp.exp(s - m)
    denom = jnp.sum(p, axis=1, keepdims=True) + jnp.exp(sink - m)
    return _dot_nn(p / denom, vb)


def _swa_valid(n, seq):
    rows = SWA_GROUP * SWA_BLOCK
    qi = lax.broadcasted_iota(jnp.int32, (rows, 3 * SWA_BLOCK), 0) % SWA_BLOCK
    sj = lax.broadcasted_iota(jnp.int32, (rows, 3 * SWA_BLOCK), 1)
    rel = sj - SWA_BLOCK - qi
    kpos = n * SWA_BLOCK + sj - SWA_BLOCK
    return (jnp.abs(rel) <= WINDOW) & (kpos >= 0) & (kpos < seq)


def _swa_operands(q_ref, kp_ref, kc_ref, kn_ref, vp_ref, vc_ref, vn_ref, b_ref, s_ref):
    q4 = q_ref[...].reshape(SWA_GROUP * SWA_BLOCK, SWA_HEAD_DIM)
    return (q4, kp_ref[...], kc_ref[...], kn_ref[...], vp_ref[...], vc_ref[...], vn_ref[...], b_ref[...], s_ref[...])


def _swa_fwd_body(q_ref, kp_ref, kc_ref, kn_ref, vp_ref, vc_ref, vn_ref, b_ref, s_ref, o_ref, *, seq):
    valid = _swa_valid(pl.program_id(1), seq)
    out = _swa_block(*_swa_operands(q_ref, kp_ref, kc_ref, kn_ref, vp_ref, vc_ref, vn_ref, b_ref, s_ref), valid=valid)
    o_ref[...] = out.reshape(SWA_GROUP, SWA_BLOCK, SWA_HEAD_DIM)


def _swa_bwd_body(q_ref, kp_ref, kc_ref, kn_ref, vp_ref, vc_ref, vn_ref, b_ref, s_ref, do_ref,
                  dq_ref, dk_ref, dv_ref, db_ref, ds_ref, *, seq):
    n = pl.program_id(1)

    @pl.when(n == 0)
    def _():
        db_ref[...] = jnp.zeros_like(db_ref)
        ds_ref[...] = jnp.zeros_like(ds_ref)

    valid = _swa_valid(n, seq)
    ops = _swa_operands(q_ref, kp_ref, kc_ref, kn_ref, vp_ref, vc_ref, vn_ref, b_ref, s_ref)
    _, vjp = jax.vjp(functools.partial(_swa_block, valid=valid), *ops)
    do = do_ref[...].reshape(SWA_GROUP * SWA_BLOCK, SWA_HEAD_DIM)
    dq4, dkp, dkc, dkn, dvp, dvc, dvn, dbias, dsink = vjp(do)
    dq_ref[...] = dq4.reshape(SWA_GROUP, SWA_BLOCK, SWA_HEAD_DIM)
    dk_ref[0] = dkp
    dk_ref[1] = dkc
    dk_ref[2] = dkn
    dv_ref[0] = dvp
    dv_ref[1] = dvc
    dv_ref[2] = dvn
    db_ref[...] += dbias
    ds_ref[...] += dsink


def _swa_in_specs(nb):
    blk = (None, SWA_BLOCK, SWA_HEAD_DIM)
    prev = lambda h, n: (h, jnp.maximum(n - 1, 0), 0)
    own = lambda h, n: (h, n, 0)
    nxt = lambda h, n: (h, jnp.minimum(n + 1, nb - 1), 0)
    rows = SWA_GROUP * SWA_BLOCK
    return [
        pl.BlockSpec((None, SWA_GROUP, SWA_BLOCK, SWA_HEAD_DIM), lambda h, n: (h, 0, n, 0)),
        pl.BlockSpec(blk, prev), pl.BlockSpec(blk, own), pl.BlockSpec(blk, nxt),
        pl.BlockSpec(blk, prev), pl.BlockSpec(blk, own), pl.BlockSpec(blk, nxt),
        pl.BlockSpec((None, rows, 3 * SWA_BLOCK), lambda h, n: (h, 0, 0)),
        pl.BlockSpec((None, rows, 1), lambda h, n: (h, 0, 0)),
    ]


def _swa_fwd_call(q, k, v, bias, sink):
    kv, g, seq, d = q.shape
    nb = seq // SWA_BLOCK
    return pl.pallas_call(
        functools.partial(_swa_fwd_body, seq=seq),
        name="swa_fwd",
        grid=(kv, nb),
        in_specs=_swa_in_specs(nb),
        out_specs=pl.BlockSpec((None, g, SWA_BLOCK, d), lambda h, n: (h, 0, n, 0)),
        out_shape=jax.ShapeDtypeStruct(q.shape, F32),
        compiler_params=_cparams("parallel", "parallel"),
    )(q, k, k, k, v, v, v, bias, sink)


def _swa_bwd_call(q, k, v, bias, sink, do):
    kv, g, seq, d = q.shape
    nb = seq // SWA_BLOCK
    rows = g * SWA_BLOCK
    part = jax.ShapeDtypeStruct((kv, nb, 3, SWA_BLOCK, d), F32)
    part_spec = pl.BlockSpec((None, None, 3, SWA_BLOCK, d), lambda h, n: (h, n, 0, 0, 0))
    dq, dkp, dvp, dbias, dsink = pl.pallas_call(
        functools.partial(_swa_bwd_body, seq=seq),
        name="swa_bwd",
        grid=(kv, nb),
        in_specs=_swa_in_specs(nb) + [pl.BlockSpec((None, g, SWA_BLOCK, d), lambda h, n: (h, 0, n, 0))],
        out_specs=[
            pl.BlockSpec((None, g, SWA_BLOCK, d), lambda h, n: (h, 0, n, 0)),
            part_spec, part_spec,
            pl.BlockSpec((None, rows, 3 * SWA_BLOCK), lambda h, n: (h, 0, 0)),
            pl.BlockSpec((None, rows, 1), lambda h, n: (h, 0, 0)),
        ],
        out_shape=[jax.ShapeDtypeStruct(q.shape, F32), part, part,
                   jax.ShapeDtypeStruct(bias.shape, F32), jax.ShapeDtypeStruct(sink.shape, F32)],
        compiler_params=_cparams("parallel", "arbitrary"),
    )(q, k, k, k, v, v, v, bias, sink, do)

    def fold(p):
        zero = jnp.zeros_like(p[:, :1, 0])
        total = p[:, :, 1] + jnp.concatenate([p[:, 1:, 0], zero], axis=1) + jnp.concatenate([zero, p[:, :-1, 2]], axis=1)
        return total.reshape(kv, seq, d)

    return dq, fold(dkp), fold(dvp), dbias, dsink


@jax.custom_vjp
def swa_attention(q, k, v, bias, sink):
    return _swa_fwd_call(q, k, v, bias, sink)


def _swa_attention_fwd(q, k, v, bias, sink):
    return _swa_fwd_call(q, k, v, bias, sink), (q, k, v, bias, sink)


def _swa_attention_bwd(res, do):
    return _swa_bwd_call(*res, do)


swa_attention.defvjp(_swa_attention_fwd, _swa_attention_bwd)


def _scan_tiles(n_tiles, reverse, tile_fn, init):
    def step(i, carry):
        ti = (n_tiles - 1 - i) if reverse else i
        return tile_fn(pl.multiple_of(ti * SUBLANES, SUBLANES), carry)

    return lax.fori_loop(0, n_tiles, step, init)


def _row_order(reverse):
    return tuple(reversed(range(SUBLANES))) if reverse else tuple(range(SUBLANES))


def _s5_fwd_dir(a_ref, bu_ref, s_ref, carry, *, reverse, tt, w):
    ar, ai = a_ref[0:1, :], a_ref[1:2, :]
    rowid = lax.broadcasted_iota(jnp.int32, (SUBLANES, w), 0)

    def tile(base, c):
        sr, si = c
        x = bu_ref[pl.ds(base, SUBLANES), :]
        xr, xi = x[:, :w], x[:, w:]
        out_r = jnp.zeros((SUBLANES, w), F32)
        out_i = jnp.zeros((SUBLANES, w), F32)
        for j in _row_order(reverse):
            nr = ar * sr - ai * si + xr[j:j + 1, :]
            ni = ar * si + ai * sr + xi[j:j + 1, :]
            out_r = jnp.where(rowid == j, nr, out_r)
            out_i = jnp.where(rowid == j, ni, out_i)
            sr, si = nr, ni
        s_ref[pl.ds(base, SUBLANES), :] = jnp.concatenate([out_r, out_i], axis=1)
        return sr, si

    sr, si = _scan_tiles(tt // SUBLANES, reverse, tile, (carry[0:1, :], carry[1:2, :]))
    carry[0:1, :] = sr
    carry[1:2, :] = si


def _s5_fwd_body(a_ref, bu_ref, s_ref, carry, *, tt, w):
    d = pl.program_id(0)

    @pl.when(pl.program_id(1) == 0)
    def _():
        carry[...] = jnp.zeros_like(carry)

    @pl.when(d == 0)
    def _():
        _s5_fwd_dir(a_ref, bu_ref, s_ref, carry, reverse=False, tt=tt, w=w)

    @pl.when(d == 1)
    def _():
        _s5_fwd_dir(a_ref, bu_ref, s_ref, carry, reverse=True, tt=tt, w=w)


def _s5_bwd_dir(a_ref, s_ref, ds_ref, dbu_ref, da_ref, carry, acc, *, reverse, tt, w):
    ar, ai = a_ref[0:1, :], a_ref[1:2, :]
    rowid = lax.broadcasted_iota(jnp.int32, (SUBLANES, w), 0)

    first = _row_order(reverse)[0]
    acc[...] = jnp.zeros_like(acc)

    def tile(base, c):
        lr, li = lam_in_r, lam_in_i = c
        s = s_ref[pl.ds(base, SUBLANES), :]
        g = ds_ref[pl.ds(base, SUBLANES), :]
        out_r = jnp.zeros((SUBLANES, w), F32)
        out_i = jnp.zeros((SUBLANES, w), F32)
        for j in _row_order(reverse):
            nr = g[j:j + 1, :w] + ar * lr + ai * li
            ni = g[j:j + 1, w:] + ar * li - ai * lr
            out_r = jnp.where(rowid == j, nr, out_r)
            out_i = jnp.where(rowid == j, ni, out_i)
            lr, li = nr, ni
        dbu_ref[pl.ds(base, SUBLANES), :] = jnp.concatenate([out_r, out_i], axis=1)
        shift = SUBLANES - 1 if reverse else 1
        next_r = jnp.where(rowid == first, lam_in_r, pltpu.roll(out_r, shift, axis=0))
        next_i = jnp.where(rowid == first, lam_in_i, pltpu.roll(out_i, shift, axis=0))
        sr, si = s[:, :w], s[:, w:]
        acc[0] += sr * next_r + si * next_i
        acc[1] += sr * next_i - si * next_r
        return lr, li

    lr, li = _scan_tiles(tt // SUBLANES, reverse, tile, (carry[0:1, :], carry[1:2, :]))
    carry[0:1, :] = lr
    carry[1:2, :] = li
    da_ref[0:1, :] += jnp.sum(acc[0], axis=0, keepdims=True)
    da_ref[1:2, :] += jnp.sum(acc[1], axis=0, keepdims=True)


def _s5_bwd_body(a_ref, s_ref, ds_ref, dbu_ref, da_ref, carry, acc, *, tt, w):
    d = pl.program_id(0)

    @pl.when(pl.program_id(1) == 0)
    def _():
        carry[...] = jnp.zeros_like(carry)
        da_ref[...] = jnp.zeros_like(da_ref)

    @pl.when(d == 0)
    def _():
        _s5_bwd_dir(a_ref, s_ref, ds_ref, dbu_ref, da_ref, carry, acc, reverse=True, tt=tt, w=w)

    @pl.when(d == 1)
    def _():
        _s5_bwd_dir(a_ref, s_ref, ds_ref, dbu_ref, da_ref, carry, acc, reverse=False, tt=tt, w=w)


def _s5_time_map(nt, flip_dir):
    def time_block(d, t):
        back = nt - 1 - t
        return jnp.where(d == flip_dir, back, t)

    return time_block


def _s5_fwd_call(a, bu):
    seq, w4 = bu.shape
    w = w4 // 4
    tt = _pick(seq, 256, SUBLANES)
    nt = seq // tt
    tb = _s5_time_map(nt, 1)
    return pl.pallas_call(
        functools.partial(_s5_fwd_body, tt=tt, w=w),
        name="s5_scan_fwd",
        grid=(2, nt),
        in_specs=[
            pl.BlockSpec((None, 2, w), lambda d, t: (d, 0, 0)),
            pl.BlockSpec((tt, 2 * w), lambda d, t: (tb(d, t), d)),
        ],
        out_specs=pl.BlockSpec((tt, 2 * w), lambda d, t: (tb(d, t), d)),
        out_shape=jax.ShapeDtypeStruct(bu.shape, F32),
        scratch_shapes=[pltpu.VMEM((2, w), F32)],
        compiler_params=_cparams("parallel", "arbitrary"),
    )(a, bu)


def _s5_bwd_call(a, s_prev, ds):
    seq, w4 = ds.shape
    w = w4 // 4
    tt = _pick(seq, 256, SUBLANES)
    nt = seq // tt
    tb = _s5_time_map(nt, 0)
    blk = pl.BlockSpec((tt, 2 * w), lambda d, t: (tb(d, t), d))
    return pl.pallas_call(
        functools.partial(_s5_bwd_body, tt=tt, w=w),
        name="s5_scan_bwd",
        grid=(2, nt),
        in_specs=[pl.BlockSpec((None, 2, w), lambda d, t: (d, 0, 0)), blk, blk],
        out_specs=[blk, pl.BlockSpec((None, 2, w), lambda d, t: (d, 0, 0))],
        out_shape=[jax.ShapeDtypeStruct(ds.shape, F32), jax.ShapeDtypeStruct(a.shape, F32)],
        scratch_shapes=[pltpu.VMEM((2, w), F32), pltpu.VMEM((2, SUBLANES, w), F32)],
        compiler_params=_cparams("parallel", "arbitrary"),
    )(a, s_prev, ds)


@jax.custom_vjp
def s5_scan(a, bu):
    return _s5_fwd_call(a, bu)


def _s5_scan_fwd(a, bu):
    s = _s5_fwd_call(a, bu)
    return s, (a, s)


def _s5_scan_bwd(res, ds):
    a, s = res
    dbu, da = _s5_bwd_call(a, s, ds)
    return da, dbu


s5_scan.defvjp(_s5_scan_fwd, _s5_scan_bwd)


NN, NT, TN = ((1,), (0,)), ((1,), (1,)), ((0,), (0,))


def _bmm(a, b, dims):
    return jnp.stack([_bdot(a[i], b[i], dims) for i in range(a.shape[0])])


def _bmm3(a, b):
    ah = a.astype(BF16)
    bh = b.astype(BF16)
    al = a - ah.astype(F32)
    bl = b - bh.astype(F32)
    return _bmm(ah, bh, NN) + _bmm(ah, bl, NN) + _bmm(al, bh, NN)


GDN_INV_BASE = 8


def _unit_triangular_inverse(a, ri, ci):
    c = a.shape[-1]

    def same_block(size):
        shift = int(math.log2(size))
        return lax.shift_right_logical(ri, shift) == lax.shift_right_logical(ci, shift)

    diag = jnp.where(same_block(GDN_INV_BASE), a, 0.0)
    inv = jnp.where(ri == ci, 1.0, 0.0) - diag
    power = diag
    for _ in range(int(math.log2(GDN_INV_BASE)) - 1):
        power = _bmm3(power, power)
        inv = inv + _bmm3(inv, power)
    size = GDN_INV_BASE
    while size < c:
        off = jnp.where(jnp.logical_and(same_block(2 * size), jnp.logical_not(same_block(size))), a, 0.0)
        inv = inv - _bmm3(_bmm3(inv, off), inv)
        size *= 2
    return inv


def _gdn_chunks(q, k, v, gr, br, state, *, n_fwd):
    b, c, _ = q.shape
    dv = v.shape[2]
    ri = lax.broadcasted_iota(jnp.int32, (b, c, c), 1)
    ci = lax.broadcasted_iota(jnp.int32, (b, c, c), 2)
    fwd = jnp.stack([jnp.full((c, c), 1 if i < n_fwd else 0, jnp.int32) for i in range(b)]) == 1
    lower = jnp.logical_or(jnp.logical_and(fwd, ri >= ci), jnp.logical_and(jnp.logical_not(fwd), ri <= ci))
    strict = jnp.logical_and(lower, ri != ci)
    as_column = lambda row: jnp.sum(jnp.where(ri == ci, row, 0.0), axis=2, keepdims=True)
    gc, bc = as_column(gr), as_column(br)
    kb = k * bc
    decay = jnp.where(lower, jnp.exp(jnp.where(lower, gc - gr, 0.0)), 0.0)
    a = jnp.where(strict, _bmm(kb, k, NT) * decay, 0.0)
    inv = _unit_triangular_inverse(a, ri, ci)
    eg = jnp.exp(gc)
    sol = _bmm3(inv, jnp.concatenate([v * bc, kb * eg], axis=2))
    u, w = sol[:, :, :dv], sol[:, :, dv:]
    attn = _bmm(q, k, NT) * decay
    v_new = u - _bmm(w, state, NN)
    o = _bmm(q * eg, state, NN) + _bmm(attn, v_new, NN)
    row = lax.broadcasted_iota(jnp.int32, (b, c, 1), 1)
    fwd_col = jnp.stack([jnp.full((c, 1), 1 if i < n_fwd else 0, jnp.int32) for i in range(b)]) == 1
    last = jnp.logical_or(jnp.logical_and(fwd_col, row == c - 1), jnp.logical_and(jnp.logical_not(fwd_col), row == 0))
    g_last = jnp.sum(jnp.where(last, gc, 0.0), axis=1, keepdims=True)
    new_state = state * jnp.exp(g_last) + _bmm(k * jnp.exp(g_last - gc), v_new, TN)
    return o, new_state


def _gdn_heads(refs, heads, dh):
    return jnp.stack([r[:, h * dh:(h + 1) * dh] for r in refs for h in range(heads)])


def _gdn_operands(q_refs, k_refs, v_refs, gr_refs, br_refs, heads, dh):
    rows = lambda refs: jnp.concatenate([r[...] for r in refs], axis=0)
    return (_gdn_heads(q_refs, heads, dh), _gdn_heads(k_refs, heads, dh), _gdn_heads(v_refs, heads, dh),
            rows(gr_refs), rows(br_refs))


def _gdn_fwd_body(qf, kf, vf, qb, kb, vb, grf, brf, grb, brb, of, ob, s0f, s0b, state, *, heads, dh):
    @pl.when(pl.program_id(0) == 0)
    def _():
        state[...] = jnp.zeros_like(state)

    s0 = state[...]
    s0f[...] = s0[:heads]
    s0b[...] = s0[heads:]
    o, new_state = _gdn_chunks(*_gdn_operands((qf, qb), (kf, kb), (vf, vb), (grf, grb), (brf, brb), heads, dh), s0,
                               n_fwd=heads)
    state[...] = new_state
    for i in range(2 * heads):
        (of, ob)[i // heads][:, (i % heads) * dh:(i % heads + 1) * dh] = o[i]


def _gdn_bwd_body(qf, kf, vf, qb, kb, vb, grf, brf, grb, brb, s0f, s0b, dof, dob,
                  dqf, dkf, dvf, dqb, dkb, dvb, dgrf, dbrf, dgrb, dbrb, dstate, *, heads, dh):
    @pl.when(pl.program_id(0) == 0)
    def _():
        dstate[...] = jnp.zeros_like(dstate)

    ops = _gdn_operands((qf, qb), (kf, kb), (vf, vb), (grf, grb), (brf, brb), heads, dh)
    s0 = jnp.concatenate([s0f[...], s0b[...]], axis=0)
    do = _gdn_heads((dof, dob), heads, dh)
    _, vjp = jax.vjp(functools.partial(_gdn_chunks, n_fwd=heads), *ops, s0)
    dq, dk, dv, dgr, dbr, ds0 = vjp((do, dstate[...]))
    dstate[...] = ds0
    for i in range(2 * heads):
        d, cols = i // heads, slice((i % heads) * dh, (i % heads + 1) * dh)
        (dqf, dqb)[d][:, cols] = dq[i]
        (dkf, dkb)[d][:, cols] = dk[i]
        (dvf, dvb)[d][:, cols] = dv[i]
    dgrf[...] = dgr[:heads]
    dgrb[...] = dgr[heads:]
    dbrf[...] = dbr[:heads]
    dbrb[...] = dbr[heads:]


def _gdn_specs(nc, heads, dh, backward):
    c = GDN_CHUNK
    up = lambda n: n
    down = lambda n: nc - 1 - n
    out = []
    for chunk in ((down, up) if backward else (up, down)):
        out.append((
            pl.BlockSpec((c, heads * dh), lambda n, chunk=chunk: (chunk(n), 0)),
            pl.BlockSpec((heads, None, 1, c), lambda n, chunk=chunk: (0, chunk(n), 0, 0)),
            pl.BlockSpec((heads, None, dh, dh), lambda n, chunk=chunk: (0, chunk(n), 0, 0)),
        ))
    return out


def _gdn_fwd_call(q, k, v, grf, brf, grb, brb):
    seq, width = q.shape
    heads = grf.shape[0]
    dh = width // heads
    nc = seq // GDN_CHUNK
    (seq_f, row_f, st_f), (seq_b, row_b, st_b) = _gdn_specs(nc, heads, dh, False)
    states = jax.ShapeDtypeStruct((heads, nc, dh, dh), F32)
    return pl.pallas_call(
        functools.partial(_gdn_fwd_body, heads=heads, dh=dh),
        name="gdn_fwd",
        grid=(nc,),
        in_specs=[seq_f] * 3 + [seq_b] * 3 + [row_f, row_f, row_b, row_b],
        out_specs=[seq_f, seq_b, st_f, st_b],
        out_shape=[jax.ShapeDtypeStruct(q.shape, F32)] * 2 + [states] * 2,
        scratch_shapes=[pltpu.VMEM((2 * heads, dh, dh), F32)],
        compiler_params=_cparams("arbitrary"),
    )(q, k, v, q, k, v, grf, brf, grb, brb)


def _gdn_bwd_call(q, k, v, grf, brf, grb, brb, s0f, s0b, dof, dob):
    seq, width = q.shape
    heads = grf.shape[0]
    dh = width // heads
    nc = seq // GDN_CHUNK
    (seq_f, row_f, st_f), (seq_b, row_b, st_b) = _gdn_specs(nc, heads, dh, True)
    like = lambda t: jax.ShapeDtypeStruct(t.shape, F32)
    return pl.pallas_call(
        functools.partial(_gdn_bwd_body, heads=heads, dh=dh),
        name="gdn_bwd",
        grid=(nc,),
        in_specs=[seq_f] * 3 + [seq_b] * 3 + [row_f, row_f, row_b, row_b, st_f, st_b, seq_f, seq_b],
        out_specs=[seq_f] * 3 + [seq_b] * 3 + [row_f, row_f, row_b, row_b],
        out_shape=[like(q)] * 6 + [like(grf), like(brf), like(grb), like(brb)],
        scratch_shapes=[pltpu.VMEM((2 * heads, dh, dh), F32)],
        compiler_params=_cparams("arbitrary"),
    )(q, k, v, q, k, v, grf, brf, grb, brb, s0f, s0b, dof, dob)


@jax.custom_vjp
def gdn_delta_rule(q, k, v, grf, brf, grb, brb):
    return tuple(_gdn_fwd_call(q, k, v, grf, brf, grb, brb)[:2])


def _gdn_delta_rule_fwd(*ops):
    of, ob, s0f, s0b = _gdn_fwd_call(*ops)
    return (of, ob), (*ops, s0f, s0b)


def _gdn_delta_rule_bwd(res, do):
    dqf, dkf, dvf, dqb, dkb, dvb, *small = _gdn_bwd_call(*res, *do)
    return (dqf + dqb, dkf + dkb, dvf + dvb, *small)


gdn_delta_rule.defvjp(_gdn_delta_rule_fwd, _gdn_delta_rule_bwd)


def _mesh_position():
    return lax.axis_index("x"), lax.axis_index("y"), lax.axis_index("c")


def _all_gather_body(x_ref, out_ref, send_sems, recv_sems, local_sem):
    x, y, c = _mesh_position()
    me, sibling = (x, y, c), (x, y, 1 - c)
    chips = [(1 - x, y), (x, 1 - y), (1 - x, 1 - y)]

    def slot(px, py, pc):
        return out_ref.at[4 * px + 2 * py + pc]

    def copy(k, block, to, src=None):
        return pltpu.make_async_remote_copy(
            src_ref=slot(*block) if src is None else src, dst_ref=slot(*block),
            send_sem=send_sems.at[k], recv_sem=recv_sems.at[k], device_id=to, device_id_type=pl.DeviceIdType.MESH)

    mine = pltpu.make_async_copy(x_ref, slot(*me), local_sem)
    mine.start()
    first = [copy(0, me, sibling, src=x_ref)]
    first += [copy(1 + j, me, (*chip, c), src=x_ref) for j, chip in enumerate(chips)]
    for cp in first:
        cp.start()
    passed = [copy(4 + j, (*chip, c), sibling) for j, chip in enumerate(chips)]
    for j, chip in enumerate(chips):
        copy(1 + j, (*chip, c), me).wait_recv()
        passed[j].start()
    copy(0, sibling, me).wait_recv()
    for j, chip in enumerate(chips):
        copy(4 + j, (*chip, 1 - c), me).wait_recv()
    for cp in first + passed:
        cp.wait_send()
    mine.wait()


def all_gather(x, name):
    return pl.pallas_call(
        functools.partial(_all_gather_body),
        name=name,
        in_specs=[pl.BlockSpec(memory_space=pl.ANY)],
        out_specs=pl.BlockSpec(memory_space=pl.ANY),
        out_shape=jax.ShapeDtypeStruct((N_DEV,) + x.shape, x.dtype),
        scratch_shapes=[pltpu.SemaphoreType.DMA((N_DEV - 1,)), pltpu.SemaphoreType.DMA((N_DEV - 1,)), pltpu.SemaphoreType.DMA],
    )(x)


N_CHIPS = 4


def _swap_cores_body(x_ref, got_ref, send_sem, recv_sem):
    x, y, c = _mesh_position()
    swap = pltpu.make_async_remote_copy(
        src_ref=x_ref.at[1 - c], dst_ref=got_ref, send_sem=send_sem, recv_sem=recv_sem,
        device_id=(x, y, 1 - c), device_id_type=pl.DeviceIdType.MESH)
    swap.start()
    swap.wait()


def swap_cores(x, name):
    return pl.pallas_call(
        functools.partial(_swap_cores_body),
        name=name,
        in_specs=[pl.BlockSpec(memory_space=pl.ANY)],
        out_specs=pl.BlockSpec(memory_space=pl.ANY),
        out_shape=jax.ShapeDtypeStruct(x.shape[1:], x.dtype),
        scratch_shapes=[pltpu.SemaphoreType.DMA, pltpu.SemaphoreType.DMA],
    )(x)


def _exchange_chips_body(x_ref, out_ref, send_sems, recv_sems, local_sem):
    x, y, c = _mesh_position()
    my_chip = 2 * x + y
    copies = []
    for m in range(1, N_CHIPS):
        px = 1 - x if m & 2 else x
        py = 1 - y if m & 1 else y
        copies.append(pltpu.make_async_remote_copy(
            src_ref=x_ref.at[2 * px + py], dst_ref=out_ref.at[my_chip],
            send_sem=send_sems.at[m - 1], recv_sem=recv_sems.at[m - 1],
            device_id=(px, py, c), device_id_type=pl.DeviceIdType.MESH))
    for cp in copies:
        cp.start()
    mine = pltpu.make_async_copy(x_ref.at[my_chip], out_ref.at[my_chip], local_sem)
    mine.start()
    for cp in copies:
        cp.wait()
    mine.wait()


def exchange_chips(x, name):
    return pl.pallas_call(
        functools.partial(_exchange_chips_body),
        name=name,
        in_specs=[pl.BlockSpec(memory_space=pl.ANY)],
        out_specs=pl.BlockSpec(memory_space=pl.ANY),
        out_shape=jax.ShapeDtypeStruct(x.shape, x.dtype),
        scratch_shapes=[pltpu.SemaphoreType.DMA((N_CHIPS - 1,)), pltpu.SemaphoreType.DMA((N_CHIPS - 1,)),
                        pltpu.SemaphoreType.DMA],
    )(x)


def _add_pairs_body(a_ref, b_ref, o_ref):
    o_ref[...] = (a_ref[...].astype(F32) + b_ref[...].astype(F32)).astype(o_ref.dtype)


def add_pairs(a, b):
    n, rows, cols = a.shape
    tr = _pick(rows, 512, 2 * SUBLANES)
    spec = pl.BlockSpec((None, tr, cols), lambda k, i: (k, i, 0))
    return pl.pallas_call(
        functools.partial(_add_pairs_body),
        name="add_pairs",
        grid=(n, rows // tr),
        in_specs=[spec, spec],
        out_specs=spec,
        out_shape=jax.ShapeDtypeStruct(a.shape, a.dtype),
        compiler_params=_cparams("parallel", "parallel"),
    )(a, b)


def reduce_scatter(parts):
    got = swap_cores(parts, "exchange_grads_cores")
    own = lax.dynamic_index_in_dim(parts, lax.axis_index("c"), axis=0, keepdims=False)
    return sum_leading(exchange_chips(add_pairs(own, got), "exchange_grads_chips"))


PACK_COLS = 1024
PACK_ROW_MULTIPLE = 512


def _pack(arrays, dtype):
    parts = []
    rows = 0
    for a in arrays:
        flat = a.reshape(-1).astype(dtype)
        n_rows = -(-flat.shape[0] // PACK_COLS)
        parts.append(jnp.pad(flat, (0, n_rows * PACK_COLS - flat.shape[0])).reshape(n_rows, PACK_COLS))
        rows += n_rows
    pad_rows = -rows % PACK_ROW_MULTIPLE
    if pad_rows:
        parts.append(jnp.zeros((pad_rows, PACK_COLS), dtype))
    return jnp.concatenate(parts, axis=0)


def _unpack(flat, shapes):
    lead = flat.shape[:-2]
    out = []
    row = 0
    for shape in shapes:
        size = int(np.prod(shape))
        n_rows = -(-size // PACK_COLS)
        seg = flat[..., row:row + n_rows, :].reshape(lead + (n_rows * PACK_COLS,))[..., :size]
        out.append(seg.reshape(lead + tuple(shape)))
        row += n_rows
    return out


W_IN_SMALL = 4608


def _split_w_in(w_in):
    d = w_in.shape[0]
    small = jnp.concatenate(
        [w_in[:, 0:2560], w_in[:, 2576:4304], w_in[:, 2560:2576], jnp.zeros((d, W_IN_SMALL - 4304), w_in.dtype)], axis=1)
    return small, w_in[:, 4304:]


def _s5_mixer(u, lam_re, lam_im, log_step, b_re, b_im, c_re, c_im, d_skip, w_glu, b_glu):
    g, p, hg = S5_GROUPS, S5_STATE, S5_GROUP
    lam_re = jnp.minimum(lam_re, -1e-4)
    dt = jnp.exp(log_step)[..., None]
    mag = jnp.exp(lam_re * dt)
    abar_r = mag * jnp.cos(lam_im * dt)
    abar_i = mag * jnp.sin(lam_im * dt)
    den = lam_re * lam_re + lam_im * lam_im
    xr = abar_r - 1.0
    xi = abar_i
    coef_r = (xr * lam_re + xi * lam_im) / den
    coef_i = (xi * lam_re - xr * lam_im) / den
    bbar_r = coef_r[..., None] * b_re - coef_i[..., None] * b_im
    bbar_i = coef_r[..., None] * b_im + coef_i[..., None] * b_re
    eye = jnp.eye(g, dtype=F32)

    def block_diag(t, rows, cols):
        return (eye[:, None, :, None] * t[:, :, None, :]).reshape(g * rows, g * cols)

    b_all = jnp.concatenate(
        [block_diag(t.transpose(0, 2, 1), hg, p) for t in (bbar_r[0], bbar_i[0], bbar_r[1], bbar_i[1])], axis=1)
    c_all = jnp.concatenate(
        [block_diag(t.transpose(0, 2, 1), p, hg) for t in (c_re[0], -c_im[0], c_re[1], -c_im[1])], axis=0)
    a_all = jnp.stack([abar_r.reshape(2, g * p), abar_i.reshape(2, g * p)], axis=1)
    s = s5_scan(a_all, mm(u, b_all))
    y = mm(s, c_all) + d_skip * u
    y = jax.nn.gelu(y)
    return y * jax.nn.sigmoid(wmm(y, *w_glu) + b_glu)


def _gdn_mixer(qkv, z, beta_logits, decay_logits, conv_w, a_log, dt_bias, o_gain):
    seq = qkv.shape[0]
    h, dh, c = GDN_HEADS, GDN_HEAD_DIM, GDN_CHUNK
    nc = seq // c
    padded = jnp.pad(qkv, ((GDN_CONV // 2, GDN_CONV - 1 - GDN_CONV // 2), (0, 0)))
    conv = sum(padded[j:j + seq] * conv_w[j] for j in range(GDN_CONV))
    q, k, v = jnp.split(jax.nn.silu(conv), 3, axis=-1)

    def l2(t):
        t = t.reshape(seq, h, dh)
        return (t * lax.rsqrt(jnp.sum(t * t, axis=-1, keepdims=True) + 1e-6)).reshape(seq, h * dh)

    q = l2(q) * (dh**-0.5)
    k = l2(k)
    beta = jax.nn.sigmoid(beta_logits).reshape(seq, 2, h)
    g = -jnp.exp(a_log) * jax.nn.softplus(decay_logits.reshape(seq, 2, h) + dt_bias)
    small = []
    for d in range(2):
        gcs = lax.cumsum(g[:, d].reshape(nc, c, h), axis=1, reverse=d == 1).transpose(2, 0, 1)
        small += [gcs.reshape(h, nc, 1, c), beta[:, d].T.reshape(h, nc, 1, c)]
    o_fwd, o_bwd = gdn_delta_rule(q, k, v, *small)
    o = rmsnorm((o_fwd + o_bwd).reshape(seq * h, dh), o_gain).reshape(seq, h * dh)
    return o * jax.nn.silu(z)


def _t5_bucket(rel):
    nb = T5_BUCKETS // 2
    max_exact = nb // 2
    ret = jnp.where(rel > 0, nb, 0)
    n = jnp.abs(rel)
    nf = jnp.maximum(n, 1).astype(F32)
    large = max_exact + (jnp.log(nf / max_exact) / math.log(T5_MAX_DISTANCE / max_exact) * (nb - max_exact)).astype(jnp.int32)
    large = jnp.minimum(large, nb - 1)
    return ret + jnp.where(n < max_exact, n, large)


def _swa_mixer(q, kv, sink, t5_bias):
    seq = q.shape[0]
    kvh, g, d, blk = SWA_KV_HEADS, SWA_GROUP, SWA_HEAD_DIM, SWA_BLOCK
    q4 = q.reshape(seq, kvh, g, d).transpose(1, 2, 0, 3)
    k, v = jnp.split(kv, 2, axis=-1)
    heads_first = lambda t: t.reshape(seq, kvh, d).transpose(1, 0, 2)
    rel = jnp.arange(3 * blk)[None, :] - blk - jnp.arange(blk)[:, None]
    onehot = (_t5_bucket(rel)[..., None] == jnp.arange(T5_BUCKETS)).astype(F32)
    bias = jnp.einsum("qsb,bh->hqs", onehot, t5_bias, precision=lax.Precision.HIGHEST).reshape(kvh, g * blk, 3 * blk)
    sink_col = jnp.broadcast_to(sink.reshape(kvh, g, 1, 1), (kvh, g, blk, 1)).reshape(kvh, g * blk, 1)
    o = swa_attention(q4, heads_first(k), heads_first(v), bias, sink_col)
    return o.transpose(2, 0, 1, 3).reshape(seq, kvh * g * d)


def _rope(t, cos, sin):
    t1, t2 = jnp.split(t, 2, axis=-1)
    return jnp.concatenate([t1 * cos - t2 * sin, t2 * cos + t1 * sin], axis=-1)


def _mla_mixer(c_q, c_kv, k_rope, q_gain, kv_gain, w_uq, w_ukv):
    seq = c_q.shape[0]
    h = MLA_HEADS
    q = wmm(rmsnorm(c_q, q_gain), *w_uq).reshape(seq, h, MLA_NOPE + MLA_ROPE)
    kv = wmm(rmsnorm(c_kv, kv_gain), *w_ukv).reshape(seq, h, MLA_NOPE + MLA_V)
    q_nope, q_pe = q[..., :MLA_NOPE], q[..., MLA_NOPE:]
    k_nope, v = kv[..., :MLA_NOPE], kv[..., MLA_NOPE:]
    pos = jnp.arange(seq, dtype=F32)
    inv_freq = ROPE_THETA ** (-jnp.arange(0, MLA_ROPE, 2, dtype=F32) / MLA_ROPE)
    ang = pos[:, None] * inv_freq[None, :]
    cos, sin = jnp.cos(ang)[:, None, :], jnp.sin(ang)[:, None, :]
    q_pe = _rope(q_pe, cos, sin)
    k_pe = _rope(k_rope[:, None, :], cos, sin)
    qf = jnp.concatenate([q_nope, q_pe], axis=-1)
    kf = jnp.concatenate([k_nope, jnp.broadcast_to(k_pe, (seq, h, MLA_ROPE))], axis=-1)
    o = mla_attention(qf.transpose(1, 0, 2), kf.transpose(1, 0, 2), v.transpose(1, 0, 2))
    return o.transpose(1, 0, 2).reshape(seq, h * MLA_V)


def _layer(x, p, t5_bias):
    d = x.shape[1]
    h = rmsnorm(x, p["mix_pre_gain"])
    (w_small, w_gate), (w_small_bf16, w_gate_bf16) = (_split_w_in(t) for t in p["w_in"])
    ps = wmm(h, w_small, w_small_bf16)
    gate_logits = wmm(h, w_gate, w_gate_bf16)
    y_a = _s5_mixer(ps[:, 0:512], p["s5_lam_re"], p["s5_lam_im"], p["s5_log_step"], p["s5_b_re"], p["s5_b_im"],
                    p["s5_c_re"], p["s5_c_im"], p["s5_d"], p["s5_w_glu"], p["s5_b_glu"])
    y_b = _gdn_mixer(ps[:, 512:2048], ps[:, 2048:2560], ps[:, 4288:4296], ps[:, 4296:4304], p["gdn_conv"],
                     p["gdn_a_log"], p["gdn_dt_bias"], p["gdn_o_gain"])
    y_c = _swa_mixer(ps[:, 2560:3072], ps[:, 3072:3328], p["swa_sink"], t5_bias)
    y_d = _mla_mixer(ps[:, 3328:3712], ps[:, 3712:4224], ps[:, 4224:4288], p["mla_q_gain"], p["mla_kv_gain"],
                     p["mla_w_uq"], p["mla_w_ukv"])
    merged = sum(jax.nn.sigmoid(gate_logits[:, b * d:(b + 1) * d]) * wmm(y, p["w_branch"][0][b], p["w_branch"][1][b])
                 for b, y in enumerate((y_a, y_b, y_c, y_d)))
    x = x + rmsnorm(wmm(merged, *p["w_out"]), p["mix_post_gain"])
    h = rmsnorm(x, p["mlp_pre_gain"])
    f = wmm(jnp.square(jax.nn.relu(wmm(h, *p["w_mlp_in"]))), *p["w_mlp_out"])
    return x + rmsnorm(f, p["mlp_post_gain"])


LAYER_WEIGHTS = ("w_in", "s5_lam_re", "s5_lam_im", "s5_log_step", "s5_b_re", "s5_b_im", "s5_c_re", "s5_c_im", "s5_d",
                 "s5_w_glu", "s5_b_glu", "gdn_conv", "gdn_a_log", "gdn_dt_bias", "gdn_o_gain", "swa_sink", "mla_q_gain",
                 "mla_kv_gain", "mla_w_uq", "mla_w_ukv", "w_branch", "w_out", "mix_pre_gain", "mix_post_gain",
                 "mlp_pre_gain", "mlp_post_gain", "w_mlp_in", "w_mlp_out")


def _forward(x, weights, values):
    for layer in range(DEPTH):
        p = {n: (weights[n][layer], values[n][layer]) if n in values else weights[n][layer] for n in LAYER_WEIGHTS}
        x = _layer(x, p, weights["t5_bias"])
    return x


WEIGHT_NAMES = ("w_in", "s5_lam_re", "s5_lam_im", "s5_log_step", "s5_b_re", "s5_b_im", "s5_c_re", "s5_c_im", "s5_d",
                "s5_w_glu", "s5_b_glu", "gdn_conv", "gdn_a_log", "gdn_dt_bias", "gdn_o_gain", "swa_sink", "t5_bias",
                "mla_q_gain", "mla_kv_gain", "mla_w_uq", "mla_w_ukv", "w_branch", "w_out", "mix_pre_gain", "mix_post_gain",
                "mlp_pre_gain", "mlp_post_gain", "w_mlp_in", "w_mlp_out")
SHARD_AXIS = {"w_in": 2, "s5_w_glu": 1, "mla_w_uq": 2, "mla_w_ukv": 2, "w_branch": 3, "w_out": 1, "w_mlp_in": 2,
              "w_mlp_out": 1}
CONV = "gdn_conv"
CONV_AXIS = 2
REPLICATED = tuple(n for n in WEIGHT_NAMES if n not in SHARD_AXIS and n != CONV)


def _step(x, target, w, m, v):
    big = tuple(SHARD_AXIS)
    x_pos, y_pos, c_pos = _mesh_position()
    my = 4 * x_pos + 2 * y_pos + c_pos

    shard_shapes = [w[n].shape for n in big]
    gathered = _unpack(all_gather(_pack([w[n] for n in big], BF16), "gather_weights"), shard_shapes)
    values = {n: jnp.concatenate([g[k] for k in range(N_DEV)], axis=SHARD_AXIS[n]) for n, g in zip(big, gathered)}
    full = {n: t.astype(F32) for n, t in values.items()}
    conv_all = _unpack(all_gather(_pack([w[CONV]], F32), "gather_conv"), [w[CONV].shape])[0]
    full[CONV] = jnp.concatenate([conv_all[k] for k in range(N_DEV)], axis=CONV_AXIS)
    for n in REPLICATED:
        full[n] = w[n]

    y, vjp = jax.vjp(lambda x_, full_: _forward(x_, full_, values), x, full)
    loss_rows, dy = loss_head(y, target)
    grad_x, grad_full = vjp(dy)
    loss_part = jnp.sum(loss_rows)

    def pieces(n, k):
        size = w[n].shape[SHARD_AXIS[n]]
        return lax.slice_in_dim(grad_full[n], k * size, (k + 1) * size, axis=SHARD_AXIS[n])

    send = jnp.stack([jnp.stack([_pack([pieces(n, 2 * chip + core) for n in big], BF16) for chip in range(N_CHIPS)])
                      for core in range(2)])
    grad_big = reduce_scatter(send)
    small_names = REPLICATED + (CONV,)
    small = [grad_full[n] for n in small_names] + [loss_part.reshape(1)]
    small_sum = sum_leading(all_gather(_pack(small, F32), "gather_small_grads"))
    small_grads = _unpack(small_sum, [a.shape for a in small])
    loss = small_grads[-1][0]
    grads = dict(zip(small_names, small_grads[:-1]))
    conv_size = w[CONV].shape[CONV_AXIS]
    grads[CONV] = lax.dynamic_slice_in_dim(grads[CONV], my * conv_size, conv_size, axis=CONV_AXIS)
    grads.update(zip(big, _unpack(grad_big, shard_shapes)))

    small_shapes = [w[n].shape for n in small_names]
    packed = lambda t: _pack([t[n] for n in small_names], F32)
    delta, new_m, new_v = (dict(zip(small_names, _unpack(t, small_shapes)))
                           for t in adamw_flat(packed(w), packed(grads), packed(m), packed(v)))
    for n in big:
        rows2d = lambda t: t.reshape(-1, t.shape[-1])
        outs = adamw_flat(rows2d(w[n]), rows2d(grads[n]), rows2d(m[n]), rows2d(v[n]))
        delta[n], new_m[n], new_v[n] = (t.reshape(w[n].shape) for t in outs)
    return loss, grad_x, grads, delta, new_m, new_v


def kernel(x, w_in, s5_lam_re, s5_lam_im, s5_log_step, s5_b_re, s5_b_im, s5_c_re, s5_c_im, s5_d, s5_w_glu, s5_b_glu, gdn_conv, gdn_a_log, gdn_dt_bias, gdn_o_gain, swa_sink, t5_bias, mla_q_gain, mla_kv_gain, mla_w_uq, mla_w_ukv, w_branch, w_out, mix_pre_gain, mix_post_gain, mlp_pre_gain, mlp_post_gain, w_mlp_in, w_mlp_out, loss_target, m_w_in, m_s5_lam_re, m_s5_lam_im, m_s5_log_step, m_s5_b_re, m_s5_b_im, m_s5_c_re, m_s5_c_im, m_s5_d, m_s5_w_glu, m_s5_b_glu, m_gdn_conv, m_gdn_a_log, m_gdn_dt_bias, m_gdn_o_gain, m_swa_sink, m_t5_bias, m_mla_q_gain, m_mla_kv_gain, m_mla_w_uq, m_mla_w_ukv, m_w_branch, m_w_out, m_mix_pre_gain, m_mix_post_gain, m_mlp_pre_gain, m_mlp_post_gain, m_w_mlp_in, m_w_mlp_out, v_w_in, v_s5_lam_re, v_s5_lam_im, v_s5_log_step, v_s5_b_re, v_s5_b_im, v_s5_c_re, v_s5_c_im, v_s5_d, v_s5_w_glu, v_s5_b_glu, v_gdn_conv, v_gdn_a_log, v_gdn_dt_bias, v_gdn_o_gain, v_swa_sink, v_t5_bias, v_mla_q_gain, v_mla_kv_gain, v_mla_w_uq, v_mla_w_ukv, v_w_branch, v_w_out, v_mix_pre_gain, v_mix_post_gain, v_mlp_pre_gain, v_mlp_post_gain, v_w_mlp_in, v_w_mlp_out):
    args = locals()
    w = {n: args[n] for n in WEIGHT_NAMES}
    m = {n: args["m_" + n] for n in WEIGHT_NAMES}
    v = {n: args["v_" + n] for n in WEIGHT_NAMES}
    loss, grad_x, grads, delta, new_m, new_v = _step(x[0], loss_target[0], w, m, v)
    return (loss, grad_x[None], *[grads[n] for n in WEIGHT_NAMES], *[delta[n] for n in WEIGHT_NAMES],
            *[new_m[n] for n in WEIGHT_NAMES], *[new_v[n] for n in WEIGHT_NAMES])
```

```python
import functools
import math

import jax
import jax.numpy as jnp
import numpy as np
from jax import lax
from jax.experimental import pallas as pl
from jax.experimental.pallas import tpu as pltpu

F32 = jnp.float32
BF16 = jnp.bfloat16

VMEM_LIMIT_BYTES = 56 * 1024 * 1024
LANES = 128
SUBLANES = 8

N_DEV = 8
MESH_AXES = ("x", "y", "c")

DEPTH = 4
N_BRANCHES = 4
BRANCH_WIDTH = 512
NORM_EPS = 1e-6
S5_GROUP = 16
S5_GROUPS = 32
S5_STATE = 64
S5_WIDTH = S5_GROUPS * S5_STATE
GDN_HEAD_DIM = 128
GDN_HEADS = 4
GDN_CONV = 4
GDN_CHUNK = 64
SWA_HEAD_DIM = 64
SWA_HEADS = 8
SWA_KV_HEADS = 2
SWA_GROUP = SWA_HEADS // SWA_KV_HEADS
WINDOW = 128
SWA_BLOCK = 128
T5_BUCKETS = 32
T5_MAX_DISTANCE = 128
MLA_HEADS = 4
MLA_Q_RANK = 384
MLA_KV_RANK = 512
MLA_NOPE = 128
MLA_ROPE = 64
MLA_V = 128
ROPE_THETA = 10000.0

ADAM_LR = 0.001
ADAM_B1 = 0.9
ADAM_B2 = 0.999
ADAM_EPS = 1e-08
ADAM_WD = 0.01
ADAM_STEP = 10

NEG_BIG = -1e30


def _cparams(*sem):
    return pltpu.CompilerParams(dimension_semantics=sem if sem else None, vmem_limit_bytes=VMEM_LIMIT_BYTES)


def _pick(n, pref, unit):
    if n <= pref:
        return n
    t = (pref // unit) * unit
    while t >= unit:
        if n % t == 0:
            return t
        t -= unit
    return n


def _bdot(a, b, dims):
    return lax.dot_general(a.astype(BF16), b.astype(BF16), (dims, ((), ())), preferred_element_type=F32)


def _dot_nn(a, b):
    return _bdot(a, b, ((1,), (0,)))


def _dot_nt(a, b):
    return _bdot(a, b, ((1,), (1,)))


def _dot_tn(a, b):
    return _bdot(a, b, ((0,), (0,)))


def _mm_body(a_ref, b_ref, o_ref, *, ta, tb, nk):
    dims = ((0,) if ta else (1,), (1,) if tb else (0,))
    part = _bdot(a_ref[...], b_ref[...], dims)
    if nk == 1:
        o_ref[...] = part
    else:
        k = pl.program_id(2)

        @pl.when(k == 0)
        def _():
            o_ref[...] = part

        @pl.when(k > 0)
        def _():
            o_ref[...] += part


MM_BLOCK_BYTES = 40 * 1024 * 1024


def _mm_tiles(m, n, k, a_bytes, b_bytes):
    tm, tn = _pick(m, 1024, LANES), _pick(n, 1024, LANES)
    tk = LANES if k % LANES == 0 else k
    for cand in range(k, 0, -LANES) if k % LANES == 0 else (k,):
        if k % cand == 0 and 2 * cand * (tm * a_bytes + tn * b_bytes) + 2 * tm * tn * 4 <= MM_BLOCK_BYTES:
            tk = cand
            break
    return tm, tn, tk


def _mm_call(a, b, *, ta=False, tb=False, name):
    m, k = (a.shape[1], a.shape[0]) if ta else a.shape
    n = b.shape[0] if tb else b.shape[1]
    assert (b.shape[1] if tb else b.shape[0]) == k, (a.shape, b.shape, ta, tb)
    tm, tn, tk = _mm_tiles(m, n, k, a.dtype.itemsize, b.dtype.itemsize)
    nk = k // tk
    a_spec = pl.BlockSpec((tk, tm), lambda i, j, kk: (kk, i)) if ta else pl.BlockSpec((tm, tk), lambda i, j, kk: (i, kk))
    b_spec = pl.BlockSpec((tn, tk), lambda i, j, kk: (j, kk)) if tb else pl.BlockSpec((tk, tn), lambda i, j, kk: (kk, j))
    return pl.pallas_call(
        functools.partial(_mm_body, ta=ta, tb=tb, nk=nk),
        name=name,
        grid=(m // tm, n // tn, nk),
        in_specs=[a_spec, b_spec],
        out_specs=pl.BlockSpec((tm, tn), lambda i, j, kk: (i, j)),
        out_shape=jax.ShapeDtypeStruct((m, n), F32),
        compiler_params=_cparams("parallel", "parallel", "arbitrary"),
    )(a, b)


@jax.custom_vjp
def mm(a, b):
    return _mm_call(a, b, name="mm_fwd")


def _mm_fwd(a, b):
    return _mm_call(a, b, name="mm_fwd"), (a, b)


def _mm_bwd(res, g):
    a, b = res
    return _mm_call(g, b, tb=True, name="mm_da"), _mm_call(a, g, ta=True, name="mm_db")


mm.defvjp(_mm_fwd, _mm_bwd)


@jax.custom_vjp
def wmm(a, w, w_bf16):
    return _mm_call(a, w_bf16, name="wmm_fwd")


def _wmm_fwd(a, w, w_bf16):
    return _mm_call(a, w_bf16, name="wmm_fwd"), (a, w_bf16)


def _wmm_bwd(res, g):
    a, w_bf16 = res
    return _mm_call(g, w_bf16, tb=True, name="wmm_da"), _mm_call(a, g, ta=True, name="wmm_db"), jnp.zeros_like(w_bf16)


wmm.defvjp(_wmm_fwd, _wmm_bwd)


def _rms_fwd_body(x_ref, g_ref, y_ref):
    x = x_ref[...]
    r = lax.rsqrt(jnp.mean(x * x, axis=-1, keepdims=True) + NORM_EPS)
    y_ref[...] = x * r * g_ref[...]


def _rms_bwd_body(x_ref, g_ref, dy_ref, dx_ref, dg_ref, *, tr):
    i = pl.program_id(0)

    @pl.when(i == 0)
    def _():
        dg_ref[...] = jnp.zeros_like(dg_ref)

    x = x_ref[...]
    dy = dy_ref[...]
    r = lax.rsqrt(jnp.mean(x * x, axis=-1, keepdims=True) + NORM_EPS)
    xhat = x * r
    gy = dy * g_ref[...]
    dx_ref[...] = r * (gy - xhat * jnp.mean(gy * xhat, axis=-1, keepdims=True))
    dg_ref[...] += jnp.sum((dy * xhat).reshape(tr // SUBLANES, SUBLANES, x.shape[-1]), axis=0)


def _rms_rows(rows, cols):
    return _pick(rows, max(SUBLANES, (2 * 1024 * 1024) // (4 * cols)), SUBLANES)


def _rms_fwd_call(x, gain):
    rows, cols = x.shape
    tr = _rms_rows(rows, cols)
    return pl.pallas_call(
        functools.partial(_rms_fwd_body),
        name="rms_fwd",
        grid=(rows // tr,),
        in_specs=[pl.BlockSpec((tr, cols), lambda i: (i, 0)), pl.BlockSpec((1, cols), lambda i: (0, 0))],
        out_specs=pl.BlockSpec((tr, cols), lambda i: (i, 0)),
        out_shape=jax.ShapeDtypeStruct((rows, cols), F32),
        compiler_params=_cparams("parallel"),
    )(x, gain.reshape(1, cols))


def _rms_bwd_call(x, gain, dy):
    rows, cols = x.shape
    tr = _rms_rows(rows, cols)
    dx, dg = pl.pallas_call(
        functools.partial(_rms_bwd_body, tr=tr),
        name="rms_bwd",
        grid=(rows // tr,),
        in_specs=[
            pl.BlockSpec((tr, cols), lambda i: (i, 0)),
            pl.BlockSpec((1, cols), lambda i: (0, 0)),
            pl.BlockSpec((tr, cols), lambda i: (i, 0)),
        ],
        out_specs=[pl.BlockSpec((tr, cols), lambda i: (i, 0)), pl.BlockSpec((SUBLANES, cols), lambda i: (0, 0))],
        out_shape=[jax.ShapeDtypeStruct((rows, cols), F32), jax.ShapeDtypeStruct((SUBLANES, cols), F32)],
        compiler_params=_cparams("arbitrary"),
    )(x, gain.reshape(1, cols), dy)
    return dx, jnp.sum(dg, axis=0)


@jax.custom_vjp
def rmsnorm(x, gain):
    return _rms_fwd_call(x, gain)


def _rmsnorm_fwd(x, gain):
    return _rms_fwd_call(x, gain), (x, gain)


def _rmsnorm_bwd(res, dy):
    x, gain = res
    return _rms_bwd_call(x, gain, dy)


rmsnorm.defvjp(_rmsnorm_fwd, _rmsnorm_bwd)


def _loss_body(y_ref, t_ref, rows_ref, dy_ref):
    d = y_ref[...] - t_ref[...]
    rows_ref[...] = 0.5 * jnp.mean(d * d, axis=-1, keepdims=True)
    dy_ref[...] = d * (1.0 / d.shape[-1])


def loss_head(y, target):
    rows, cols = y.shape
    tr = _rms_rows(rows, cols)
    return pl.pallas_call(
        functools.partial(_loss_body),
        name="loss_head",
        grid=(rows // tr,),
        in_specs=[pl.BlockSpec((tr, cols), lambda i: (i, 0))] * 2,
        out_specs=[pl.BlockSpec((tr, 1), lambda i: (i, 0)), pl.BlockSpec((tr, cols), lambda i: (i, 0))],
        out_shape=[jax.ShapeDtypeStruct((rows, 1), F32), jax.ShapeDtypeStruct((rows, cols), F32)],
        compiler_params=_cparams("parallel"),
    )(y, target)


def _adamw_body(w_ref, g_ref, m_ref, v_ref, d_ref, nm_ref, nv_ref):
    g = g_ref[...]
    m = ADAM_B1 * m_ref[...] + (1.0 - ADAM_B1) * g
    v = ADAM_B2 * v_ref[...] + (1.0 - ADAM_B2) * (g * g)
    m_hat = m / (1.0 - ADAM_B1**ADAM_STEP)
    v_hat = v / (1.0 - ADAM_B2**ADAM_STEP)
    d_ref[...] = -ADAM_LR * (m_hat / (jnp.sqrt(v_hat) + ADAM_EPS) + ADAM_WD * w_ref[...])
    nm_ref[...] = m
    nv_ref[...] = v


def adamw_flat(w, g, m, v):
    rows, cols = w.shape
    tr = _rms_rows(rows, cols)
    spec = pl.BlockSpec((tr, cols), lambda i: (i, 0))
    return pl.pallas_call(
        functools.partial(_adamw_body),
        name="adamw",
        grid=(rows // tr,),
        in_specs=[spec] * 4,
        out_specs=[spec] * 3,
        out_shape=[jax.ShapeDtypeStruct((rows, cols), F32)] * 3,
        compiler_params=_cparams("parallel"),
    )(w, g, m, v)


def _sum_body(x_ref, o_ref, *, n):
    acc = x_ref[0].astype(F32)
    for k in range(1, n):
        acc = acc + x_ref[k].astype(F32)
    o_ref[...] = acc


def sum_leading(x):
    n, rows, cols = x.shape
    tr = _pick(rows, 256, 2 * SUBLANES)
    return pl.pallas_call(
        functools.partial(_sum_body, n=n),
        name="sum_leading",
        grid=(rows // tr,),
        in_specs=[pl.BlockSpec((n, tr, cols), lambda i: (0, i, 0))],
        out_specs=pl.BlockSpec((tr, cols), lambda i: (i, 0)),
        out_shape=jax.ShapeDtypeStruct((rows, cols), F32),
        compiler_params=_cparams("parallel"),
    )(x)


def _mla_fwd_body(q_ref, k_ref, v_ref, o_ref, lse_ref, m_sc, l_sc, acc_sc, *, scale, nk):
    ki = pl.program_id(2)

    @pl.when(ki == 0)
    def _():
        m_sc[...] = jnp.full_like(m_sc, NEG_BIG)
        l_sc[...] = jnp.zeros_like(l_sc)
        acc_sc[...] = jnp.zeros_like(acc_sc)

    s = _dot_nt(q_ref[...], k_ref[...]) * scale
    m_prev = m_sc[...]
    m_new = jnp.maximum(m_prev, jnp.max(s, axis=1, keepdims=True))
    alpha = jnp.exp(m_prev - m_new)
    p = jnp.exp(s - m_new)
    l_sc[...] = alpha * l_sc[...] + jnp.sum(p, axis=1, keepdims=True)
    acc_sc[...] = alpha * acc_sc[...] + _dot_nn(p, v_ref[...])
    m_sc[...] = m_new

    @pl.when(ki == nk - 1)
    def _():
        o_ref[...] = acc_sc[...] / l_sc[...]
        lse_ref[...] = m_sc[...] + jnp.log(l_sc[...])


def _mla_fwd_call(q, k, v):
    h, seq, dq = q.shape
    dv = v.shape[-1]
    tq, tk = _pick(seq, 1024, LANES), _pick(seq, 1024, LANES)
    nk = seq // tk
    scale = dq**-0.5
    return pl.pallas_call(
        functools.partial(_mla_fwd_body, scale=scale, nk=nk),
        name="mla_fwd",
        grid=(h, seq // tq, nk),
        in_specs=[
            pl.BlockSpec((None, tq, dq), lambda hh, i, j: (hh, i, 0)),
            pl.BlockSpec((None, tk, dq), lambda hh, i, j: (hh, j, 0)),
            pl.BlockSpec((None, tk, dv), lambda hh, i, j: (hh, j, 0)),
        ],
        out_specs=[
            pl.BlockSpec((None, tq, dv), lambda hh, i, j: (hh, i, 0)),
            pl.BlockSpec((None, tq, 1), lambda hh, i, j: (hh, i, 0)),
        ],
        out_shape=[jax.ShapeDtypeStruct((h, seq, dv), F32), jax.ShapeDtypeStruct((h, seq, 1), F32)],
        scratch_shapes=[pltpu.VMEM((tq, 1), F32), pltpu.VMEM((tq, 1), F32), pltpu.VMEM((tq, dv), F32)],
        compiler_params=_cparams("parallel", "parallel", "arbitrary"),
    )(q, k, v)


def _mla_bwd_body(q_ref, k_ref, v_ref, do_ref, lse_ref, dl_ref, dq_ref, dk_ref, dv_ref, dk_sc, dv_sc, *, scale, nq, tq):
    ki = pl.program_id(1)
    qi = pl.program_id(2)

    @pl.when(jnp.logical_and(ki == 0, qi == 0))
    def _():
        dq_ref[...] = jnp.zeros_like(dq_ref)

    @pl.when(qi == 0)
    def _():
        dk_sc[...] = jnp.zeros_like(dk_sc)
        dv_sc[...] = jnp.zeros_like(dv_sc)

    q = q_ref[...]
    k = k_ref[...]
    do = do_ref[...]
    p = jnp.exp(_dot_nt(q, k) * scale - lse_ref[...])
    dv_sc[...] += _dot_tn(p, do)
    ds = p * (_dot_nt(do, v_ref[...]) - dl_ref[...]) * scale
    dk_sc[...] += _dot_tn(ds, q)
    rows = pl.ds(pl.multiple_of(qi * tq, tq), tq)
    dq_ref[rows, :] += _dot_nn(ds, k)

    @pl.when(qi == nq - 1)
    def _():
        dk_ref[...] = dk_sc[...]
        dv_ref[...] = dv_sc[...]


def _mla_bwd_call(q, k, v, do, lse, delta):
    h, seq, dq = q.shape
    dv = v.shape[-1]
    tq, tk = _pick(seq, 1024, LANES), _pick(seq, 1024, LANES)
    nq = seq // tq
    scale = dq**-0.5
    return pl.pallas_call(
        functools.partial(_mla_bwd_body, scale=scale, nq=nq, tq=tq),
        name="mla_bwd",
        grid=(h, seq // tk, nq),
        in_specs=[
            pl.BlockSpec((None, tq, dq), lambda hh, j, i: (hh, i, 0)),
            pl.BlockSpec((None, tk, dq), lambda hh, j, i: (hh, j, 0)),
            pl.BlockSpec((None, tk, dv), lambda hh, j, i: (hh, j, 0)),
            pl.BlockSpec((None, tq, dv), lambda hh, j, i: (hh, i, 0)),
            pl.BlockSpec((None, tq, 1), lambda hh, j, i: (hh, i, 0)),
            pl.BlockSpec((None, tq, 1), lambda hh, j, i: (hh, i, 0)),
        ],
        out_specs=[
            pl.BlockSpec((None, seq, dq), lambda hh, j, i: (hh, 0, 0)),
            pl.BlockSpec((None, tk, dq), lambda hh, j, i: (hh, j, 0)),
            pl.BlockSpec((None, tk, dv), lambda hh, j, i: (hh, j, 0)),
        ],
        out_shape=[
            jax.ShapeDtypeStruct((h, seq, dq), F32),
            jax.ShapeDtypeStruct((h, seq, dq), F32),
            jax.ShapeDtypeStruct((h, seq, dv), F32),
        ],
        scratch_shapes=[pltpu.VMEM((tk, dq), F32), pltpu.VMEM((tk, dv), F32)],
        compiler_params=_cparams("parallel", "arbitrary", "arbitrary"),
    )(q, k, v, do, lse, delta)


@jax.custom_vjp
def mla_attention(q, k, v):
    return _mla_fwd_call(q, k, v)[0]


def _mla_attention_fwd(q, k, v):
    o, lse = _mla_fwd_call(q, k, v)
    return o, (q, k, v, o, lse)


def _mla_attention_bwd(res, do):
    q, k, v, o, lse = res
    delta = jnp.sum(do * o, axis=-1, keepdims=True)
    return tuple(_mla_bwd_call(q, k, v, do, lse, delta))


mla_attention.defvjp(_mla_attention_fwd, _mla_attention_bwd)


def _swa_block(q4, kp, kc, kn, vp, vc, vn, bias, sink, *, valid):
    kb = jnp.concatenate([kp, kc, kn], axis=0)
    vb = jnp.concatenate([vp, vc, vn], axis=0)
    s = _dot_nt(q4, kb) * (SWA_HEAD_DIM**-0.5) + bias
    s = jnp.where(valid, s, NEG_BIG)
    m = lax.stop_gradient(jnp.maximum(jnp.max(s, axis=1, keepdims=True), sink))
    p = jnp.exp(s - m)
    denom = jnp.sum(p, axis=1, keepdims=True) + jnp.exp(sink - m)
    return _dot_nn(p / denom, vb)


def _swa_valid(n, seq):
    rows = SWA_GROUP * SWA_BLOCK
    qi = lax.broadcasted_iota(jnp.int32, (rows, 3 * SWA_BLOCK), 0) % SWA_BLOCK
    sj = lax.broadcasted_iota(jnp.int32, (rows, 3 * SWA_BLOCK), 1)
    rel = sj - SWA_BLOCK - qi
    kpos = n * SWA_BLOCK + sj - SWA_BLOCK
    return (jnp.abs(rel) <= WINDOW) & (kpos >= 0) & (kpos < seq)


def _swa_operands(q_ref, kp_ref, kc_ref, kn_ref, vp_ref, vc_ref, vn_ref, b_ref, s_ref):
    q4 = q_ref[...].reshape(SWA_GROUP * SWA_BLOCK, SWA_HEAD_DIM)
    return (q4, kp_ref[...], kc_ref[...], kn_ref[...], vp_ref[...], vc_ref[...], vn_ref[...], b_ref[...], s_ref[...])


def _swa_fwd_body(q_ref, kp_ref, kc_ref, kn_ref, vp_ref, vc_ref, vn_ref, b_ref, s_ref, o_ref, *, seq):
    valid = _swa_valid(pl.program_id(1), seq)
    out = _swa_block(*_swa_operands(q_ref, kp_ref, kc_ref, kn_ref, vp_ref, vc_ref, vn_ref, b_ref, s_ref), valid=valid)
    o_ref[...] = out.reshape(SWA_GROUP, SWA_BLOCK, SWA_HEAD_DIM)


def _swa_bwd_body(q_ref, kp_ref, kc_ref, kn_ref, vp_ref, vc_ref, vn_ref, b_ref, s_ref, do_ref,
                  dq_ref, dk_ref, dv_ref, db_ref, ds_ref, *, seq):
    n = pl.program_id(1)

    @pl.when(n == 0)
    def _():
        db_ref[...] = jnp.zeros_like(db_ref)
        ds_ref[...] = jnp.zeros_like(ds_ref)

    valid = _swa_valid(n, seq)
    ops = _swa_operands(q_ref, kp_ref, kc_ref, kn_ref, vp_ref, vc_ref, vn_ref, b_ref, s_ref)
    _, vjp = jax.vjp(functools.partial(_swa_block, valid=valid), *ops)
    do = do_ref[...].reshape(SWA_GROUP * SWA_BLOCK, SWA_HEAD_DIM)
    dq4, dkp, dkc, dkn, dvp, dvc, dvn, dbias, dsink = vjp(do)
    dq_ref[...] = dq4.reshape(SWA_GROUP, SWA_BLOCK, SWA_HEAD_DIM)
    dk_ref[0] = dkp
    dk_ref[1] = dkc
    dk_ref[2] = dkn
    dv_ref[0] = dvp
    dv_ref[1] = dvc
    dv_ref[2] = dvn
    db_ref[...] += dbias
    ds_ref[...] += dsink


def _swa_in_specs(nb):
    blk = (None, SWA_BLOCK, SWA_HEAD_DIM)
    prev = lambda h, n: (h, jnp.maximum(n - 1, 0), 0)
    own = lambda h, n: (h, n, 0)
    nxt = lambda h, n: (h, jnp.minimum(n + 1, nb - 1), 0)
    rows = SWA_GROUP * SWA_BLOCK
    return [
        pl.BlockSpec((None, SWA_GROUP, SWA_BLOCK, SWA_HEAD_DIM), lambda h, n: (h, 0, n, 0)),
        pl.BlockSpec(blk, prev), pl.BlockSpec(blk, own), pl.BlockSpec(blk, nxt),
        pl.BlockSpec(blk, prev), pl.BlockSpec(blk, own), pl.BlockSpec(blk, nxt),
        pl.BlockSpec((None, rows, 3 * SWA_BLOCK), lambda h, n: (h, 0, 0)),
        pl.BlockSpec((None, rows, 1), lambda h, n: (h, 0, 0)),
    ]


def _swa_fwd_call(q, k, v, bias, sink):
    kv, g, seq, d = q.shape
    nb = seq // SWA_BLOCK
    return pl.pallas_call(
        functools.partial(_swa_fwd_body, seq=seq),
        name="swa_fwd",
        grid=(kv, nb),
        in_specs=_swa_in_specs(nb),
        out_specs=pl.BlockSpec((None, g, SWA_BLOCK, d), lambda h, n: (h, 0, n, 0)),
        out_shape=jax.ShapeDtypeStruct(q.shape, F32),
        compiler_params=_cparams("parallel", "parallel"),
    )(q, k, k, k, v, v, v, bias, sink)


def _swa_bwd_call(q, k, v, bias, sink, do):
    kv, g, seq, d = q.shape
    nb = seq // SWA_BLOCK
    rows = g * SWA_BLOCK
    part = jax.ShapeDtypeStruct((kv, nb, 3, SWA_BLOCK, d), F32)
    part_spec = pl.BlockSpec((None, None, 3, SWA_BLOCK, d), lambda h, n: (h, n, 0, 0, 0))
    dq, dkp, dvp, dbias, dsink = pl.pallas_call(
        functools.partial(_swa_bwd_body, seq=seq),
        name="swa_bwd",
        grid=(kv, nb),
        in_specs=_swa_in_specs(nb) + [pl.BlockSpec((None, g, SWA_BLOCK, d), lambda h, n: (h, 0, n, 0))],
        out_specs=[
            pl.BlockSpec((None, g, SWA_BLOCK, d), lambda h, n: (h, 0, n, 0)),
            part_spec, part_spec,
            pl.BlockSpec((None, rows, 3 * SWA_BLOCK), lambda h, n: (h, 0, 0)),
            pl.BlockSpec((None, rows, 1), lambda h, n: (h, 0, 0)),
        ],
        out_shape=[jax.ShapeDtypeStruct(q.shape, F32), part, part,
                   jax.ShapeDtypeStruct(bias.shape, F32), jax.ShapeDtypeStruct(sink.shape, F32)],
        compiler_params=_cparams("parallel", "arbitrary"),
    )(q, k, k, k, v, v, v, bias, sink, do)

    def fold(p):
        zero = jnp.zeros_like(p[:, :1, 0])
        total = p[:, :, 1] + jnp.concatenate([p[:, 1:, 0], zero], axis=1) + jnp.concatenate([zero, p[:, :-1, 2]], axis=1)
        return total.reshape(kv, seq, d)

    return dq, fold(dkp), fold(dvp), dbias, dsink


@jax.custom_vjp
def swa_attention(q, k, v, bias, sink):
    return _swa_fwd_call(q, k, v, bias, sink)


def _swa_attention_fwd(q, k, v, bias, sink):
    return _swa_fwd_call(q, k, v, bias, sink), (q, k, v, bias, sink)


def _swa_attention_bwd(res, do):
    return _swa_bwd_call(*res, do)


swa_attention.defvjp(_swa_attention_fwd, _swa_attention_bwd)


def _scan_tiles(n_tiles, reverse, tile_fn, init):
    def step(i, carry):
        ti = (n_tiles - 1 - i) if reverse else i
        return tile_fn(pl.multiple_of(ti * SUBLANES, SUBLANES), carry)

    return lax.fori_loop(0, n_tiles, step, init)


def _row_order(reverse):
    return tuple(reversed(range(SUBLANES))) if reverse else tuple(range(SUBLANES))


def _s5_fwd_dir(a_ref, bu_ref, s_ref, carry, *, reverse, tt, w):
    ar, ai = a_ref[0:1, :], a_ref[1:2, :]
    rowid = lax.broadcasted_iota(jnp.int32, (SUBLANES, w), 0)

    def tile(base, c):
        sr, si = c
        x = bu_ref[pl.ds(base, SUBLANES), :]
        xr, xi = x[:, :w], x[:, w:]
        out_r = jnp.zeros((SUBLANES, w), F32)
        out_i = jnp.zeros((SUBLANES, w), F32)
        for j in _row_order(reverse):
            nr = ar * sr - ai * si + xr[j:j + 1, :]
            ni = ar * si + ai * sr + xi[j:j + 1, :]
            out_r = jnp.where(rowid == j, nr, out_r)
            out_i = jnp.where(rowid == j, ni, out_i)
            sr, si = nr, ni
        s_ref[pl.ds(base, SUBLANES), :] = jnp.concatenate([out_r, out_i], axis=1)
        return sr, si

    sr, si = _scan_tiles(tt // SUBLANES, reverse, tile, (carry[0:1, :], carry[1:2, :]))
    carry[0:1, :] = sr
    carry[1:2, :] = si


def _s5_fwd_body(a_ref, bu_ref, s_ref, carry, *, tt, w):
    d = pl.program_id(0)

    @pl.when(pl.program_id(1) == 0)
    def _():
        carry[...] = jnp.zeros_like(carry)

    @pl.when(d == 0)
    def _():
        _s5_fwd_dir(a_ref, bu_ref, s_ref, carry, reverse=False, tt=tt, w=w)

    @pl.when(d == 1)
    def _():
        _s5_fwd_dir(a_ref, bu_ref, s_ref, carry, reverse=True, tt=tt, w=w)


def _s5_bwd_dir(a_ref, s_ref, ds_ref, dbu_ref, da_ref, carry, acc, *, reverse, tt, w):
    ar, ai = a_ref[0:1, :], a_ref[1:2, :]
    rowid = lax.broadcasted_iota(jnp.int32, (SUBLANES, w), 0)

    first = _row_order(reverse)[0]
    acc[...] = jnp.zeros_like(acc)

    def tile(base, c):
        lr, li = lam_in_r, lam_in_i = c
        s = s_ref[pl.ds(base, SUBLANES), :]
        g = ds_ref[pl.ds(base, SUBLANES), :]
        out_r = jnp.zeros((SUBLANES, w), F32)
        out_i = jnp.zeros((SUBLANES, w), F32)
        for j in _row_order(reverse):
            nr = g[j:j + 1, :w] + ar * lr + ai * li
            ni = g[j:j + 1, w:] + ar * li - ai * lr
            out_r = jnp.where(rowid == j, nr, out_r)
            out_i = jnp.where(rowid == j, ni, out_i)
            lr, li = nr, ni
        dbu_ref[pl.ds(base, SUBLANES), :] = jnp.concatenate([out_r, out_i], axis=1)
        shift = SUBLANES - 1 if reverse else 1
        next_r = jnp.where(rowid == first, lam_in_r, pltpu.roll(out_r, shift, axis=0))
        next_i = jnp.where(rowid == first, lam_in_i, pltpu.roll(out_i, shift, axis=0))
        sr, si = s[:, :w], s[:, w:]
        acc[0] += sr * next_r + si * next_i
        acc[1] += sr * next_i - si * next_r
        return lr, li

    lr, li = _scan_tiles(tt // SUBLANES, reverse, tile, (carry[0:1, :], carry[1:2, :]))
    carry[0:1, :] = lr
    carry[1:2, :] = li
    da_ref[0:1, :] += jnp.sum(acc[0], axis=0, keepdims=True)
    da_ref[1:2, :] += jnp.sum(acc[1], axis=0, keepdims=True)


def _s5_bwd_body(a_ref, s_ref, ds_ref, dbu_ref, da_ref, carry, acc, *, tt, w):
    d = pl.program_id(0)

    @pl.when(pl.program_id(1) == 0)
    def _():
        carry[...] = jnp.zeros_like(carry)
        da_ref[...] = jnp.zeros_like(da_ref)

    @pl.when(d == 0)
    def _():
        _s5_bwd_dir(a_ref, s_ref, ds_ref, dbu_ref, da_ref, carry, acc, reverse=True, tt=tt, w=w)

    @pl.when(d == 1)
    def _():
        _s5_bwd_dir(a_ref, s_ref, ds_ref, dbu_ref, da_ref, carry, acc, reverse=False, tt=tt, w=w)


def _s5_time_map(nt, flip_dir):
    def time_block(d, t):
        back = nt - 1 - t
        return jnp.where(d == flip_dir, back, t)

    return time_block


def _s5_fwd_call(a, bu):
    seq, w4 = bu.shape
    w = w4 // 4
    tt = _pick(seq, 256, SUBLANES)
    nt = seq // tt
    tb = _s5_time_map(nt, 1)
    return pl.pallas_call(
        functools.partial(_s5_fwd_body, tt=tt, w=w),
        name="s5_scan_fwd",
        grid=(2, nt),
        in_specs=[
            pl.BlockSpec((None, 2, w), lambda d, t: (d, 0, 0)),
            pl.BlockSpec((tt, 2 * w), lambda d, t: (tb(d, t), d)),
        ],
        out_specs=pl.BlockSpec((tt, 2 * w), lambda d, t: (tb(d, t), d)),
        out_shape=jax.ShapeDtypeStruct(bu.shape, F32),
        scratch_shapes=[pltpu.VMEM((2, w), F32)],
        compiler_params=_cparams("parallel", "arbitrary"),
    )(a, bu)


def _s5_bwd_call(a, s_prev, ds):
    seq, w4 = ds.shape
    w = w4 // 4
    tt = _pick(seq, 256, SUBLANES)
    nt = seq // tt
    tb = _s5_time_map(nt, 0)
    blk = pl.BlockSpec((tt, 2 * w), lambda d, t: (tb(d, t), d))
    return pl.pallas_call(
        functools.partial(_s5_bwd_body, tt=tt, w=w),
        name="s5_scan_bwd",
        grid=(2, nt),
        in_specs=[pl.BlockSpec((None, 2, w), lambda d, t: (d, 0, 0)), blk, blk],
        out_specs=[blk, pl.BlockSpec((None, 2, w), lambda d, t: (d, 0, 0))],
        out_shape=[jax.ShapeDtypeStruct(ds.shape, F32), jax.ShapeDtypeStruct(a.shape, F32)],
        scratch_shapes=[pltpu.VMEM((2, w), F32), pltpu.VMEM((2, SUBLANES, w), F32)],
        compiler_params=_cparams("parallel", "arbitrary"),
    )(a, s_prev, ds)


@jax.custom_vjp
def s5_scan(a, bu):
    return _s5_fwd_call(a, bu)


def _s5_scan_fwd(a, bu):
    s = _s5_fwd_call(a, bu)
    return s, (a, s)


def _s5_scan_bwd(res, ds):
    a, s = res
    dbu, da = _s5_bwd_call(a, s, ds)
    return da, dbu


s5_scan.defvjp(_s5_scan_fwd, _s5_scan_bwd)


NN, NT, TN = ((1,), (0,)), ((1,), (1,)), ((0,), (0,))


def _bmm(a, b, dims):
    return jnp.stack([_bdot(a[i], b[i], dims) for i in range(a.shape[0])])


def _bmm3(a, b):
    ah = a.astype(BF16)
    bh = b.astype(BF16)
    al = a - ah.astype(F32)
    bl = b - bh.astype(F32)
    return _bmm(ah, bh, NN) + _bmm(ah, bl, NN) + _bmm(al, bh, NN)


GDN_INV_BASE = 8


def _unit_triangular_inverse(a, ri, ci):
    c = a.shape[-1]

    def same_block(size):
        shift = int(math.log2(size))
        return lax.shift_right_logical(ri, shift) == lax.shift_right_logical(ci, shift)

    diag = jnp.where(same_block(GDN_INV_BASE), a, 0.0)
    inv = jnp.where(ri == ci, 1.0, 0.0) - diag
    power = diag
    for _ in range(int(math.log2(GDN_INV_BASE)) - 1):
        power = _bmm3(power, power)
        inv = inv + _bmm3(inv, power)
    size = GDN_INV_BASE
    while size < c:
        off = jnp.where(jnp.logical_and(same_block(2 * size), jnp.logical_not(same_block(size))), a, 0.0)
        inv = inv - _bmm3(_bmm3(inv, off), inv)
        size *= 2
    return inv


def _gdn_chunks(q, k, v, gr, br, state, *, n_fwd):
    b, c, _ = q.shape
    dv = v.shape[2]
    ri = lax.broadcasted_iota(jnp.int32, (b, c, c), 1)
    ci = lax.broadcasted_iota(jnp.int32, (b, c, c), 2)
    fwd = jnp.stack([jnp.full((c, c), 1 if i < n_fwd else 0, jnp.int32) for i in range(b)]) == 1
    lower = jnp.logical_or(jnp.logical_and(fwd, ri >= ci), jnp.logical_and(jnp.logical_not(fwd), ri <= ci))
    strict = jnp.logical_and(lower, ri != ci)
    as_column = lambda row: jnp.sum(jnp.where(ri == ci, row, 0.0), axis=2, keepdims=True)
    gc, bc = as_column(gr), as_column(br)
    kb = k * bc
    decay = jnp.where(lower, jnp.exp(jnp.where(lower, gc - gr, 0.0)), 0.0)
    a = jnp.where(strict, _bmm(kb, k, NT) * decay, 0.0)
    inv = _unit_triangular_inverse(a, ri, ci)
    eg = jnp.exp(gc)
    sol = _bmm3(inv, jnp.concatenate([v * bc, kb * eg], axis=2))
    u, w = sol[:, :, :dv], sol[:, :, dv:]
    attn = _bmm(q, k, NT) * decay
    v_new = u - _bmm(w, state, NN)
    o = _bmm(q * eg, state, NN) + _bmm(attn, v_new, NN)
    row = lax.broadcasted_iota(jnp.int32, (b, c, 1), 1)
    fwd_col = jnp.stack([jnp.full((c, 1), 1 if i < n_fwd else 0, jnp.int32) for i in range(b)]) == 1
    last = jnp.logical_or(jnp.logical_and(fwd_col, row == c - 1), jnp.logical_and(jnp.logical_not(fwd_col), row == 0))
    g_last = jnp.sum(jnp.where(last, gc, 0.0), axis=1, keepdims=True)
    new_state = state * jnp.exp(g_last) + _bmm(k * jnp.exp(g_last - gc), v_new, TN)
    return o, new_state


def _gdn_heads(refs, heads, dh):
    return jnp.stack([r[:, h * dh:(h + 1) * dh] for r in refs for h in range(heads)])


def _gdn_operands(q_refs, k_refs, v_refs, gr_refs, br_refs, heads, dh):
    rows = lambda refs: jnp.concatenate([r[...] for r in refs], axis=0)
    return (_gdn_heads(q_refs, heads, dh), _gdn_heads(k_refs, heads, dh), _gdn_heads(v_refs, heads, dh),
            rows(gr_refs), rows(br_refs))


def _gdn_fwd_body(qf, kf, vf, qb, kb, vb, grf, brf, grb, brb, of, ob, s0f, s0b, state, *, heads, dh):
    @pl.when(pl.program_id(0) == 0)
    def _():
        state[...] = jnp.zeros_like(state)

    s0 = state[...]
    s0f[...] = s0[:heads]
    s0b[...] = s0[heads:]
    o, new_state = _gdn_chunks(*_gdn_operands((qf, qb), (kf, kb), (vf, vb), (grf, grb), (brf, brb), heads, dh), s0,
                               n_fwd=heads)
    state[...] = new_state
    for i in range(2 * heads):
        (of, ob)[i // heads][:, (i % heads) * dh:(i % heads + 1) * dh] = o[i]


def _gdn_bwd_body(qf, kf, vf, qb, kb, vb, grf, brf, grb, brb, s0f, s0b, dof, dob,
                  dqf, dkf, dvf, dqb, dkb, dvb, dgrf, dbrf, dgrb, dbrb, dstate, *, heads, dh):
    @pl.when(pl.program_id(0) == 0)
    def _():
        dstate[...] = jnp.zeros_like(dstate)

    ops = _gdn_operands((qf, qb), (kf, kb), (vf, vb), (grf, grb), (brf, brb), heads, dh)
    s0 = jnp.concatenate([s0f[...], s0b[...]], axis=0)
    do = _gdn_heads((dof, dob), heads, dh)
    _, vjp = jax.vjp(functools.partial(_gdn_chunks, n_fwd=heads), *ops, s0)
    dq, dk, dv, dgr, dbr, ds0 = vjp((do, dstate[...]))
    dstate[...] = ds0
    for i in range(2 * heads):
        d, cols = i // heads, slice((i % heads) * dh, (i % heads + 1) * dh)
        (dqf, dqb)[d][:, cols] = dq[i]
        (dkf, dkb)[d][:, cols] = dk[i]
        (dvf, dvb)[d][:, cols] = dv[i]
    dgrf[...] = dgr[:heads]
    dgrb[...] = dgr[heads:]
    dbrf[...] = dbr[:heads]
    dbrb[...] = dbr[heads:]


def _gdn_specs(nc, heads, dh, backward):
    c = GDN_CHUNK
    up = lambda n: n
    down = lambda n: nc - 1 - n
    out = []
    for chunk in ((down, up) if backward else (up, down)):
        out.append((
            pl.BlockSpec((c, heads * dh), lambda n, chunk=chunk: (chunk(n), 0)),
            pl.BlockSpec((heads, None, 1, c), lambda n, chunk=chunk: (0, chunk(n), 0, 0)),
            pl.BlockSpec((heads, None, dh, dh), lambda n, chunk=chunk: (0, chunk(n), 0, 0)),
        ))
    return out


def _gdn_fwd_call(q, k, v, grf, brf, grb, brb):
    seq, width = q.shape
    heads = grf.shape[0]
    dh = width // heads
    nc = seq // GDN_CHUNK
    (seq_f, row_f, st_f), (seq_b, row_b, st_b) = _gdn_specs(nc, heads, dh, False)
    states = jax.ShapeDtypeStruct((heads, nc, dh, dh), F32)
    return pl.pallas_call(
        functools.partial(_gdn_fwd_body, heads=heads, dh=dh),
        name="gdn_fwd",
        grid=(nc,),
        in_specs=[seq_f] * 3 + [seq_b] * 3 + [row_f, row_f, row_b, row_b],
        out_specs=[seq_f, seq_b, st_f, st_b],
        out_shape=[jax.ShapeDtypeStruct(q.shape, F32)] * 2 + [states] * 2,
        scratch_shapes=[pltpu.VMEM((2 * heads, dh, dh), F32)],
        compiler_params=_cparams("arbitrary"),
    )(q, k, v, q, k, v, grf, brf, grb, brb)


def _gdn_bwd_call(q, k, v, grf, brf, grb, brb, s0f, s0b, dof, dob):
    seq, width = q.shape
    heads = grf.shape[0]
    dh = width // heads
    nc = seq // GDN_CHUNK
    (seq_f, row_f, st_f), (seq_b, row_b, st_b) = _gdn_specs(nc, heads, dh, True)
    like = lambda t: jax.ShapeDtypeStruct(t.shape, F32)
    return pl.pallas_call(
        functools.partial(_gdn_bwd_body, heads=heads, dh=dh),
        name="gdn_bwd",
        grid=(nc,),
        in_specs=[seq_f] * 3 + [seq_b] * 3 + [row_f, row_f, row_b, row_b, st_f, st_b, seq_f, seq_b],
        out_specs=[seq_f] * 3 + [seq_b] * 3 + [row_f, row_f, row_b, row_b],
        out_shape=[like(q)] * 6 + [like(grf), like(brf), like(grb), like(brb)],
        scratch_shapes=[pltpu.VMEM((2 * heads, dh, dh), F32)],
        compiler_params=_cparams("arbitrary"),
    )(q, k, v, q, k, v, grf, brf, grb, brb, s0f, s0b, dof, dob)


@jax.custom_vjp
def gdn_delta_rule(q, k, v, grf, brf, grb, brb):
    return tuple(_gdn_fwd_call(q, k, v, grf, brf, grb, brb)[:2])


def _gdn_delta_rule_fwd(*ops):
    of, ob, s0f, s0b = _gdn_fwd_call(*ops)
    return (of, ob), (*ops, s0f, s0b)


def _gdn_delta_rule_bwd(res, do):
    dqf, dkf, dvf, dqb, dkb, dvb, *small = _gdn_bwd_call(*res, *do)
    return (dqf + dqb, dkf + dkb, dvf + dvb, *small)


gdn_delta_rule.defvjp(_gdn_delta_rule_fwd, _gdn_delta_rule_bwd)


def _mesh_position():
    return lax.axis_index("x"), lax.axis_index("y"), lax.axis_index("c")


def _all_gather_body(x_ref, out_ref, send_sems, recv_sems, local_sem):
    x, y, c = _mesh_position()
    me, sibling = (x, y, c), (x, y, 1 - c)
    chips = [(1 - x, y), (x, 1 - y), (1 - x, 1 - y)]

    def slot(px, py, pc):
        return out_ref.at[4 * px + 2 * py + pc]

    def copy(k, block, to, src=None):
        return pltpu.make_async_remote_copy(
            src_ref=slot(*block) if src is None else src, dst_ref=slot(*block),
            send_sem=send_sems.at[k], recv_sem=recv_sems.at[k], device_id=to, device_id_type=pl.DeviceIdType.MESH)

    mine = pltpu.make_async_copy(x_ref, slot(*me), local_sem)
    mine.start()
    first = [copy(0, me, sibling, src=x_ref)]
    first += [copy(1 + j, me, (*chip, c), src=x_ref) for j, chip in enumerate(chips)]
    for cp in first:
        cp.start()
    passed = [copy(4 + j, (*chip, c), sibling) for j, chip in enumerate(chips)]
    for j, chip in enumerate(chips):
        copy(1 + j, (*chip, c), me).wait_recv()
        passed[j].start()
    copy(0, sibling, me).wait_recv()
    for j, chip in enumerate(chips):
        copy(4 + j, (*chip, 1 - c), me).wait_recv()
    for cp in first + passed:
        cp.wait_send()
    mine.wait()


def all_gather(x, name):
    return pl.pallas_call(
        functools.partial(_all_gather_body),
        name=name,
        in_specs=[pl.BlockSpec(memory_space=pl.ANY)],
        out_specs=pl.BlockSpec(memory_space=pl.ANY),
        out_shape=jax.ShapeDtypeStruct((N_DEV,) + x.shape, x.dtype),
        scratch_shapes=[pltpu.SemaphoreType.DMA((N_DEV - 1,)), pltpu.SemaphoreType.DMA((N_DEV - 1,)), pltpu.SemaphoreType.DMA],
    )(x)


N_CHIPS = 4


def _swap_cores_body(x_ref, got_ref, send_sem, recv_sem):
    x, y, c = _mesh_position()
    swap = pltpu.make_async_remote_copy(
        src_ref=x_ref.at[1 - c], dst_ref=got_ref, send_sem=send_sem, recv_sem=recv_sem,
        device_id=(x, y, 1 - c), device_id_type=pl.DeviceIdType.MESH)
    swap.start()
    swap.wait()


def swap_cores(x, name):
    return pl.pallas_call(
        functools.partial(_swap_cores_body),
        name=name,
        in_specs=[pl.BlockSpec(memory_space=pl.ANY)],
        out_specs=pl.BlockSpec(memory_space=pl.ANY),
        out_shape=jax.ShapeDtypeStruct(x.shape[1:], x.dtype),
        scratch_shapes=[pltpu.SemaphoreType.DMA, pltpu.SemaphoreType.DMA],
    )(x)


def _exchange_chips_body(x_ref, out_ref, send_sems, recv_sems, local_sem):
    x, y, c = _mesh_position()
    my_chip = 2 * x + y
    copies = []
    for m in range(1, N_CHIPS):
        px = 1 - x if m & 2 else x
        py = 1 - y if m & 1 else y
        copies.append(pltpu.make_async_remote_copy(
            src_ref=x_ref.at[2 * px + py], dst_ref=out_ref.at[my_chip],
            send_sem=send_sems.at[m - 1], recv_sem=recv_sems.at[m - 1],
            device_id=(px, py, c), device_id_type=pl.DeviceIdType.MESH))
    for cp in copies:
        cp.start()
    mine = pltpu.make_async_copy(x_ref.at[my_chip], out_ref.at[my_chip], local_sem)
    mine.start()
    for cp in copies:
        cp.wait()
    mine.wait()


def exchange_chips(x, name):
    return pl.pallas_call(
        functools.partial(_exchange_chips_body),
        name=name,
        in_specs=[pl.BlockSpec(memory_space=pl.ANY)],
        out_specs=pl.BlockSpec(memory_space=pl.ANY),
        out_shape=jax.ShapeDtypeStruct(x.shape, x.dtype),
        scratch_shapes=[pltpu.SemaphoreType.DMA((N_CHIPS - 1,)), pltpu.SemaphoreType.DMA((N_CHIPS - 1,)),
                        pltpu.SemaphoreType.DMA],
    )(x)


def _add_pairs_body(a_ref, b_ref, o_ref):
    o_ref[...] = (a_ref[...].astype(F32) + b_ref[...].astype(F32)).astype(o_ref.dtype)


def add_pairs(a, b):
    n, rows, cols = a.shape
    tr = _pick(rows, 512, 2 * SUBLANES)
    spec = pl.BlockSpec((None, tr, cols), lambda k, i: (k, i, 0))
    return pl.pallas_call(
        functools.partial(_add_pairs_body),
        name="add_pairs",
        grid=(n, rows // tr),
        in_specs=[spec, spec],
        out_specs=spec,
        out_shape=jax.ShapeDtypeStruct(a.shape, a.dtype),
        compiler_params=_cparams("parallel", "parallel"),
    )(a, b)


def reduce_scatter(parts):
    got = swap_cores(parts, "exchange_grads_cores")
    own = lax.dynamic_index_in_dim(parts, lax.axis_index("c"), axis=0, keepdims=False)
    return sum_leading(exchange_chips(add_pairs(own, got), "exchange_grads_chips"))


PACK_COLS = 1024
PACK_ROW_MULTIPLE = 512


def _pack(arrays, dtype):
    parts = []
    rows = 0
    for a in arrays:
        flat = a.reshape(-1).astype(dtype)
        n_rows = -(-flat.shape[0] // PACK_COLS)
        parts.append(jnp.pad(flat, (0, n_rows * PACK_COLS - flat.shape[0])).reshape(n_rows, PACK_COLS))
        rows += n_rows
    pad_rows = -rows % PACK_ROW_MULTIPLE
    if pad_rows:
        parts.append(jnp.zeros((pad_rows, PACK_COLS), dtype))
    return jnp.concatenate(parts, axis=0)


def _unpack(flat, shapes):
    lead = flat.shape[:-2]
    out = []
    row = 0
    for shape in shapes:
        size = int(np.prod(shape))
        n_rows = -(-size // PACK_COLS)
        seg = flat[..., row:row + n_rows, :].reshape(lead + (n_rows * PACK_COLS,))[..., :size]
        out.append(seg.reshape(lead + tuple(shape)))
        row += n_rows
    return out


W_IN_SMALL = 4608


def _split_w_in(w_in):
    d = w_in.shape[0]
    small = jnp.concatenate(
        [w_in[:, 0:2560], w_in[:, 2576:4304], w_in[:, 2560:2576], jnp.zeros((d, W_IN_SMALL - 4304), w_in.dtype)], axis=1)
    return small, w_in[:, 4304:]


def _s5_mixer(u, lam_re, lam_im, log_step, b_re, b_im, c_re, c_im, d_skip, w_glu, b_glu):
    g, p, hg = S5_GROUPS, S5_STATE, S5_GROUP
    lam_re = jnp.minimum(lam_re, -1e-4)
    dt = jnp.exp(log_step)[..., None]
    mag = jnp.exp(lam_re * dt)
    abar_r = mag * jnp.cos(lam_im * dt)
    abar_i = mag * jnp.sin(lam_im * dt)
    den = lam_re * lam_re + lam_im * lam_im
    xr = abar_r - 1.0
    xi = abar_i
    coef_r = (xr * lam_re + xi * lam_im) / den
    coef_i = (xi * lam_re - xr * lam_im) / den
    bbar_r = coef_r[..., None] * b_re - coef_i[..., None] * b_im
    bbar_i = coef_r[..., None] * b_im + coef_i[..., None] * b_re
    eye = jnp.eye(g, dtype=F32)

    def block_diag(t, rows, cols):
        return (eye[:, None, :, None] * t[:, :, None, :]).reshape(g * rows, g * cols)

    b_all = jnp.concatenate(
        [block_diag(t.transpose(0, 2, 1), hg, p) for t in (bbar_r[0], bbar_i[0], bbar_r[1], bbar_i[1])], axis=1)
    c_all = jnp.concatenate(
        [block_diag(t.transpose(0, 2, 1), p, hg) for t in (c_re[0], -c_im[0], c_re[1], -c_im[1])], axis=0)
    a_all = jnp.stack([abar_r.reshape(2, g * p), abar_i.reshape(2, g * p)], axis=1)
    s = s5_scan(a_all, mm(u, b_all))
    y = mm(s, c_all) + d_skip * u
    y = jax.nn.gelu(y)
    return y * jax.nn.sigmoid(wmm(y, *w_glu) + b_glu)


def _gdn_mixer(qkv, z, beta_logits, decay_logits, conv_w, a_log, dt_bias, o_gain):
    seq = qkv.shape[0]
    h, dh, c = GDN_HEADS, GDN_HEAD_DIM, GDN_CHUNK
    nc = seq // c
    padded = jnp.pad(qkv, ((GDN_CONV // 2, GDN_CONV - 1 - GDN_CONV // 2), (0, 0)))
    conv = sum(padded[j:j + seq] * conv_w[j] for j in range(GDN_CONV))
    q, k, v = jnp.split(jax.nn.silu(conv), 3, axis=-1)

    def l2(t):
        t = t.reshape(seq, h, dh)
        return (t * lax.rsqrt(jnp.sum(t * t, axis=-1, keepdims=True) + 1e-6)).reshape(seq, h * dh)

    q = l2(q) * (dh**-0.5)
    k = l2(k)
    beta = jax.nn.sigmoid(beta_logits).reshape(seq, 2, h)
    g = -jnp.exp(a_log) * jax.nn.softplus(decay_logits.reshape(seq, 2, h) + dt_bias)
    small = []
    for d in range(2):
        gcs = lax.cumsum(g[:, d].reshape(nc, c, h), axis=1, reverse=d == 1).transpose(2, 0, 1)
        small += [gcs.reshape(h, nc, 1, c), beta[:, d].T.reshape(h, nc, 1, c)]
    o_fwd, o_bwd = gdn_delta_rule(q, k, v, *small)
    o = rmsnorm((o_fwd + o_bwd).reshape(seq * h, dh), o_gain).reshape(seq, h * dh)
    return o * jax.nn.silu(z)


def _t5_bucket(rel):
    nb = T5_BUCKETS // 2
    max_exact = nb // 2
    ret = jnp.where(rel > 0, nb, 0)
    n = jnp.abs(rel)
    nf = jnp.maximum(n, 1).astype(F32)
    large = max_exact + (jnp.log(nf / max_exact) / math.log(T5_MAX_DISTANCE / max_exact) * (nb - max_exact)).astype(jnp.int32)
    large = jnp.minimum(large, nb - 1)
    return ret + jnp.where(n < max_exact, n, large)


def _swa_mixer(q, kv, sink, t5_bias):
    seq = q.shape[0]
    kvh, g, d, blk = SWA_KV_HEADS, SWA_GROUP, SWA_HEAD_DIM, SWA_BLOCK
    q4 = q.reshape(seq, kvh, g, d).transpose(1, 2, 0, 3)
    k, v = jnp.split(kv, 2, axis=-1)
    heads_first = lambda t: t.reshape(seq, kvh, d).transpose(1, 0, 2)
    rel = jnp.arange(3 * blk)[None, :] - blk - jnp.arange(blk)[:, None]
    onehot = (_t5_bucket(rel)[..., None] == jnp.arange(T5_BUCKETS)).astype(F32)
    bias = jnp.einsum("qsb,bh->hqs", onehot, t5_bias, precision=lax.Precision.HIGHEST).reshape(kvh, g * blk, 3 * blk)
    sink_col = jnp.broadcast_to(sink.reshape(kvh, g, 1, 1), (kvh, g, blk, 1)).reshape(kvh, g * blk, 1)
    o = swa_attention(q4, heads_first(k), heads_first(v), bias, sink_col)
    return o.transpose(2, 0, 1, 3).reshape(seq, kvh * g * d)


def _rope(t, cos, sin):
    t1, t2 = jnp.split(t, 2, axis=-1)
    return jnp.concatenate([t1 * cos - t2 * sin, t2 * cos + t1 * sin], axis=-1)


def _mla_mixer(c_q, c_kv, k_rope, q_gain, kv_gain, w_uq, w_ukv):
    seq = c_q.shape[0]
    h = MLA_HEADS
    q = wmm(rmsnorm(c_q, q_gain), *w_uq).reshape(seq, h, MLA_NOPE + MLA_ROPE)
    kv = wmm(rmsnorm(c_kv, kv_gain), *w_ukv).reshape(seq, h, MLA_NOPE + MLA_V)
    q_nope, q_pe = q[..., :MLA_NOPE], q[..., MLA_NOPE:]
    k_nope, v = kv[..., :MLA_NOPE], kv[..., MLA_NOPE:]
    pos = jnp.arange(seq, dtype=F32)
    inv_freq = ROPE_THETA ** (-jnp.arange(0, MLA_ROPE, 2, dtype=F32) / MLA_ROPE)
    ang = pos[:, None] * inv_freq[None, :]
    cos, sin = jnp.cos(ang)[:, None, :], jnp.sin(ang)[:, None, :]
    q_pe = _rope(q_pe, cos, sin)
    k_pe = _rope(k_rope[:, None, :], cos, sin)
    qf = jnp.concatenate([q_nope, q_pe], axis=-1)
    kf = jnp.concatenate([k_nope, jnp.broadcast_to(k_pe, (seq, h, MLA_ROPE))], axis=-1)
    o = mla_attention(qf.transpose(1, 0, 2), kf.transpose(1, 0, 2), v.transpose(1, 0, 2))
    return o.transpose(1, 0, 2).reshape(seq, h * MLA_V)


def _layer(x, p, t5_bias):
    d = x.shape[1]
    h = rmsnorm(x, p["mix_pre_gain"])
    (w_small, w_gate), (w_small_bf16, w_gate_bf16) = (_split_w_in(t) for t in p["w_in"])
    ps = wmm(h, w_small, w_small_bf16)
    gate_logits = wmm(h, w_gate, w_gate_bf16)
    y_a = _s5_mixer(ps[:, 0:512], p["s5_lam_re"], p["s5_lam_im"], p["s5_log_step"], p["s5_b_re"], p["s5_b_im"],
                    p["s5_c_re"], p["s5_c_im"], p["s5_d"], p["s5_w_glu"], p["s5_b_glu"])
    y_b = _gdn_mixer(ps[:, 512:2048], ps[:, 2048:2560], ps[:, 4288:4296], ps[:, 4296:4304], p["gdn_conv"],
                     p["gdn_a_log"], p["gdn_dt_bias"], p["gdn_o_gain"])
    y_c = _swa_mixer(ps[:, 2560:3072], ps[:, 3072:3328], p["swa_sink"], t5_bias)
    y_d = _mla_mixer(ps[:, 3328:3712], ps[:, 3712:4224], ps[:, 4224:4288], p["mla_q_gain"], p["mla_kv_gain"],
                     p["mla_w_uq"], p["mla_w_ukv"])
    merged = sum(jax.nn.sigmoid(gate_logits[:, b * d:(b + 1) * d]) * wmm(y, p["w_branch"][0][b], p["w_branch"][1][b])
                 for b, y in enumerate((y_a, y_b, y_c, y_d)))
    x = x + rmsnorm(wmm(merged, *p["w_out"]), p["mix_post_gain"])
    h = rmsnorm(x, p["mlp_pre_gain"])
    f = wmm(jnp.square(jax.nn.relu(wmm(h, *p["w_mlp_in"]))), *p["w_mlp_out"])
    return x + rmsnorm(f, p["mlp_post_gain"])


LAYER_WEIGHTS = ("w_in", "s5_lam_re", "s5_lam_im", "s5_log_step", "s5_b_re", "s5_b_im", "s5_c_re", "s5_c_im", "s5_d",
                 "s5_w_glu", "s5_b_glu", "gdn_conv", "gdn_a_log", "gdn_dt_bias", "gdn_o_gain", "swa_sink", "mla_q_gain",
                 "mla_kv_gain", "mla_w_uq", "mla_w_ukv", "w_branch", "w_out", "mix_pre_gain", "mix_post_gain",
                 "mlp_pre_gain", "mlp_post_gain", "w_mlp_in", "w_mlp_out")


def _forward(x, weights, values):
    for layer in range(DEPTH):
        p = {n: (weights[n][layer], values[n][layer]) if n in values else weights[n][layer] for n in LAYER_WEIGHTS}
        x = _layer(x, p, weights["t5_bias"])
    return x


WEIGHT_NAMES = ("w_in", "s5_lam_re", "s5_lam_im", "s5_log_step", "s5_b_re", "s5_b_im", "s5_c_re", "s5_c_im", "s5_d",
                "s5_w_glu", "s5_b_glu", "gdn_conv", "gdn_a_log", "gdn_dt_bias", "gdn_o_gain", "swa_sink", "t5_bias",
                "mla_q_gain", "mla_kv_gain", "mla_w_uq", "mla_w_ukv", "w_branch", "w_out", "mix_pre_gain", "mix_post_gain",
                "mlp_pre_gain", "mlp_post_gain", "w_mlp_in", "w_mlp_out")
SHARD_AXIS = {"w_in": 2, "s5_w_glu": 1, "mla_w_uq": 2, "mla_w_ukv": 2, "w_branch": 3, "w_out": 1, "w_mlp_in": 2,
              "w_mlp_out": 1}
CONV = "gdn_conv"
CONV_AXIS = 2
REPLICATED = tuple(n for n in WEIGHT_NAMES if n not in SHARD_AXIS and n != CONV)


def _step(x, target, w, m, v):
    big = tuple(SHARD_AXIS)
    x_pos, y_pos, c_pos = _mesh_position()
    my = 4 * x_pos + 2 * y_pos + c_pos

    shard_shapes = [w[n].shape for n in big]
    gathered = _unpack(all_gather(_pack([w[n] for n in big], BF16), "gather_weights"), shard_shapes)
    values = {n: jnp.concatenate([g[k] for k in range(N_DEV)], axis=SHARD_AXIS[n]) for n, g in zip(big, gathered)}
    full = {n: t.astype(F32) for n, t in values.items()}
    conv_all = _unpack(all_gather(_pack([w[CONV]], F32), "gather_conv"), [w[CONV].shape])[0]
    full[CONV] = jnp.concatenate([conv_all[k] for k in range(N_DEV)], axis=CONV_AXIS)
    for n in REPLICATED:
        full[n] = w[n]

    y, vjp = jax.vjp(lambda x_, full_: _forward(x_, full_, values), x, full)
    loss_rows, dy = loss_head(y, target)
    grad_x, grad_full = vjp(dy)
    loss_part = jnp.sum(loss_rows)

    def pieces(n, k):
        size = w[n].shape[SHARD_AXIS[n]]
        return lax.slice_in_dim(grad_full[n], k * size, (k + 1) * size, axis=SHARD_AXIS[n])

    send = jnp.stack([jnp.stack([_pack([pieces(n, 2 * chip + core) for n in big], BF16) for chip in range(N_CHIPS)])
                      for core in range(2)])
    grad_big = reduce_scatter(send)
    small_names = REPLICATED + (CONV,)
    small = [grad_full[n] for n in small_names] + [loss_part.reshape(1)]
    small_sum = sum_leading(all_gather(_pack(small, F32), "gather_small_grads"))
    small_grads = _unpack(small_sum, [a.shape for a in small])
    loss = small_grads[-1][0]
    grads = dict(zip(small_names, small_grads[:-1]))
    conv_size = w[CONV].shape[CONV_AXIS]
    grads[CONV] = lax.dynamic_slice_in_dim(grads[CONV], my * conv_size, conv_size, axis=CONV_AXIS)
    grads.update(zip(big, _unpack(grad_big, shard_shapes)))

    small_shapes = [w[n].shape for n in small_names]
    packed = lambda t: _pack([t[n] for n in small_names], F32)
    delta, new_m, new_v = (dict(zip(small_names, _unpack(t, small_shapes)))
                           for t in adamw_flat(packed(w), packed(grads), packed(m), packed(v)))
    for n in big:
        rows2d = lambda t: t.reshape(-1, t.shape[-1])
        outs = adamw_flat(rows2d(w[n]), rows2d(grads[n]), rows2d(m[n]), rows2d(v[n]))
        delta[n], new_m[n], new_v[n] = (t.reshape(w[n].shape) for t in outs)
    return loss, grad_x, grads, delta, new_m, new_v


def kernel(x, w_in, s5_lam_re, s5_lam_im, s5_log_step, s5_b_re, s5_b_im, s5_c_re, s5_c_im, s5_d, s5_w_glu, s5_b_glu, gdn_conv, gdn_a_log, gdn_dt_bias, gdn_o_gain, swa_sink, t5_bias, mla_q_gain, mla_kv_gain, mla_w_uq, mla_w_ukv, w_branch, w_out, mix_pre_gain, mix_post_gain, mlp_pre_gain, mlp_post_gain, w_mlp_in, w_mlp_out, loss_target, m_w_in, m_s5_lam_re, m_s5_lam_im, m_s5_log_step, m_s5_b_re, m_s5_b_im, m_s5_c_re, m_s5_c_im, m_s5_d, m_s5_w_glu, m_s5_b_glu, m_gdn_conv, m_gdn_a_log, m_gdn_dt_bias, m_gdn_o_gain, m_swa_sink, m_t5_bias, m_mla_q_gain, m_mla_kv_gain, m_mla_w_uq, m_mla_w_ukv, m_w_branch, m_w_out, m_mix_pre_gain, m_mix_post_gain, m_mlp_pre_gain, m_mlp_post_gain, m_w_mlp_in, m_w_mlp_out, v_w_in, v_s5_lam_re, v_s5_lam_im, v_s5_log_step, v_s5_b_re, v_s5_b_im, v_s5_c_re, v_s5_c_im, v_s5_d, v_s5_w_glu, v_s5_b_glu, v_gdn_conv, v_gdn_a_log, v_gdn_dt_bias, v_gdn_o_gain, v_swa_sink, v_t5_bias, v_mla_q_gain, v_mla_kv_gain, v_mla_w_uq, v_mla_w_ukv, v_w_branch, v_w_out, v_mix_pre_gain, v_mix_post_gain, v_mlp_pre_gain, v_mlp_post_gain, v_w_mlp_in, v_w_mlp_out):
    args = locals()
    w = {n: args[n] for n in WEIGHT_NAMES}
    m = {n: args["m_" + n] for n in WEIGHT_NAMES}
    v = {n: args["v_" + n] for n in WEIGHT_NAMES}
    loss, grad_x, grads, delta, new_m, new_v = _step(x[0], loss_target[0], w, m, v)
    return (loss, grad_x[None], *[grads[n] for n in WEIGHT_NAMES], *[delta[n] for n in WEIGHT_NAMES],
            *[new_m[n] for n in WEIGHT_NAMES], *[new_v[n] for n in WEIGHT_NAMES])
```
